```python
import jax
import jax.numpy as jnp
from jax import lax
import numpy as np

D_MODEL = 1024
BATCH = 2
SEQ = 8192
DEPTH = 4

GRID_W = 64
CTX_LEN = 256
N_MIXERS = 3
EXPAND = 2
D_INNER = EXPAND * D_MODEL
HEAD_DIM = 64
N_HEADS = D_INNER // HEAD_DIM
DECAY_LORA = 64
ICLR_LORA = 64
CONV_WIDTH = 31
NA_ROWS = 8
NA_COLS = 16
RMS_EPS = 1e-6
LN_EPS = 1e-5
GN_EPS = 6.4e-4
N_LAYERS_A = (DEPTH + 2) // N_MIXERS
N_LAYERS_B = (DEPTH + 1) // N_MIXERS
N_LAYERS_C = DEPTH // N_MIXERS

kernel_name = 'hybrid_rwkv7_conformer_natten_dit'


def _rmsnorm(x, g):
    xf = x.astype(jnp.float32)
    xf = xf * lax.rsqrt(jnp.mean(xf * xf, axis=-1, keepdims=True) + RMS_EPS)
    return (xf * g.astype(jnp.float32)).astype(x.dtype)


def _layernorm(x, w, b):
    xf = x.astype(jnp.float32)
    mu = jnp.mean(xf, axis=-1, keepdims=True)
    var = jnp.mean(jnp.square(xf - mu), axis=-1, keepdims=True)
    y = (xf - mu) * lax.rsqrt(var + LN_EPS) * w.astype(jnp.float32) + b.astype(jnp.float32)
    return y.astype(x.dtype)


def _split_heads(t):
    return t.reshape(t.shape[:-1] + (N_HEADS, HEAD_DIM))


def _token_shift(h):
    zero = jnp.zeros_like(h[:, :1])
    prev = jnp.concatenate([zero, h[:, :-1]], axis=1)
    nxt = jnp.concatenate([h[:, 1:], zero], axis=1)
    return 0.5 * (prev + nxt) - h


def _rwkv_prep(h, mu, w_in, w0, w1, w2, a0, a1, a2, k_k, k_a):
    xs = h[None] + _token_shift(h)[None] * mu[:, None, None, :]
    r, k, v, g = jnp.einsum('ibtd,ide->ibte', xs[:4], w_in)
    w_raw = w0[:, None, None, :] + jnp.einsum(
        'zbtl,zle->zbte', jnp.tanh(jnp.einsum('btd,zdl->zbtl', xs[4], w1)), w2)
    decay = jnp.exp(-jnp.exp(-jax.nn.softplus(-w_raw.astype(jnp.float32)) - 0.5))
    a = jax.nn.sigmoid(a0[:, None, None, :] + jnp.einsum(
        'zbtl,zle->zbte', jnp.einsum('btd,zdl->zbtl', xs[5], a1), a2))
    kkh = _split_heads(k * k_k).astype(jnp.float32)
    kk = (kkh * lax.rsqrt(jnp.sum(kkh * kkh, axis=-1, keepdims=True) + 1e-12)).reshape(k.shape)
    k_dir = k[None] * (1.0 + (a - 1.0) * k_a)
    return r, v, kk, k_dir, decay, a, g


def _wkv_bidir(r, v, kk, k_dir, decay, a, state0):
    B, T, _ = r.shape

    def shared(t):
        return jnp.stack([t, t[:, ::-1]])

    def per_dir(t):
        return jnp.stack([t[0], t[1][:, ::-1]])

    def time_major(t):
        return t.astype(jnp.float32).reshape(2, B, T, N_HEADS, HEAD_DIM).transpose(2, 0, 1, 3, 4)

    xs = (time_major(shared(r)), time_major(per_dir(decay)), time_major(per_dir(k_dir)),
          time_major(shared(v)), time_major(shared(kk)), time_major(per_dir(a)))

    def step(S, inp):
        r_t, w_t, k_t, v_t, kk_t, a_t = inp
        s_kk = jnp.einsum('zbhvk,zbhk->zbhv', S, kk_t)
        S = (S * w_t[..., None, :] - s_kk[..., None] * (kk_t * a_t)[..., None, :]
             + v_t[..., None] * k_t[..., None, :])
        return S, jnp.einsum('zbhvk,zbhk->zbhv', S, r_t)

    state, ys = lax.scan(step, state0, xs)
    ys = ys.transpose(1, 2, 0, 3, 4)
    y = ys[0] + ys[1][:, ::-1]
    return y.reshape(B, T, D_INNER).astype(r.dtype), state


def _rwkv_out(y, r, v, k_dir, g, r_k, ln_w, ln_b, w_out):
    yh = _split_heads(y).astype(jnp.float32)
    mu = jnp.mean(yh, axis=-1, keepdims=True)
    var = jnp.mean(jnp.square(yh - mu), axis=-1, keepdims=True)
    yn = ((yh - mu) * lax.rsqrt(var + GN_EPS)).reshape(y.shape) * ln_w + ln_b
    bonus = jnp.sum(_split_heads(r * (k_dir[0] + k_dir[1])) * r_k, axis=-1, keepdims=True) * _split_heads(v)
    o = (yn.astype(y.dtype) + bonus.reshape(y.shape)) * jax.nn.silu(g)
    return o @ w_out


def _rwkv_mixer(hx, hc, need_ctx, mu, w_in, w0, w1, w2, a0, a1, a2, k_k, k_a, r_k, ln_w, ln_b, w_out):
    B = hx.shape[0]
    rc, vc, kkc, kdc, wc, ac, gc = _rwkv_prep(hc, mu, w_in, w0, w1, w2, a0, a1, a2, k_k, k_a)
    state0 = jnp.zeros((2, B, N_HEADS, HEAD_DIM, HEAD_DIM), jnp.float32)
    yc, state_c = _wkv_bidir(rc, vc, kkc, kdc, wc, ac, state0)
    rx, vx, kkx, kdx, wx, ax, gx = _rwkv_prep(hx, mu, w_in, w0, w1, w2, a0, a1, a2, k_k, k_a)
    yx, _ = _wkv_bidir(rx, vx, kkx, kdx, wx, ax, state_c)
    out_x = _rwkv_out(yx, rx, vx, kdx, gx, r_k, ln_w, ln_b, w_out)
    out_c = _rwkv_out(yc, rc, vc, kdc, gc, r_k, ln_w, ln_b, w_out) if need_ctx else None
    return out_x, out_c


def _conv_module(h, w_in, b_in, dw, dw_b, ln_w, ln_b, w_out, b_out):
    u, u_glu, gate = jnp.split(h @ w_in + b_in, 3, axis=-1)
    u = u * jax.nn.sigmoid(u_glu)
    u = lax.conv_general_dilated(u, dw[:, None, :], window_strides=(1,),
                                 padding=[(CONV_WIDTH // 2, CONV_WIDTH // 2)],
                                 dimension_numbers=('NWC', 'WIO', 'NWC'),
                                 feature_group_count=D_INNER) + dw_b
    u = jax.nn.silu(_layernorm(u, ln_w, ln_b)) * jax.nn.silu(gate)
    return u @ w_out + b_out


def _conv_mixer(hx, hc, need_ctx, w_in, b_in, dw, dw_b, ln_w, ln_b, w_out, b_out):
    out_x = _conv_module(hx, w_in, b_in, dw, dw_b, ln_w, ln_b, w_out, b_out)
    out_c = _conv_module(hc, w_in, b_in, dw, dw_b, ln_w, ln_b, w_out, b_out) if need_ctx else None
    return out_x, out_c


def _na_mixer(hx, hc, need_ctx, w_in, rpb, w_out):
    B, T, _ = hx.shape
    rows = T // GRID_W
    kr = min(NA_ROWS, rows)
    n_loc = kr * NA_COLS
    scale = HEAD_DIM ** -0.5
    q, k, v, g = jnp.split(hx @ w_in, 4, axis=-1)
    kc, vc = jnp.split(hc @ w_in[:, D_INNER:3 * D_INNER], 2, axis=-1)
    qg = q.reshape(B, rows, GRID_W, N_HEADS, HEAD_DIM)
    kg = k.reshape(B, rows, GRID_W, N_HEADS, HEAD_DIM)
    vg = v.reshape(B, rows, GRID_W, N_HEADS, HEAD_DIM)
    kch, vch = _split_heads(kc), _split_heads(vc)
    cols = jnp.arange(GRID_W)
    col_idx = jnp.clip(cols - NA_COLS // 2, 0, GRID_W - NA_COLS)[:, None] + jnp.arange(NA_COLS)
    rel_c = col_idx - cols[:, None] + NA_COLS - 1

    def row_block(r):
        start = jnp.clip(r - kr // 2, 0, rows - kr)
        k_win = lax.dynamic_slice_in_dim(kg, start, kr, axis=1)[:, :, col_idx]
        v_win = lax.dynamic_slice_in_dim(vg, start, kr, axis=1)[:, :, col_idx]
        q_r = lax.dynamic_index_in_dim(qg, r, axis=1, keepdims=False)
        rel_r = start + jnp.arange(kr) - r + NA_ROWS - 1
        bias = rpb[:, rel_r[None, :, None], rel_c[:, None, :]]
        s_loc = jnp.einsum('bqhd,brqjhd->bhqrj', q_r, k_win).astype(jnp.float32) * scale + bias.astype(jnp.float32)
        s_ctx = jnp.einsum('bqhd,bchd->bhqc', q_r, kch).astype(jnp.float32) * scale
        p = jax.nn.softmax(jnp.concatenate([s_loc.reshape(B, N_HEADS, GRID_W, n_loc), s_ctx], axis=-1), axis=-1)
        p = p.astype(v.dtype)
        p_loc = p[..., :n_loc].reshape(B, N_HEADS, GRID_W, kr, NA_COLS)
        return (jnp.einsum('bhqrj,brqjhd->bqhd', p_loc, v_win)
                + jnp.einsum('bhqc,bchd->bqhd', p[..., n_loc:], vch))

    o = lax.map(row_block, jnp.arange(rows))
    o = o.transpose(1, 0, 2, 3, 4).reshape(B, T, D_INNER)
    out_x = (o * jax.nn.silu(g)) @ w_out
    out_c = None
    if need_ctx:
        qc = _split_heads(hc @ w_in[:, :D_INNER])
        gc = hc @ w_in[:, 3 * D_INNER:]
        s = jnp.einsum('bqhd,bkhd->bhqk', qc, kch).astype(jnp.float32) * scale
        p = jax.nn.softmax(s, axis=-1).astype(vc.dtype)
        oc = jnp.einsum('bhqk,bkhd->bqhd', p, vch).reshape(hc.shape[:2] + (D_INNER,))
        out_c = (oc * jax.nn.silu(gc)) @ w_out
    return out_x, out_c


def setup_inputs(seed: int = 0) -> dict:
    key = jax.random.key(seed)
    ks = iter(jax.random.split(key, 40))

    def nrm(shape, scale):
        return scale * jax.random.normal(next(ks), shape, jnp.float32)

    def uni(shape, lo, hi):
        return jax.random.uniform(next(ks), shape, jnp.float32, lo, hi)

    D, E, H = D_MODEL, D_INNER, N_HEADS
    nA, nB, nC = N_LAYERS_A, N_LAYERS_B, N_LAYERS_C
    return {
        'x': nrm((BATCH, SEQ, D), 1.0),
        'c': nrm((BATCH, D), 1.0),
        'ctx': nrm((BATCH, CTX_LEN, D), 1.0),
        'c_ctx': nrm((D,), 1.0),
        'ada_w': nrm((DEPTH, D, 3 * D), D ** -0.5),
        'ada_b': nrm((DEPTH, 3 * D), 0.02),
        'norm_pre': 1.0 + nrm((DEPTH, D), 0.02),
        'norm_post': 1.0 + nrm((DEPTH, D), 0.02),
        'rw_mu': uni((nA, 6, D), 0.0, 1.0),
        'rw_w_in': nrm((nA, 4, D, E), D ** -0.5),
        'rw_w0': uni((nA, 2, E), -6.0, 1.0),
        'rw_w1': nrm((nA, 2, D, DECAY_LORA), D ** -0.5),
        'rw_w2': nrm((nA, 2, DECAY_LORA, E), 0.1 * DECAY_LORA ** -0.5),
        'rw_a0': nrm((nA, 2, E), 0.1),
        'rw_a1': nrm((nA, 2, D, ICLR_LORA), D ** -0.5),
        'rw_a2': nrm((nA, 2, ICLR_LORA, E), 0.1 * ICLR_LORA ** -0.5),
        'rw_k_k': 0.85 + nrm((nA, E), 0.02),
        'rw_k_a': 1.0 + nrm((nA, E), 0.02),
        'rw_r_k': nrm((nA, H, HEAD_DIM), 0.1),
        'rw_ln_w': 1.0 + nrm((nA, E), 0.02),
        'rw_ln_b': nrm((nA, E), 0.02),
        'rw_w_out': nrm((nA, E, D), E ** -0.5),
        'cf_w_in': nrm((nB, D, 3 * E), D ** -0.5),
        'cf_b_in': nrm((nB, 3 * E), 0.02),
        'cf_dw': nrm((nB, CONV_WIDTH, E), CONV_WIDTH ** -0.5),
        'cf_dw_b': nrm((nB, E), 0.02),
        'cf_ln_w': 1.0 + nrm((nB, E), 0.02),
        'cf_ln_b': nrm((nB, E), 0.02),
        'cf_w_out': nrm((nB, E, D), E ** -0.5),
        'cf_b_out': nrm((nB, D), 0.02),
        'na_w_in': nrm((nC, D, 4 * E), D ** -0.5),
        'na_rpb': nrm((nC, H, 2 * NA_ROWS - 1, 2 * NA_COLS - 1), 0.02),
        'na_w_out': nrm((nC, E, D), E ** -0.5),
    }


def reference(x, c, ctx, c_ctx, ada_w, ada_b, norm_pre, norm_post,
              rw_mu, rw_w_in, rw_w0, rw_w1, rw_w2, rw_a0, rw_a1, rw_a2, rw_k_k, rw_k_a, rw_r_k,
              rw_ln_w, rw_ln_b, rw_w_out,
              cf_w_in, cf_b_in, cf_dw, cf_dw_b, cf_ln_w, cf_ln_b, cf_w_out, cf_b_out,
              na_w_in, na_rpb, na_w_out):
    ctx_stream = ctx
    for i in range(DEPTH):
        kind = i % N_MIXERS
        j = i // N_MIXERS
        need_ctx = i < DEPTH - 1
        shift, scale, gate = jnp.split(jax.nn.silu(c) @ ada_w[i] + ada_b[i], 3, axis=-1)
        shift_c, scale_c, gate_c = jnp.split(jax.nn.silu(c_ctx) @ ada_w[i] + ada_b[i], 3, axis=-1)
        hx = _rmsnorm(x, norm_pre[i]) * (1.0 + scale[:, None, :]) + shift[:, None, :]
        hc = _rmsnorm(ctx_stream, norm_pre[i]) * (1.0 + scale_c) + shift_c
        if kind == 0:
            ox, oc = _rwkv_mixer(hx, hc, need_ctx, rw_mu[j], rw_w_in[j], rw_w0[j], rw_w1[j], rw_w2[j],
                                 rw_a0[j], rw_a1[j], rw_a2[j], rw_k_k[j], rw_k_a[j], rw_r_k[j],
                                 rw_ln_w[j], rw_ln_b[j], rw_w_out[j])
        elif kind == 1:
            ox, oc = _conv_mixer(hx, hc, need_ctx, cf_w_in[j], cf_b_in[j], cf_dw[j], cf_dw_b[j],
                                 cf_ln_w[j], cf_ln_b[j], cf_w_out[j], cf_b_out[j])
        else:
            ox, oc = _na_mixer(hx, hc, need_ctx, na_w_in[j], na_rpb[j], na_w_out[j])
        x = x + gate[:, None, :] * _rmsnorm(ox, norm_post[i])
        if need_ctx:
            ctx_stream = ctx_stream + gate_c * _rmsnorm(oc, norm_post[i])
    return x
```

```python
import functools
import math

import jax
import jax.numpy as jnp
import numpy as np
from jax import lax
from jax.experimental import pallas as pl
from jax.experimental.pallas import tpu as pltpu

F32 = jnp.float32
BF16 = jnp.bfloat16

D_MODEL = 1024
D_INNER = 2048
N_HEADS = 32
HEAD_DIM = 64
LORA = 64
GRID_W = 64
CONV_WIDTH = 31
NA_ROWS = 8
NA_COLS = 16
RMS_EPS = 1e-6
LN_EPS = 1e-5
GN_EPS = 6.4e-4

LANES = 128
GROUP_W = 256
N_GROUPS = D_INNER // GROUP_W
HEADS_PER_GROUP = GROUP_W // HEAD_DIM
CHUNK = 64
CONV_HALO = 16
NEG_BIG = -1e30
VMEM_LIMIT = 56 * 1024 * 1024


def _cparams(sem):
    return pltpu.CompilerParams(dimension_semantics=sem, vmem_limit_bytes=VMEM_LIMIT)


def _dot(a, b):
    return jnp.dot(a, b, preferred_element_type=F32)


def _dot_t(a, b):
    return lax.dot_general(a, b, (((1,), (1,)), ((), ())), preferred_element_type=F32)


def _split_bf16(x):
    hi = x.astype(BF16)
    lo = (x - hi.astype(F32)).astype(BF16)
    return hi, lo


def _sigmoid(x):
    return 1.0 / (1.0 + jnp.exp(-x))


def _silu(x):
    return x * _sigmoid(x)


def _modnorm(x, g, s1p, sh):
    ms = jnp.mean(x * x, axis=-1, keepdims=True)
    return (x * lax.rsqrt(ms + RMS_EPS) * g) * s1p + sh


def _post(xres, ox, gate, gpost):
    ms = jnp.mean(ox * ox, axis=-1, keepdims=True)
    return xres + gate * (ox * lax.rsqrt(ms + RMS_EPS) * gpost)


def _head_sum(x, bd):
    outs = []
    for s in range(x.shape[1] // LANES):
        hi, lo = _split_bf16(x[:, s * LANES:(s + 1) * LANES])
        outs.append(_dot(hi, bd) + _dot(lo, bd))
    return jnp.concatenate(outs, axis=1) if len(outs) > 1 else outs[0]


def _mod_kernel(c_ref, w_ref, b_ref, o_ref):
    a_hi, a_lo = _split_bf16(_silu(c_ref[...]))
    w_hi, w_lo = _split_bf16(w_ref[...])
    o_ref[...] = _dot(a_hi, w_hi) + _dot(a_hi, w_lo) + _dot(a_lo, w_hi) + b_ref[...]


def _modulation(cc, ada_w, ada_b):
    depth = ada_w.shape[0]
    tn = 1024
    return pl.pallas_call(
        _mod_kernel,
        grid=(depth, 3 * D_MODEL // tn),
        in_specs=[pl.BlockSpec((8, D_MODEL), lambda l, j: (0, 0)),
                  pl.BlockSpec((None, D_MODEL, tn), lambda l, j: (l, 0, j)),
                  pl.BlockSpec((None, 1, tn), lambda l, j: (l, 0, j))],
        out_specs=pl.BlockSpec((None, 8, tn), lambda l, j: (l, 0, j)),
        out_shape=jax.ShapeDtypeStruct((depth, 8, 3 * D_MODEL), F32),
        name="adaln_mod",
        compiler_params=_cparams(("parallel", "parallel")),
    )(cc, ada_w, ada_b.reshape(depth, 1, 3 * D_MODEL))


def _inproj_kernel(ns, has_bias, epi, x_ref, g_ref, s_ref, sh_ref, *rest):
    w_refs = rest[:ns]
    rest = rest[ns:]
    b_refs = rest[:ns] if has_bias else None
    rest = rest[ns:] if has_bias else rest
    out_refs, hn_ref = rest[:-1], rest[-1]

    @pl.when(pl.program_id(2) == 0)
    def _():
        hn_ref[...] = _modnorm(x_ref[...], g_ref[...], s_ref[...], sh_ref[...]).astype(BF16)

    hn = hn_ref[...]
    ys = []
    for s in range(ns):
        y = _dot(hn, w_refs[s][...])
        if has_bias:
            y = y + b_refs[s][...]
        ys.append(y)
    for o_ref, o in zip(out_refs, epi(ys)):
        o_ref[...] = o.astype(o_ref.dtype)


def _inproj(x2, nb, g_pre, s1p, sh, w_bf, bias, ns, n_out, epi, tm, tn=512):
    rows = x2.shape[0]
    nt = rows // nb // tm
    nj = D_INNER // tn
    row_map = lambda b, i, j: (b * nt + i, 0)
    vec_map = lambda b, i, j: (b, 0, 0)
    in_specs = [pl.BlockSpec((tm, D_MODEL), row_map),
                pl.BlockSpec((1, D_MODEL), lambda b, i, j: (0, 0)),
                pl.BlockSpec((None, 1, D_MODEL), vec_map),
                pl.BlockSpec((None, 1, D_MODEL), vec_map)]
    args = [x2, g_pre, s1p, sh]
    for s in range(ns):
        in_specs.append(pl.BlockSpec((D_MODEL, tn), functools.partial(lambda s, b, i, j: (0, s * nj + j), s)))
        args.append(w_bf)
    if bias is not None:
        for s in range(ns):
            in_specs.append(pl.BlockSpec((1, tn), functools.partial(lambda s, b, i, j: (0, s * nj + j), s)))
            args.append(bias)
    out_spec = pl.BlockSpec((tm, tn), lambda b, i, j: (b * nt + i, j))
    return pl.pallas_call(
        functools.partial(_inproj_kernel, ns, bias is not None, epi),
        grid=(nb, nt, nj),
        in_specs=in_specs,
        out_specs=[out_spec] * n_out,
        out_shape=[jax.ShapeDtypeStruct((rows, D_INNER), BF16)] * n_out,
        scratch_shapes=[pltpu.VMEM((tm, D_MODEL), BF16)],
        name="inproj%d" % ns,
        compiler_params=_cparams(("parallel", "parallel", "arbitrary")),
    )(*args)


def _conv_epi(ys):
    u, ug, gt = ys
    return [u * _sigmoid(ug), _silu(gt)]


def _na_epi(ys):
    q, k, v, g = ys
    return [q * (HEAD_DIM ** -0.5), k, v, _silu(g)]


def _outproj_kernel(pro, has_bias, nin, *refs):
    in_refs = refs[:nin]
    x_ref, gate_ref, gpost_ref, w_ref = refs[nin:nin + 4]
    b_ref = refs[nin + 4] if has_bias else None
    o_ref = refs[-1]
    a = pro(*in_refs)
    ox = _dot(a, w_ref[...])
    if has_bias:
        ox = ox + b_ref[...]
    o_ref[...] = _post(x_ref[...], ox, gate_ref[...], gpost_ref[...])


def _outproj(ins, in_specs_extra, pro, x2, nb, gate, g_post, w_bf, bias, tm):
    rows = x2.shape[0]
    nt = rows // nb // tm
    row_map = lambda b, i: (b * nt + i, 0)
    in_specs = [pl.BlockSpec((tm, D_INNER), row_map) for _ in ins]
    args = list(ins)
    for arr, spec in in_specs_extra:
        in_specs.append(spec)
        args.append(arr)
    nin = len(args)
    in_specs += [pl.BlockSpec((tm, D_MODEL), row_map),
                 pl.BlockSpec((None, 1, D_MODEL), lambda b, i: (b, 0, 0)),
                 pl.BlockSpec((1, D_MODEL), lambda b, i: (0, 0)),
                 pl.BlockSpec((D_INNER, D_MODEL), lambda b, i: (0, 0))]
    args += [x2, gate, g_post, w_bf]
    if bias is not None:
        in_specs.append(pl.BlockSpec((1, D_MODEL), lambda b, i: (0, 0)))
        args.append(bias)
    return pl.pallas_call(
        functools.partial(_outproj_kernel, pro, bias is not None, nin),
        grid=(nb, nt),
        in_specs=in_specs,
        out_specs=pl.BlockSpec((tm, D_MODEL), row_map),
        out_shape=jax.ShapeDtypeStruct((rows, D_MODEL), F32),
        name="outproj_" + pro.__name__.strip("_"),
        compiler_params=_cparams(("parallel", "parallel")),
    )(*args)


def _na_pro(o_ref, sg_ref):
    return o_ref[...] * sg_ref[...]


def _conv_pro(u_ref, sg_ref, lnw_ref, lnb_ref):
    u = u_ref[...].astype(F32)
    mu = jnp.mean(u, axis=-1, keepdims=True)
    d = u - mu
    var = jnp.mean(d * d, axis=-1, keepdims=True)
    y = d * lax.rsqrt(var + LN_EPS) * lnw_ref[...] + lnb_ref[...]
    return (_silu(y) * sg_ref[...].astype(F32)).astype(BF16)


def _dwconv_kernel(tm, rb, nt, u_ref, up_ref, un_ref, dw_ref, db_ref, o_ref, buf_ref):
    i = pl.program_id(1)
    keep_prev = (i > 0).astype(F32)
    keep_next = (i < nt - 1).astype(F32)
    buf_ref[0:CONV_HALO, :] = up_ref[...].astype(F32) * keep_prev
    buf_ref[CONV_HALO:CONV_HALO + tm, :] = u_ref[...].astype(F32)
    buf_ref[CONV_HALO + tm:2 * CONV_HALO + tm, :] = un_ref[...].astype(F32) * keep_next
    dw = dw_ref[...]
    off = CONV_HALO - CONV_WIDTH // 2
    for r0 in range(0, tm, rb):
        acc = jnp.zeros((rb, u_ref.shape[1]), F32) + db_ref[...]
        for k in range(CONV_WIDTH):
            acc = acc + buf_ref[r0 + off + k:r0 + off + k + rb, :] * dw[k:k + 1, :]
        o_ref[r0:r0 + rb, :] = acc.astype(o_ref.dtype)


def _dwconv(u2, nb, dw, dw_b, tm, tc=512, rb=32):
    rows = u2.shape[0]
    t_len = rows // nb
    nt = t_len // tm
    nj = D_INNER // tc
    hb = tm // CONV_HALO
    nhb = rows // CONV_HALO
    prev_map = lambda b, i, j: (jnp.maximum((b * nt + i) * hb - 1, 0), j)
    next_map = lambda b, i, j: (jnp.minimum((b * nt + i + 1) * hb, nhb - 1), j)
    return pl.pallas_call(
        functools.partial(_dwconv_kernel, tm, rb, nt),
        grid=(nb, nt, nj),
        in_specs=[pl.BlockSpec((tm, tc), lambda b, i, j: (b * nt + i, j)),
                  pl.BlockSpec((CONV_HALO, tc), prev_map),
                  pl.BlockSpec((CONV_HALO, tc), next_map),
                  pl.BlockSpec((CONV_WIDTH, tc), lambda b, i, j: (0, j)),
                  pl.BlockSpec((1, tc), lambda b, i, j: (0, j))],
        out_specs=pl.BlockSpec((tm, tc), lambda b, i, j: (b * nt + i, j)),
        out_shape=jax.ShapeDtypeStruct((rows, D_INNER), BF16),
        scratch_shapes=[pltpu.VMEM((tm + 2 * CONV_HALO, tc), F32)],
        name="dwconv",
        compiler_params=_cparams(("parallel", "parallel", "parallel")),
    )(u2, u2, u2, dw, dw_b)


def _stack_heads(q, lane_lo):
    zero = jnp.zeros_like(q)
    return jnp.concatenate([jnp.where(lane_lo, q, zero), jnp.where(lane_lo, zero, q)], axis=0)


def _na_kernel(n_rows, q_ref, k_ref, v_ref, kc_ref, vc_ref, bias_ref, o_ref):
    kr = min(NA_ROWS, n_rows)
    nloc = kr * GRID_W
    lane_lo = lax.broadcasted_iota(jnp.int32, (GRID_W, LANES), 1) < HEAD_DIM
    kc = kc_ref[...]
    vc = vc_ref[...]

    def row(r, carry):
        start = jnp.clip(r - kr // 2, 0, n_rows - kr)
        delta = r - start
        q0 = pl.multiple_of(r * GRID_W, GRID_W)
        k0 = pl.multiple_of(start * GRID_W, GRID_W)
        qm = _stack_heads(q_ref[pl.ds(q0, GRID_W), :], lane_lo)
        kw = k_ref[pl.ds(k0, nloc), :]
        vw = v_ref[pl.ds(k0, nloc), :]
        s_loc = _dot_t(qm, kw) + bias_ref[delta, 0]
        s_ctx = _dot_t(qm, kc)
        m = jnp.maximum(jnp.max(s_loc, axis=-1, keepdims=True), jnp.max(s_ctx, axis=-1, keepdims=True))
        p_loc = jnp.exp(s_loc - m)
        p_ctx = jnp.exp(s_ctx - m)
        den = jnp.sum(p_loc, axis=-1, keepdims=True) + jnp.sum(p_ctx, axis=-1, keepdims=True)
        o2 = (_dot(p_loc.astype(BF16), vw) + _dot(p_ctx.astype(BF16), vc)) / den
        o_ref[pl.ds(q0, GRID_W), :] = jnp.where(lane_lo, o2[:GRID_W], o2[GRID_W:]).astype(o_ref.dtype)
        return carry

    lax.fori_loop(0, n_rows, row, 0)


def _na_attention(q, k, v, kc, vc, bias8, nb):
    rows = q.shape[0]
    t_len = rows // nb
    n_rows = t_len // GRID_W
    ctx_len = kc.shape[0] // nb
    nhp = D_INNER // LANES
    kr = min(NA_ROWS, n_rows)
    big = pl.BlockSpec((t_len, LANES), lambda b, h: (b, h))
    cspec = pl.BlockSpec((ctx_len, LANES), lambda b, h: (b, h))
    return pl.pallas_call(
        functools.partial(_na_kernel, n_rows),
        grid=(nb, nhp),
        in_specs=[big, big, big, cspec, cspec,
                  pl.BlockSpec((NA_ROWS, 1, LANES, kr * GRID_W), lambda b, h: (0, h, 0, 0))],
        out_specs=big,
        out_shape=jax.ShapeDtypeStruct((rows, D_INNER), BF16),
        name="na_attn",
        compiler_params=_cparams(("parallel", "parallel")),
    )(q, k, v, kc, vc, bias8)


def _ctx_attn_kernel(q_ref, k_ref, v_ref, o_ref):
    n = q_ref.shape[0]
    lane_lo = lax.broadcasted_iota(jnp.int32, (n, LANES), 1) < HEAD_DIM
    qm = _stack_heads(q_ref[...], lane_lo)
    s = _dot_t(qm, k_ref[...])
    m = jnp.max(s, axis=-1, keepdims=True)
    p = jnp.exp(s - m)
    den = jnp.sum(p, axis=-1, keepdims=True)
    o2 = _dot(p.astype(BF16), v_ref[...]) / den
    o_ref[...] = jnp.where(lane_lo, o2[:n], o2[n:]).astype(o_ref.dtype)


def _ctx_attention(qc, kc, vc, nb):
    ctx_len = qc.shape[0] // nb
    spec = pl.BlockSpec((ctx_len, LANES), lambda b, h: (b, h))
    return pl.pallas_call(
        _ctx_attn_kernel,
        grid=(nb, D_INNER // LANES),
        in_specs=[spec, spec, spec],
        out_specs=spec,
        out_shape=jax.ShapeDtypeStruct(qc.shape, BF16),
        name="ctx_attn",
        compiler_params=_cparams(("parallel", "parallel")),
    )(qc, kc, vc)


def _na_bias_table(rpb, n_rows):
    kr = min(NA_ROWS, n_rows)
    cols = np.arange(GRID_W)
    cstart = np.clip(cols - NA_COLS // 2, 0, GRID_W - NA_COLS)
    jc = np.arange(GRID_W)
    inwin = (jc[None, :] >= cstart[:, None]) & (jc[None, :] < cstart[:, None] + NA_COLS)
    relc = np.clip(jc[None, :] - cols[:, None] + NA_COLS - 1, 0, 2 * NA_COLS - 2)
    delta = np.arange(NA_ROWS)
    relr = np.clip(np.arange(kr)[None, :] - delta[:, None] + NA_ROWS - 1, 0, 2 * NA_ROWS - 2)
    tab = rpb[:, relr[:, :, None, None], relc[None, None, :, :]]
    tab = jnp.where(inwin[None, None, None], tab, NEG_BIG)
    tab = tab.transpose(1, 0, 3, 2, 4)
    return tab.reshape(NA_ROWS, N_HEADS // 2, 2 * GRID_W, kr * GRID_W).astype(F32)


def _rwkv_prep_kernel(tm, nt, x_ref, xp_ref, xn_ref, g_ref, s_ref, sh_ref, mu_ref,
                      wr_ref, wk_ref, wv_ref, wg_ref, w1_ref, w2_ref, w0_ref,
                      a1_ref, a2_ref, a0_ref, kk_ref, ka_ref, rk_ref, bd_ref,
                      r_o, v_o, kkn_o, bon_o, sg_o, lw_o, kd_o, bb_o,
                      xs_ref, hw_ref, ha_ref):
    i = pl.program_id(1)

    @pl.when(pl.program_id(2) == 0)
    def _():
        g, s1p, sh = g_ref[...], s_ref[...], sh_ref[...]
        h = _modnorm(x_ref[...], g, s1p, sh)
        hp = _modnorm(xp_ref[...], g, s1p, sh)[7:8] * (i > 0).astype(F32)
        hn = _modnorm(xn_ref[...], g, s1p, sh)[0:1] * (i < nt - 1).astype(F32)
        rows = lax.broadcasted_iota(jnp.int32, (tm, 1), 0)
        prev = jnp.where(rows == 0, hp, pltpu.roll(h, 1, axis=0))
        nxt = jnp.where(rows == tm - 1, hn, pltpu.roll(h, tm - 1, axis=0))
        ts = 0.5 * (prev + nxt) - h
        for s in range(6):
            xs_ref[s] = (h + ts * mu_ref[s:s + 1, :]).astype(BF16)
        for z in range(2):
            hw_ref[z] = jnp.tanh(_dot(xs_ref[4], w1_ref[z])).astype(BF16)
            ha_ref[z] = _dot(xs_ref[5], a1_ref[z]).astype(BF16)

    bd = bd_ref[...]
    r = _dot(xs_ref[0], wr_ref[...])
    k = _dot(xs_ref[1], wk_ref[...])
    v = _dot(xs_ref[2], wv_ref[...])
    g = _dot(xs_ref[3], wg_ref[...])
    kkh = k * kk_ref[...]
    kk = kkh * lax.rsqrt(_head_sum(kkh * kkh, bd) + 1e-12)
    ksum = jnp.zeros_like(k)
    for z in range(2):
        w_raw = w0_ref[z] + _dot(hw_ref[z], w2_ref[z])
        lw_o[z] = -math.exp(-0.5) * _sigmoid(w_raw)
        a = _sigmoid(a0_ref[z] + _dot(ha_ref[z], a2_ref[z]))
        kd = k * (1.0 + (a - 1.0) * ka_ref[...])
        kd_o[z] = kd.astype(BF16)
        bb_o[z] = (kk * a).astype(BF16)
        ksum = ksum + kd
    r_o[...] = r.astype(BF16)
    v_o[...] = v.astype(BF16)
    kkn_o[...] = kk.astype(BF16)
    bon_o[...] = (_head_sum(r * ksum * rk_ref[...], bd) * v).astype(BF16)
    sg_o[...] = _silu(g).astype(BF16)


def _rwkv_prep(x2, nb, g_pre, s1p, sh, p, tm):
    rows = x2.shape[0]
    nt = rows // nb // tm
    tn = GROUP_W
    hb = tm // 8
    nhb = rows // 8
    c3 = lambda b, i, j: (0, 0)
    vec = lambda b, i, j: (b, 0, 0)
    colv = pl.BlockSpec((1, tn), lambda b, i, j: (0, j))
    in_specs = [
        pl.BlockSpec((tm, D_MODEL), lambda b, i, j: (b * nt + i, 0)),
        pl.BlockSpec((8, D_MODEL), lambda b, i, j: (jnp.maximum((b * nt + i) * hb - 1, 0), 0)),
        pl.BlockSpec((8, D_MODEL), lambda b, i, j: (jnp.minimum((b * nt + i + 1) * hb, nhb - 1), 0)),
        pl.BlockSpec((1, D_MODEL), c3),
        pl.BlockSpec((None, 1, D_MODEL), vec),
        pl.BlockSpec((None, 1, D_MODEL), vec),
        pl.BlockSpec((6, D_MODEL), c3),
    ]
    args = [x2, x2, x2, g_pre, s1p, sh, p['mu']]
    for s in range(4):
        in_specs.append(pl.BlockSpec((None, D_MODEL, tn), functools.partial(lambda s, b, i, j: (s, 0, j), s)))
        args.append(p['w_in'])
    lora_in = pl.BlockSpec((2, D_MODEL, LORA), lambda b, i, j: (0, 0, 0))
    lora_out = pl.BlockSpec((2, LORA, tn), lambda b, i, j: (0, 0, j))
    lora_b = pl.BlockSpec((2, 1, tn), lambda b, i, j: (0, 0, j))
    in_specs += [lora_in, lora_out, lora_b, lora_in, lora_out, lora_b, colv, colv, colv,
                 pl.BlockSpec((LANES, LANES), c3)]
    args += [p['w1'], p['w2'], p['w0'], p['a1'], p['a2'], p['a0'], p['k_k'], p['k_a'], p['r_k'], p['bd']]
    gm = pl.BlockSpec((None, tm, tn), lambda b, i, j: (j, b * nt + i, 0))
    gm2 = pl.BlockSpec((2, None, tm, tn), lambda b, i, j: (0, j, b * nt + i, 0))
    sh1 = jax.ShapeDtypeStruct((N_GROUPS, rows, tn), BF16)
    sh2 = jax.ShapeDtypeStruct((2, N_GROUPS, rows, tn), BF16)
    return pl.pallas_call(
        functools.partial(_rwkv_prep_kernel, tm, nt),
        grid=(nb, nt, N_GROUPS),
        in_specs=in_specs,
        out_specs=[gm, gm, gm, gm, gm, gm2, gm2, gm2],
        out_shape=[sh1, sh1, sh1, sh1, sh1, jax.ShapeDtypeStruct((2, N_GROUPS, rows, tn), F32), sh2, sh2],
        scratch_shapes=[pltpu.VMEM((6, tm, D_MODEL), BF16),
                        pltpu.VMEM((2, tm, LORA), BF16),
                        pltpu.VMEM((2, tm, LORA), BF16)],
        name="rwkv_prep",
        compiler_params=_cparams(("parallel", "parallel", "arbitrary")),
    )(*args)


def _block_diag(x_bf, bdm):
    return jnp.concatenate([x_bf] * HEADS_PER_GROUP, axis=0) * bdm


def _scan_kernel(r_ref, v_ref, kk_ref, lw_ref, kd_ref, bb_ref, sin_ref,
                 tri_ref, mk_ref, eye_ref, bdm_ref, y_ref, s_ref):
    @pl.when(pl.program_id(2) == 0)
    def _():
        s_ref[...] = sin_ref[...]

    tri = tri_ref[...]
    bdm = bdm_ref[...]
    bdm_f = bdm.astype(F32)

    def group(g, carry):
        lw = lw_ref[g]
        r = r_ref[g].astype(F32)
        v_bf = v_ref[g]
        kk = kk_ref[g].astype(F32)
        kd = kd_ref[g].astype(F32)
        bb = bb_ref[g].astype(F32)
        st = s_ref[g]
        ms = mk_ref[0]
        mi = mk_ref[1]

        lw_hi, lw_lo = _split_bf16(lw)
        cs = _dot(tri, lw_hi) + _dot(tri, lw_lo)
        tot = jnp.sum(lw, axis=0, keepdims=True)
        e_neg = jnp.exp(-cs)
        e_rem = jnp.exp(tot - cs)
        at = (-kk * jnp.exp(cs - lw)).astype(BF16)
        rt = (r * jnp.exp(cs)).astype(BF16)
        lhs = jnp.concatenate([at, rt], axis=0)
        a1 = _dot_t(lhs, _block_diag((bb * e_neg).astype(BF16), bdm))
        a2 = _dot_t(lhs, _block_diag((kd * e_neg).astype(BF16), bdm))
        n = a1[:CHUNK] * ms
        a_rb = (a1[CHUNK:] * mi).astype(BF16)
        a_ak = (a2[:CHUNK] * ms).astype(BF16)
        a_rk = (a2[CHUNK:] * mi).astype(BF16)

        nd = n * mk_ref[2]
        t_inv = eye_ref[...] + nd
        pw = nd
        for _ in range(2):
            pw_bf = pw.astype(BF16)
            pw = _dot(pw_bf, _block_diag(pw_bf, bdm))
            t_inv = t_inv + _dot(t_inv.astype(BF16), _block_diag(pw.astype(BF16), bdm))
        for lvl in range(3):
            t_bf = t_inv.astype(BF16)
            xm = _dot((n * mk_ref[3 + lvl]).astype(BF16), _block_diag(t_bf, bdm))
            t_inv = t_inv + _dot(t_bf, _block_diag(xm.astype(BF16), bdm))

        st_bd = _block_diag(st.astype(BF16), bdm)
        v_bd = _block_diag(v_bf, bdm)
        rhs_u = _dot_t(at, st_bd) + _dot(a_ak, v_bd)
        u = _dot(t_inv.astype(BF16), _block_diag(rhs_u.astype(BF16), bdm))
        u_bf = u.astype(BF16)
        y = _dot_t(rt, st_bd) + _dot(a_rb, _block_diag(u_bf, bdm)) + _dot(a_rk, v_bd)
        y_ref[g] = y.astype(y_ref.dtype)

        uv_t = jnp.concatenate([u, v_bf.astype(F32)], axis=0).T.astype(BF16)
        bk = jnp.concatenate([bb * e_rem, kd * e_rem], axis=0).astype(BF16)
        full = _dot(uv_t, bk) * bdm_f
        upd = full[0:CHUNK]
        for hh in range(1, HEADS_PER_GROUP):
            upd = upd + full[hh * CHUNK:(hh + 1) * CHUNK]
        s_ref[g] = st * jnp.exp(tot) + upd
        return carry

    lax.fori_loop(0, N_GROUPS, group, 0)


def _scan_consts():
    t = np.arange(CHUNK)
    tt, jj = t[:, None], t[None, :]
    lower = jj <= tt
    strict = jj < tt
    base = strict & (tt // 8 == jj // 8)
    pairs = [(tt // s == jj // s + 1) & (tt // (2 * s) == jj // (2 * s)) for s in (8, 16, 32)]
    fwd = np.stack([strict, lower, base] + pairs)
    both = np.stack([fwd, fwd.transpose(0, 2, 1)]).astype(np.float32)
    masks = np.tile(both, (1, 1, 1, HEADS_PER_GROUP))
    tri = np.stack([lower, lower.T]).astype(np.float32)
    eye = np.tile(np.eye(CHUNK, dtype=np.float32), (1, HEADS_PER_GROUP))
    blk = np.arange(GROUP_W) // HEAD_DIM
    bdm = (blk[:, None] == blk[None, :]).astype(np.float32)
    return jnp.asarray(tri, BF16), jnp.asarray(masks, F32), jnp.asarray(eye, F32), jnp.asarray(bdm, BF16)


def _rwkv_scan(r, v, kk, lw, kd, bb, s_in, nb):
    rows = r.shape[1]
    nc = rows // nb // CHUNK
    tri, masks, eye, bdm = _scan_consts()
    blk = lambda d, b, c: b * nc + c + d * (nc - 1 - 2 * c)
    shared = pl.BlockSpec((N_GROUPS, CHUNK, GROUP_W), lambda d, b, c: (0, blk(d, b, c), 0))
    perdir = pl.BlockSpec((None, N_GROUPS, CHUNK, GROUP_W), lambda d, b, c: (d, 0, blk(d, b, c), 0))
    sspec = pl.BlockSpec((None, None, N_GROUPS, HEAD_DIM, GROUP_W), lambda d, b, c: (d, b, 0, 0, 0))
    dsel = lambda shape: pl.BlockSpec((None,) + shape, lambda d, b, c: (d, 0, 0))
    return pl.pallas_call(
        _scan_kernel,
        grid=(2, nb, nc),
        in_specs=[shared, shared, shared, perdir, perdir, perdir, sspec,
                  dsel((CHUNK, CHUNK)),
                  pl.BlockSpec((None, 6, CHUNK, GROUP_W), lambda d, b, c: (d, 0, 0, 0)),
                  pl.BlockSpec((CHUNK, GROUP_W), lambda d, b, c: (0, 0)),
                  pl.BlockSpec((GROUP_W, GROUP_W), lambda d, b, c: (0, 0))],
        out_specs=[perdir, sspec],
        out_shape=[jax.ShapeDtypeStruct((2, N_GROUPS, rows, GROUP_W), BF16),
                   jax.ShapeDtypeStruct(s_in.shape, F32)],
        name="rwkv_scan",
        compiler_params=_cparams(("parallel", "parallel", "arbitrary")),
    )(r, v, kk, lw, kd, bb, s_in, tri, masks, eye, bdm)


def _rwkv_out_kernel(y_ref, bon_ref, sg_ref, lnw_ref, lnb_ref, bd_ref,
                     x_ref, gate_ref, gpost_ref, w_ref, o_ref):
    bd = bd_ref[...]
    acc = jnp.zeros(o_ref.shape, F32)
    for g in range(N_GROUPS):
        yy = y_ref[0, g].astype(F32) + y_ref[1, g].astype(F32)
        mu = _head_sum(yy, bd) * (1.0 / HEAD_DIM)
        d = yy - mu
        var = _head_sum(d * d, bd) * (1.0 / HEAD_DIM)
        yn = d * lax.rsqrt(var + GN_EPS) * lnw_ref[g] + lnb_ref[g]
        o = (yn + bon_ref[g].astype(F32)) * sg_ref[g].astype(F32)
        acc = acc + _dot(o.astype(BF16), w_ref[g])
    o_ref[...] = _post(x_ref[...], acc, gate_ref[...], gpost_ref[...])


def _rwkv_out(y, bon, sg, p, x2, nb, gate, g_post, tm):
    rows = x2.shape[0]
    nt = rows // nb // tm
    row_map = lambda b, i: (b * nt + i, 0)
    gm = pl.BlockSpec((N_GROUPS, tm, GROUP_W), lambda b, i: (0, b * nt + i, 0))
    pv = pl.BlockSpec((N_GROUPS, 1, GROUP_W), lambda b, i: (0, 0, 0))
    return pl.pallas_call(
        _rwkv_out_kernel,
        grid=(nb, nt),
        in_specs=[pl.BlockSpec((2, N_GROUPS, tm, GROUP_W), lambda b, i: (0, 0, b * nt + i, 0)),
                  gm, gm, pv, pv,
                  pl.BlockSpec((LANES, LANES), lambda b, i: (0, 0)),
                  pl.BlockSpec((tm, D_MODEL), row_map),
                  pl.BlockSpec((None, 1, D_MODEL), lambda b, i: (b, 0, 0)),
                  pl.BlockSpec((1, D_MODEL), lambda b, i: (0, 0)),
                  pl.BlockSpec((N_GROUPS, GROUP_W, D_MODEL), lambda b, i: (0, 0, 0))],
        out_specs=pl.BlockSpec((tm, D_MODEL), row_map),
        out_shape=jax.ShapeDtypeStruct((rows, D_MODEL), F32),
        name="rwkv_out",
        compiler_params=_cparams(("parallel", "parallel")),
    )(y, bon, sg, p['ln_w'], p['ln_b'], p['bd'], x2, gate, g_post, p['w_out'])


def _tile(t_len, pref):
    return pref if t_len % pref == 0 else t_len


def _rwkv_layer(xs, cs, nb, mods, g_pre, g_post, p, need_ctx):
    t_x, t_c = xs.shape[0] // nb, cs.shape[0] // nb
    (s1p_x, sh_x, gate_x), (s1p_c, sh_c, gate_c) = mods
    zeros = jnp.zeros((2, nb, N_GROUPS, HEAD_DIM, GROUP_W), F32)
    rc, vc, kkc, bonc, sgc, lwc, kdc, bbc = _rwkv_prep(cs, nb, g_pre, s1p_c, sh_c, p, _tile(t_c, 256))
    yc, state_c = _rwkv_scan(rc, vc, kkc, lwc, kdc, bbc, zeros, nb)
    rx, vx, kkx, bonx, sgx, lwx, kdx, bbx = _rwkv_prep(xs, nb, g_pre, s1p_x, sh_x, p, _tile(t_x, 512))
    yx, _ = _rwkv_scan(rx, vx, kkx, lwx, kdx, bbx, state_c, nb)
    xs_new = _rwkv_out(yx, bonx, sgx, p, xs, nb, gate_x, g_post, _tile(t_x, 512))
    cs_new = _rwkv_out(yc, bonc, sgc, p, cs, nb, gate_c, g_post, _tile(t_c, 256)) if need_ctx else cs
    return xs_new, cs_new


def _conv_layer(xs, cs, nb, mods, g_pre, g_post, p, need_ctx):
    def run(x2, mod, t_len):
        s1p, sh, gate = mod
        u, sgate = _inproj(x2, nb, g_pre, s1p, sh, p['w_in'], p['b_in'], 3, 2, _conv_epi, _tile(t_len, 1024))
        uc = _dwconv(u, nb, p['dw'], p['dw_b'], _tile(t_len, 128))
        vspec = pl.BlockSpec((1, D_INNER), lambda b, i: (0, 0))
        return _outproj([uc, sgate], [(p['ln_w'], vspec), (p['ln_b'], vspec)], _conv_pro,
                        x2, nb, gate, g_post, p['w_out'], p['b_out'], _tile(t_len, 512))
    xs_new = run(xs, mods[0], xs.shape[0] // nb)
    cs_new = run(cs, mods[1], cs.shape[0] // nb) if need_ctx else cs
    return xs_new, cs_new


def _na_layer(xs, cs, nb, mods, g_pre, g_post, p, need_ctx):
    t_x, t_c = xs.shape[0] // nb, cs.shape[0] // nb
    (s1p_x, sh_x, gate_x), (s1p_c, sh_c, gate_c) = mods
    q, k, v, sg = _inproj(xs, nb, g_pre, s1p_x, sh_x, p['w_in'], None, 4, 4, _na_epi, _tile(t_x, 1024))
    qc, kc, vc, sgc = _inproj(cs, nb, g_pre, s1p_c, sh_c, p['w_in'], None, 4, 4, _na_epi, _tile(t_c, 256))
    o = _na_attention(q, k, v, kc, vc, _na_bias_table(p['rpb'], t_x // GRID_W), nb)
    xs_new = _outproj([o, sg], [], _na_pro, xs, nb, gate_x, g_post, p['w_out'], None, _tile(t_x, 512))
    cs_new = cs
    if need_ctx:
        oc = _ctx_attention(qc, kc, vc, nb)
        cs_new = _outproj([oc, sgc], [], _na_pro, cs, nb, gate_c, g_post, p['w_out'], None, _tile(t_c, 256))
    return xs_new, cs_new


def kernel(x, c, ctx, c_ctx, ada_w, ada_b, norm_pre, norm_post, rw_mu, rw_w_in, rw_w0, rw_w1, rw_w2, rw_a0, rw_a1, rw_a2, rw_k_k, rw_k_a, rw_r_k, rw_ln_w, rw_ln_b, rw_w_out, cf_w_in, cf_b_in, cf_dw, cf_dw_b, cf_ln_w, cf_ln_b, cf_w_out, cf_b_out, na_w_in, na_rpb, na_w_out):
    nb, t_len, _ = x.shape
    ctx_len = ctx.shape[1]
    depth = ada_w.shape[0]
    xs = x.reshape(nb * t_len, D_MODEL)
    cs = ctx.reshape(nb * ctx_len, D_MODEL)

    cc = jnp.zeros((8, D_MODEL), F32).at[:nb].set(c).at[nb].set(c_ctx)
    mod = _modulation(cc, ada_w, ada_b)
    bd = jnp.asarray(np.kron(np.eye(LANES // HEAD_DIM), np.ones((HEAD_DIM, HEAD_DIM))), BF16)

    for i in range(depth):
        kind, j = i % 3, i // 3
        need_ctx = i < depth - 1
        shift, scale, gate = (mod[i, :, s * D_MODEL:(s + 1) * D_MODEL] for s in range(3))
        mods = []
        for sel in (slice(0, nb), slice(nb, nb + 1)):
            mods.append(tuple(jnp.broadcast_to(m[sel], (nb, D_MODEL)).reshape(nb, 1, D_MODEL)
                              for m in (1.0 + scale, shift, gate)))
        g_pre = norm_pre[i].reshape(1, D_MODEL)
        g_post = norm_post[i].reshape(1, D_MODEL)
        if kind == 0:
            p = dict(mu=rw_mu[j], w_in=rw_w_in[j].astype(BF16),
                     w0=rw_w0[j].reshape(2, 1, D_INNER), w1=rw_w1[j].astype(BF16), w2=rw_w2[j].astype(BF16),
                     a0=rw_a0[j].reshape(2, 1, D_INNER), a1=rw_a1[j].astype(BF16), a2=rw_a2[j].astype(BF16),
                     k_k=rw_k_k[j].reshape(1, D_INNER), k_a=rw_k_a[j].reshape(1, D_INNER),
                     r_k=rw_r_k[j].reshape(1, D_INNER),
                     ln_w=rw_ln_w[j].reshape(N_GROUPS, 1, GROUP_W), ln_b=rw_ln_b[j].reshape(N_GROUPS, 1, GROUP_W),
                     w_out=rw_w_out[j].astype(BF16).reshape(N_GROUPS, GROUP_W, D_MODEL), bd=bd)
            xs, cs = _rwkv_layer(xs, cs, nb, mods, g_pre, g_post, p, need_ctx)
        elif kind == 1:
            p = dict(w_in=cf_w_in[j].astype(BF16), b_in=cf_b_in[j].reshape(1, 3 * D_INNER),
                     dw=cf_dw[j], dw_b=cf_dw_b[j].reshape(1, D_INNER),
                     ln_w=cf_ln_w[j].reshape(1, D_INNER), ln_b=cf_ln_b[j].reshape(1, D_INNER),
                     w_out=cf_w_out[j].astype(BF16), b_out=cf_b_out[j].reshape(1, D_MODEL))
            xs, cs = _conv_layer(xs, cs, nb, mods, g_pre, g_post, p, need_ctx)
        else:
            p = dict(w_in=na_w_in[j].astype(BF16), rpb=na_rpb[j], w_out=na_w_out[j].astype(BF16))
            xs, cs = _na_layer(xs, cs, nb, mods, g_pre, g_post, p, need_ctx)
    return xs.reshape(nb, t_len, D_MODEL)
```

```python
import functools
import math

import jax
import jax.numpy as jnp
import numpy as np
from jax import lax
from jax.experimental import pallas as pl
from jax.experimental.pallas import tpu as pltpu

F32 = jnp.float32
BF16 = jnp.bfloat16

D_MODEL = 1024
D_INNER = 2048
N_HEADS = 32
HEAD_DIM = 64
LORA = 64
GRID_W = 64
CONV_WIDTH = 31
NA_ROWS = 8
NA_COLS = 16
RMS_EPS = 1e-6
LN_EPS = 1e-5
GN_EPS = 6.4e-4

LANES = 128
GROUP_W = 256
N_GROUPS = D_INNER // GROUP_W
HEADS_PER_GROUP = GROUP_W // HEAD_DIM
CHUNK = 64
CONV_HALO = 16
NA_UNROLL = 4
NEG_BIG = -1e30
VMEM_LIMIT = 56 * 1024 * 1024


def _cparams(sem):
    return pltpu.CompilerParams(dimension_semantics=sem, vmem_limit_bytes=VMEM_LIMIT)


def _dot(a, b):
    return jnp.dot(a, b, preferred_element_type=F32)


def _dot_t(a, b):
    return lax.dot_general(a, b, (((1,), (1,)), ((), ())), preferred_element_type=F32)


def _split_bf16(x):
    hi = x.astype(BF16)
    lo = (x - hi.astype(F32)).astype(BF16)
    return hi, lo


def _sigmoid(x):
    return 1.0 / (1.0 + jnp.exp(-x))


def _silu(x):
    return x * _sigmoid(x)


def _modnorm(x, g, s1p, sh):
    ms = jnp.mean(x * x, axis=-1, keepdims=True)
    return (x * lax.rsqrt(ms + RMS_EPS) * g) * s1p + sh


def _post(xres, ox, gate, gpost):
    ms = jnp.mean(ox * ox, axis=-1, keepdims=True)
    return xres + gate * (ox * lax.rsqrt(ms + RMS_EPS) * gpost)


def _head_sum(x, bd):
    outs = []
    for s in range(x.shape[1] // LANES):
        hi, lo = _split_bf16(x[:, s * LANES:(s + 1) * LANES])
        outs.append(_dot(hi, bd) + _dot(lo, bd))
    return jnp.concatenate(outs, axis=1) if len(outs) > 1 else outs[0]


def _mod_kernel(c_ref, w_ref, b_ref, o_ref):
    a_hi, a_lo = _split_bf16(_silu(c_ref[...]))
    w_hi, w_lo = _split_bf16(w_ref[...])
    o_ref[...] = _dot(a_hi, w_hi) + _dot(a_hi, w_lo) + _dot(a_lo, w_hi) + b_ref[...]


def _modulation(cc, ada_w, ada_b):
    depth = ada_w.shape[0]
    tn = 1024
    return pl.pallas_call(
        _mod_kernel,
        grid=(depth, 3 * D_MODEL // tn),
        in_specs=[pl.BlockSpec((8, D_MODEL), lambda l, j: (0, 0)),
                  pl.BlockSpec((None, D_MODEL, tn), lambda l, j: (l, 0, j)),
                  pl.BlockSpec((None, 1, tn), lambda l, j: (l, 0, j))],
        out_specs=pl.BlockSpec((None, 8, tn), lambda l, j: (l, 0, j)),
        out_shape=jax.ShapeDtypeStruct((depth, 8, 3 * D_MODEL), F32),
        name="adaln_mod",
        compiler_params=_cparams(("parallel", "parallel")),
    )(cc, ada_w, ada_b.reshape(depth, 1, 3 * D_MODEL))


def _inproj_kernel(ns, has_bias, epi, x_ref, g_ref, s_ref, sh_ref, *rest):
    w_refs = rest[:ns]
    rest = rest[ns:]
    b_refs = rest[:ns] if has_bias else None
    rest = rest[ns:] if has_bias else rest
    out_refs, hn_ref = rest[:-1], rest[-1]

    @pl.when(pl.program_id(2) == 0)
    def _():
        hn_ref[...] = _modnorm(x_ref[...], g_ref[...], s_ref[...], sh_ref[...]).astype(BF16)

    hn = hn_ref[...]
    ys = []
    for s in range(ns):
        y = _dot(hn, w_refs[s][...])
        if has_bias:
            y = y + b_refs[s][...]
        ys.append(y)
    for o_ref, o in zip(out_refs, epi(ys)):
        o_ref[...] = o.astype(o_ref.dtype)


def _inproj(x2, nb, g_pre, s1p, sh, w_bf, bias, ns, n_out, epi, tm, tn=512):
    rows = x2.shape[0]
    nt = rows // nb // tm
    nj = D_INNER // tn
    row_map = lambda b, i, j: (b * nt + i, 0)
    vec_map = lambda b, i, j: (b, 0, 0)
    in_specs = [pl.BlockSpec((tm, D_MODEL), row_map),
                pl.BlockSpec((1, D_MODEL), lambda b, i, j: (0, 0)),
                pl.BlockSpec((None, 1, D_MODEL), vec_map),
                pl.BlockSpec((None, 1, D_MODEL), vec_map)]
    args = [x2, g_pre, s1p, sh]
    for s in range(ns):
        in_specs.append(pl.BlockSpec((D_MODEL, tn), functools.partial(lambda s, b, i, j: (0, s * nj + j), s)))
        args.append(w_bf)
    if bias is not None:
        for s in range(ns):
            in_specs.append(pl.BlockSpec((1, tn), functools.partial(lambda s, b, i, j: (0, s * nj + j), s)))
            args.append(bias)
    out_spec = pl.BlockSpec((tm, tn), lambda b, i, j: (b * nt + i, j))
    return pl.pallas_call(
        functools.partial(_inproj_kernel, ns, bias is not None, epi),
        grid=(nb, nt, nj),
        in_specs=in_specs,
        out_specs=[out_spec] * n_out,
        out_shape=[jax.ShapeDtypeStruct((rows, D_INNER), BF16)] * n_out,
        scratch_shapes=[pltpu.VMEM((tm, D_MODEL), BF16)],
        name="inproj%d" % ns,
        compiler_params=_cparams(("parallel", "parallel", "arbitrary")),
    )(*args)


def _conv_epi(ys):
    u, ug, gt = ys
    return [u * _sigmoid(ug), _silu(gt)]


def _na_epi(ys):
    q, k, v, g = ys
    return [q * (HEAD_DIM ** -0.5), k, v, _silu(g)]


def _outproj_kernel(pro, has_bias, nin, *refs):
    in_refs = refs[:nin]
    x_ref, gate_ref, gpost_ref, w_ref = refs[nin:nin + 4]
    b_ref = refs[nin + 4] if has_bias else None
    o_ref = refs[-1]
    a = pro(*in_refs)
    ox = _dot(a, w_ref[...])
    if has_bias:
        ox = ox + b_ref[...]
    o_ref[...] = _post(x_ref[...], ox, gate_ref[...], gpost_ref[...])


def _outproj(ins, in_specs_extra, pro, x2, nb, gate, g_post, w_bf, bias, tm):
    rows = x2.shape[0]
    nt = rows // nb // tm
    row_map = lambda b, i: (b * nt + i, 0)
    in_specs = [pl.BlockSpec((tm, D_INNER), row_map) for _ in ins]
    args = list(ins)
    for arr, spec in in_specs_extra:
        in_specs.append(spec)
        args.append(arr)
    nin = len(args)
    in_specs += [pl.BlockSpec((tm, D_MODEL), row_map),
                 pl.BlockSpec((None, 1, D_MODEL), lambda b, i: (b, 0, 0)),
                 pl.BlockSpec((1, D_MODEL), lambda b, i: (0, 0)),
                 pl.BlockSpec((D_INNER, D_MODEL), lambda b, i: (0, 0))]
    args += [x2, gate, g_post, w_bf]
    if bias is not None:
        in_specs.append(pl.BlockSpec((1, D_MODEL), lambda b, i: (0, 0)))
        args.append(bias)
    return pl.pallas_call(
        functools.partial(_outproj_kernel, pro, bias is not None, nin),
        grid=(nb, nt),
        in_specs=in_specs,
        out_specs=pl.BlockSpec((tm, D_MODEL), row_map),
        out_shape=jax.ShapeDtypeStruct((rows, D_MODEL), F32),
        name="outproj_" + pro.__name__.strip("_"),
        compiler_params=_cparams(("parallel", "parallel")),
    )(*args)


def _na_pro(o_ref, sg_ref):
    return o_ref[...] * sg_ref[...]


def _conv_pro(u_ref, sg_ref, lnw_ref, lnb_ref):
    u = u_ref[...].astype(F32)
    mu = jnp.mean(u, axis=-1, keepdims=True)
    d = u - mu
    var = jnp.mean(d * d, axis=-1, keepdims=True)
    y = d * lax.rsqrt(var + LN_EPS) * lnw_ref[...] + lnb_ref[...]
    return (_silu(y) * sg_ref[...].astype(F32)).astype(BF16)


def _dwconv_kernel(tm, rb, nt, u_ref, up_ref, un_ref, dw_ref, db_ref, o_ref, buf_ref):
    i = pl.program_id(1)
    keep_prev = (i > 0).astype(F32)
    keep_next = (i < nt - 1).astype(F32)
    buf_ref[0:CONV_HALO, :] = up_ref[...].astype(F32) * keep_prev
    buf_ref[CONV_HALO:CONV_HALO + tm, :] = u_ref[...].astype(F32)
    buf_ref[CONV_HALO + tm:2 * CONV_HALO + tm, :] = un_ref[...].astype(F32) * keep_next
    dw = dw_ref[...]
    off = CONV_HALO - CONV_WIDTH // 2
    for r0 in range(0, tm, rb):
        acc = jnp.zeros((rb, u_ref.shape[1]), F32) + db_ref[...]
        for k in range(CONV_WIDTH):
            acc = acc + buf_ref[r0 + off + k:r0 + off + k + rb, :] * dw[k:k + 1, :]
        o_ref[r0:r0 + rb, :] = acc.astype(o_ref.dtype)


def _dwconv(u2, nb, dw, dw_b, tm, tc=512, rb=32):
    rows = u2.shape[0]
    t_len = rows // nb
    nt = t_len // tm
    nj = D_INNER // tc
    hb = tm // CONV_HALO
    nhb = rows // CONV_HALO
    prev_map = lambda b, i, j: (jnp.maximum((b * nt + i) * hb - 1, 0), j)
    next_map = lambda b, i, j: (jnp.minimum((b * nt + i + 1) * hb, nhb - 1), j)
    return pl.pallas_call(
        functools.partial(_dwconv_kernel, tm, rb, nt),
        grid=(nb, nt, nj),
        in_specs=[pl.BlockSpec((tm, tc), lambda b, i, j: (b * nt + i, j)),
                  pl.BlockSpec((CONV_HALO, tc), prev_map),
                  pl.BlockSpec((CONV_HALO, tc), next_map),
                  pl.BlockSpec((CONV_WIDTH, tc), lambda b, i, j: (0, j)),
                  pl.BlockSpec((1, tc), lambda b, i, j: (0, j))],
        out_specs=pl.BlockSpec((tm, tc), lambda b, i, j: (b * nt + i, j)),
        out_shape=jax.ShapeDtypeStruct((rows, D_INNER), BF16),
        scratch_shapes=[pltpu.VMEM((tm + 2 * CONV_HALO, tc), F32)],
        name="dwconv",
        compiler_params=_cparams(("parallel", "parallel", "parallel")),
    )(u2, u2, u2, dw, dw_b)


def _stack_heads(q, lane_lo):
    zero = jnp.zeros_like(q)
    return jnp.concatenate([jnp.where(lane_lo, q, zero), jnp.where(lane_lo, zero, q)], axis=0)


def _na_kernel(n_rows, q_ref, k_ref, v_ref, kc_ref, vc_ref, bias_ref, o_ref):
    kr = min(NA_ROWS, n_rows)
    nloc = kr * GRID_W
    lane_lo = lax.broadcasted_iota(jnp.int32, (GRID_W, LANES), 1) < HEAD_DIM
    kc = kc_ref[...]
    vc = vc_ref[...]

    def rows(it, carry):
        rr = [it * NA_UNROLL + u for u in range(NA_UNROLL)]
        start = [jnp.clip(r - kr // 2, 0, n_rows - kr) for r in rr]
        q0 = [pl.multiple_of(r * GRID_W, GRID_W) for r in rr]
        k0 = [pl.multiple_of(s * GRID_W, GRID_W) for s in start]
        qm = [_stack_heads(q_ref[pl.ds(q, GRID_W), :], lane_lo) for q in q0]
        s_loc = [_dot_t(a, k_ref[pl.ds(k, nloc), :]) for a, k in zip(qm, k0)]
        s_ctx = [_dot_t(a, kc) for a in qm]
        s_loc = [s + bias_ref[r - st, 0] for s, r, st in zip(s_loc, rr, start)]
        m = [jnp.maximum(jnp.max(a, axis=-1, keepdims=True), jnp.max(b, axis=-1, keepdims=True))
             for a, b in zip(s_loc, s_ctx)]
        p_loc = [jnp.exp(a - mm) for a, mm in zip(s_loc, m)]
        p_ctx = [jnp.exp(a - mm) for a, mm in zip(s_ctx, m)]
        den = [jnp.sum(a, axis=-1, keepdims=True) + jnp.sum(b, axis=-1, keepdims=True)
               for a, b in zip(p_loc, p_ctx)]
        o2 = [_dot(a.astype(BF16), v_ref[pl.ds(k, nloc), :]) + _dot(b.astype(BF16), vc)
              for a, b, k in zip(p_loc, p_ctx, k0)]
        for o, d, q in zip(o2, den, q0):
            o = o / d
            o_ref[pl.ds(q, GRID_W), :] = jnp.where(lane_lo, o[:GRID_W], o[GRID_W:]).astype(o_ref.dtype)
        return carry

    lax.fori_loop(0, n_rows // NA_UNROLL, rows, 0)


def _na_attention(q, k, v, kc, vc, bias8, nb):
    rows = q.shape[0]
    t_len = rows // nb
    n_rows = t_len // GRID_W
    ctx_len = kc.shape[0] // nb
    nhp = D_INNER // LANES
    kr = min(NA_ROWS, n_rows)
    big = pl.BlockSpec((t_len, LANES), lambda b, h: (b, h))
    cspec = pl.BlockSpec((ctx_len, LANES), lambda b, h: (b, h))
    return pl.pallas_call(
        functools.partial(_na_kernel, n_rows),
        grid=(nb, nhp),
        in_specs=[big, big, big, cspec, cspec,
                  pl.BlockSpec((NA_ROWS, 1, LANES, kr * GRID_W), lambda b, h: (0, h, 0, 0))],
        out_specs=big,
        out_shape=jax.ShapeDtypeStruct((rows, D_INNER), BF16),
        name="na_attn",
        compiler_params=_cparams(("parallel", "parallel")),
    )(q, k, v, kc, vc, bias8)


def _ctx_attn_kernel(q_ref, k_ref, v_ref, o_ref):
    n = q_ref.shape[0]
    lane_lo = lax.broadcasted_iota(jnp.int32, (n, LANES), 1) < HEAD_DIM
    qm = _stack_heads(q_ref[...], lane_lo)
    s = _dot_t(qm, k_ref[...])
    m = jnp.max(s, axis=-1, keepdims=True)
    p = jnp.exp(s - m)
    den = jnp.sum(p, axis=-1, keepdims=True)
    o2 = _dot(p.astype(BF16), v_ref[...]) / den
    o_ref[...] = jnp.where(lane_lo, o2[:n], o2[n:]).astype(o_ref.dtype)


def _ctx_attention(qc, kc, vc, nb):
    ctx_len = qc.shape[0] // nb
    spec = pl.BlockSpec((ctx_len, LANES), lambda b, h: (b, h))
    return pl.pallas_call(
        _ctx_attn_kernel,
        grid=(nb, D_INNER // LANES),
        in_specs=[spec, spec, spec],
        out_specs=spec,
        out_shape=jax.ShapeDtypeStruct(qc.shape, BF16),
        name="ctx_attn",
        compiler_params=_cparams(("parallel", "parallel")),
    )(qc, kc, vc)


def _na_bias_table(rpb, n_rows):
    kr = min(NA_ROWS, n_rows)
    cols = np.arange(GRID_W)
    cstart = np.clip(cols - NA_COLS // 2, 0, GRID_W - NA_COLS)
    jc = np.arange(GRID_W)
    inwin = (jc[None, :] >= cstart[:, None]) & (jc[None, :] < cstart[:, None] + NA_COLS)
    pad = jnp.pad(rpb, ((0, 0), (0, 0), (GRID_W, GRID_W)))
    off = GRID_W + NA_COLS - 1
    toep = jnp.stack([pad[:, :, off - c:off - c + GRID_W] for c in range(GRID_W)], axis=2)
    toep = jnp.where(inwin[None, None], toep, NEG_BIG)
    tab = jnp.stack([toep[:, NA_ROWS - 1 - d:NA_ROWS - 1 - d + kr] for d in range(NA_ROWS)])
    tab = tab.transpose(0, 1, 3, 2, 4)
    return tab.reshape(NA_ROWS, N_HEADS // 2, 2 * GRID_W, kr * GRID_W).astype(F32)


def _rwkv_prep_kernel(tm, nt, x_ref, xp_ref, xn_ref, g_ref, s_ref, sh_ref, mu_ref,
                      wr_ref, wk_ref, wv_ref, wg_ref, w1_ref, w2_ref, w0_ref,
                      a1_ref, a2_ref, a0_ref, kk_ref, ka_ref, rk_ref, bd_ref,
                      r_o, v_o, kkn_o, bon_o, sg_o, lw_o, kd_o, bb_o,
                      xs_ref, hw_ref, ha_ref):
    i = pl.program_id(1)

    @pl.when(pl.program_id(2) == 0)
    def _():
        g, s1p, sh = g_ref[...], s_ref[...], sh_ref[...]
        h = _modnorm(x_ref[...], g, s1p, sh)
        hp = _modnorm(xp_ref[...], g, s1p, sh)[7:8] * (i > 0).astype(F32)
        hn = _modnorm(xn_ref[...], g, s1p, sh)[0:1] * (i < nt - 1).astype(F32)
        rows = lax.broadcasted_iota(jnp.int32, (tm, 1), 0)
        prev = jnp.where(rows == 0, hp, pltpu.roll(h, 1, axis=0))
        nxt = jnp.where(rows == tm - 1, hn, pltpu.roll(h, tm - 1, axis=0))
        ts = 0.5 * (prev + nxt) - h
        for s in range(6):
            xs_ref[s] = (h + ts * mu_ref[s:s + 1, :]).astype(BF16)
        for z in range(2):
            hw_ref[z] = jnp.tanh(_dot(xs_ref[4], w1_ref[z])).astype(BF16)
            ha_ref[z] = _dot(xs_ref[5], a1_ref[z]).astype(BF16)

    bd = bd_ref[...]
    r = _dot(xs_ref[0], wr_ref[...])
    k = _dot(xs_ref[1], wk_ref[...])
    v = _dot(xs_ref[2], wv_ref[...])
    g = _dot(xs_ref[3], wg_ref[...])
    kkh = k * kk_ref[...]
    kk = kkh * lax.rsqrt(_head_sum(kkh * kkh, bd) + 1e-12)
    ksum = jnp.zeros_like(k)
    for z in range(2):
        w_raw = w0_ref[z] + _dot(hw_ref[z], w2_ref[z])
        lw_o[z] = -math.exp(-0.5) * _sigmoid(w_raw)
        a = _sigmoid(a0_ref[z] + _dot(ha_ref[z], a2_ref[z]))
        kd = k * (1.0 + (a - 1.0) * ka_ref[...])
        kd_o[z] = kd.astype(BF16)
        bb_o[z] = (kk * a).astype(BF16)
        ksum = ksum + kd
    r_o[...] = r.astype(BF16)
    v_o[...] = v.astype(BF16)
    kkn_o[...] = kk.astype(BF16)
    bon_o[...] = (_head_sum(r * ksum * rk_ref[...], bd) * v).astype(BF16)
    sg_o[...] = _silu(g).astype(BF16)


def _rwkv_prep(x2, nb, g_pre, s1p, sh, p, tm):
    rows = x2.shape[0]
    nt = rows // nb // tm
    tn = GROUP_W
    hb = tm // 8
    nhb = rows // 8
    c3 = lambda b, i, j: (0, 0)
    vec = lambda b, i, j: (b, 0, 0)
    colv = pl.BlockSpec((1, tn), lambda b, i, j: (0, j))
    in_specs = [
        pl.BlockSpec((tm, D_MODEL), lambda b, i, j: (b * nt + i, 0)),
        pl.BlockSpec((8, D_MODEL), lambda b, i, j: (jnp.maximum((b * nt + i) * hb - 1, 0), 0)),
        pl.BlockSpec((8, D_MODEL), lambda b, i, j: (jnp.minimum((b * nt + i + 1) * hb, nhb - 1), 0)),
        pl.BlockSpec((1, D_MODEL), c3),
        pl.BlockSpec((None, 1, D_MODEL), vec),
        pl.BlockSpec((None, 1, D_MODEL), vec),
        pl.BlockSpec((6, D_MODEL), c3),
    ]
    args = [x2, x2, x2, g_pre, s1p, sh, p['mu']]
    for s in range(4):
        in_specs.append(pl.BlockSpec((None, D_MODEL, tn), functools.partial(lambda s, b, i, j: (s, 0, j), s)))
        args.append(p['w_in'])
    lora_in = pl.BlockSpec((2, D_MODEL, LORA), lambda b, i, j: (0, 0, 0))
    lora_out = pl.BlockSpec((2, LORA, tn), lambda b, i, j: (0, 0, j))
    lora_b = pl.BlockSpec((2, 1, tn), lambda b, i, j: (0, 0, j))
    in_specs += [lora_in, lora_out, lora_b, lora_in, lora_out, lora_b, colv, colv, colv,
                 pl.BlockSpec((LANES, LANES), c3)]
    args += [p['w1'], p['w2'], p['w0'], p['a1'], p['a2'], p['a0'], p['k_k'], p['k_a'], p['r_k'], p['bd']]
    gm = pl.BlockSpec((None, tm, tn), lambda b, i, j: (j, b * nt + i, 0))
    gm2 = pl.BlockSpec((2, None, tm, tn), lambda b, i, j: (0, j, b * nt + i, 0))
    sh1 = jax.ShapeDtypeStruct((N_GROUPS, rows, tn), BF16)
    sh2 = jax.ShapeDtypeStruct((2, N_GROUPS, rows, tn), BF16)
    return pl.pallas_call(
        functools.partial(_rwkv_prep_kernel, tm, nt),
        grid=(nb, nt, N_GROUPS),
        in_specs=in_specs,
        out_specs=[gm, gm, gm, gm, gm, gm2, gm2, gm2],
        out_shape=[sh1, sh1, sh1, sh1, sh1, jax.ShapeDtypeStruct((2, N_GROUPS, rows, tn), F32), sh2, sh2],
        scratch_shapes=[pltpu.VMEM((6, tm, D_MODEL), BF16),
                        pltpu.VMEM((2, tm, LORA), BF16),
                        pltpu.VMEM((2, tm, LORA), BF16)],
        name="rwkv_prep",
        compiler_params=_cparams(("parallel", "parallel", "arbitrary")),
    )(*args)


def _block_diag(x_bf, bdm):
    return jnp.concatenate([x_bf] * HEADS_PER_GROUP, axis=0) * bdm


def _scan_kernel(r_ref, v_ref, kk_ref, lw_ref, kd_ref, bb_ref, sin_ref,
                 tri_ref, mk_ref, eye_ref, bdm_ref, y_ref, s_ref):
    @pl.when(pl.program_id(2) == 0)
    def _():
        s_ref[...] = sin_ref[...]

    tri = tri_ref[...]
    bdm = bdm_ref[...]
    bdm_f = bdm.astype(F32)

    def each(fn, *lists):
        return [fn(*args) for args in zip(*lists)]

    def bdiag(x):
        return _block_diag(x, bdm)

    def mm(a_list, b_list):
        return each(lambda a, b: _dot(a.astype(BF16), bdiag(b.astype(BF16))), a_list, b_list)

    grp = range(N_GROUPS)
    ms, mi = mk_ref[0], mk_ref[1]
    lw = [lw_ref[g] for g in grp]
    cs = each(lambda x: sum(_dot(tri, p) for p in _split_bf16(x)), lw)
    tot = each(lambda x: jnp.sum(x, axis=0, keepdims=True), lw)
    at = [(-kk_ref[g].astype(F32) * jnp.exp(cs[g] - lw[g])).astype(BF16) for g in grp]
    rt = [(r_ref[g].astype(F32) * jnp.exp(cs[g])).astype(BF16) for g in grp]
    lhs = each(lambda a, b: jnp.concatenate([a, b], axis=0), at, rt)
    a1 = [_dot_t(lhs[g], bdiag((bb_ref[g].astype(F32) * jnp.exp(-cs[g])).astype(BF16))) for g in grp]
    a2 = [_dot_t(lhs[g], bdiag((kd_ref[g].astype(F32) * jnp.exp(-cs[g])).astype(BF16))) for g in grp]
    n = each(lambda a: a[:CHUNK] * ms, a1)
    a_rb = each(lambda a: (a[CHUNK:] * mi).astype(BF16), a1)
    a_ak = each(lambda a: (a[:CHUNK] * ms).astype(BF16), a2)
    a_rk = each(lambda a: (a[CHUNK:] * mi).astype(BF16), a2)

    base = mk_ref[2]
    pw = each(lambda x: x * base, n)
    t_inv = each(lambda x: eye_ref[...] + x, pw)
    for _ in range(2):
        pw = mm(pw, pw)
        t_inv = each(jnp.add, t_inv, mm(t_inv, pw))
    for lvl in range(3):
        pair = mk_ref[3 + lvl]
        xm = mm(each(lambda x: x * pair, n), t_inv)
        t_inv = each(jnp.add, t_inv, mm(t_inv, xm))

    st = [s_ref[g] for g in grp]
    st_bd = each(lambda x: bdiag(x.astype(BF16)), st)
    v_bd = [bdiag(v_ref[g]) for g in grp]
    rhs_u = [_dot_t(at[g], st_bd[g]) + _dot(a_ak[g], v_bd[g]) for g in grp]
    u = mm(t_inv, rhs_u)
    y_rb = mm(a_rb, u)
    for g in grp:
        y = _dot_t(rt[g], st_bd[g]) + y_rb[g] + _dot(a_rk[g], v_bd[g])
        y_ref[g] = y.astype(y_ref.dtype)

    for g in grp:
        e_rem = jnp.exp(tot[g] - cs[g])
        uv_t = jnp.concatenate([u[g], v_ref[g].astype(F32)], axis=0).T.astype(BF16)
        bk = jnp.concatenate([bb_ref[g].astype(F32) * e_rem, kd_ref[g].astype(F32) * e_rem],
                             axis=0).astype(BF16)
        full = _dot(uv_t, bk) * bdm_f
        upd = full[0:CHUNK]
        for hh in range(1, HEADS_PER_GROUP):
            upd = upd + full[hh * CHUNK:(hh + 1) * CHUNK]
        s_ref[g] = st[g] * jnp.exp(tot[g]) + upd


def _scan_consts():
    t = np.arange(CHUNK)
    tt, jj = t[:, None], t[None, :]
    lower = jj <= tt
    strict = jj < tt
    base = strict & (tt // 8 == jj // 8)
    pairs = [(tt // s == jj // s + 1) & (tt // (2 * s) == jj // (2 * s)) for s in (8, 16, 32)]
    fwd = np.stack([strict, lower, base] + pairs)
    both = np.stack([fwd, fwd.transpose(0, 2, 1)]).astype(np.float32)
    masks = np.tile(both, (1, 1, 1, HEADS_PER_GROUP))
    tri = np.stack([lower, lower.T]).astype(np.float32)
    eye = np.tile(np.eye(CHUNK, dtype=np.float32), (1, HEADS_PER_GROUP))
    blk = np.arange(GROUP_W) // HEAD_DIM
    bdm = (blk[:, None] == blk[None, :]).astype(np.float32)
    return jnp.asarray(tri, BF16), jnp.asarray(masks, F32), jnp.asarray(eye, F32), jnp.asarray(bdm, BF16)


def _rwkv_scan(r, v, kk, lw, kd, bb, s_in, nb):
    rows = r.shape[1]
    nc = rows // nb // CHUNK
    tri, masks, eye, bdm = _scan_consts()
    blk = lambda d, b, c: b * nc + c + d * (nc - 1 - 2 * c)
    shared = pl.BlockSpec((N_GROUPS, CHUNK, GROUP_W), lambda d, b, c: (0, blk(d, b, c), 0))
    perdir = pl.BlockSpec((None, N_GROUPS, CHUNK, GROUP_W), lambda d, b, c: (d, 0, blk(d, b, c), 0))
    sspec = pl.BlockSpec((None, None, N_GROUPS, HEAD_DIM, GROUP_W), lambda d, b, c: (d, b, 0, 0, 0))
    dsel = lambda shape: pl.BlockSpec((None,) + shape, lambda d, b, c: (d, 0, 0))
    return pl.pallas_call(
        _scan_kernel,
        grid=(2, nb, nc),
        in_specs=[shared, shared, shared, perdir, perdir, perdir, sspec,
                  dsel((CHUNK, CHUNK)),
                  pl.BlockSpec((None, 6, CHUNK, GROUP_W), lambda d, b, c: (d, 0, 0, 0)),
                  pl.BlockSpec((CHUNK, GROUP_W), lambda d, b, c: (0, 0)),
                  pl.BlockSpec((GROUP_W, GROUP_W), lambda d, b, c: (0, 0))],
        out_specs=[perdir, sspec],
        out_shape=[jax.ShapeDtypeStruct((2, N_GROUPS, rows, GROUP_W), BF16),
                   jax.ShapeDtypeStruct(s_in.shape, F32)],
        name="rwkv_scan",
        compiler_params=_cparams(("parallel", "parallel", "arbitrary")),
    )(r, v, kk, lw, kd, bb, s_in, tri, masks, eye, bdm)


def _rwkv_out_kernel(y_ref, bon_ref, sg_ref, lnw_ref, lnb_ref, bd_ref,
                     x_ref, gate_ref, gpost_ref, w_ref, o_ref):
    bd = bd_ref[...]
    acc = jnp.zeros(o_ref.shape, F32)
    for g in range(N_GROUPS):
        yy = y_ref[0, g].astype(F32) + y_ref[1, g].astype(F32)
        mu = _head_sum(yy, bd) * (1.0 / HEAD_DIM)
        d = yy - mu
        var = _head_sum(d * d, bd) * (1.0 / HEAD_DIM)
        yn = d * lax.rsqrt(var + GN_EPS) * lnw_ref[g] + lnb_ref[g]
        o = (yn + bon_ref[g].astype(F32)) * sg_ref[g].astype(F32)
        acc = acc + _dot(o.astype(BF16), w_ref[g])
    o_ref[...] = _post(x_ref[...], acc, gate_ref[...], gpost_ref[...])


def _rwkv_out(y, bon, sg, p, x2, nb, gate, g_post, tm):
    rows = x2.shape[0]
    nt = rows // nb // tm
    row_map = lambda b, i: (b * nt + i, 0)
    gm = pl.BlockSpec((N_GROUPS, tm, GROUP_W), lambda b, i: (0, b * nt + i, 0))
    pv = pl.BlockSpec((N_GROUPS, 1, GROUP_W), lambda b, i: (0, 0, 0))
    return pl.pallas_call(
        _rwkv_out_kernel,
        grid=(nb, nt),
        in_specs=[pl.BlockSpec((2, N_GROUPS, tm, GROUP_W), lambda b, i: (0, 0, b * nt + i, 0)),
                  gm, gm, pv, pv,
                  pl.BlockSpec((LANES, LANES), lambda b, i: (0, 0)),
                  pl.BlockSpec((tm, D_MODEL), row_map),
                  pl.BlockSpec((None, 1, D_MODEL), lambda b, i: (b, 0, 0)),
                  pl.BlockSpec((1, D_MODEL), lambda b, i: (0, 0)),
                  pl.BlockSpec((N_GROUPS, GROUP_W, D_MODEL), lambda b, i: (0, 0, 0))],
        out_specs=pl.BlockSpec((tm, D_MODEL), row_map),
        out_shape=jax.ShapeDtypeStruct((rows, D_MODEL), F32),
        name="rwkv_out",
        compiler_params=_cparams(("parallel", "parallel")),
    )(y, bon, sg, p['ln_w'], p['ln_b'], p['bd'], x2, gate, g_post, p['w_out'])


def _tile(t_len, pref):
    return pref if t_len % pref == 0 else t_len


def _rwkv_layer(xs, cs, nb, mods, g_pre, g_post, p, need_ctx):
    t_x, t_c = xs.shape[0] // nb, cs.shape[0] // nb
    (s1p_x, sh_x, gate_x), (s1p_c, sh_c, gate_c) = mods
    zeros = jnp.zeros((2, nb, N_GROUPS, HEAD_DIM, GROUP_W), F32)
    rc, vc, kkc, bonc, sgc, lwc, kdc, bbc = _rwkv_prep(cs, nb, g_pre, s1p_c, sh_c, p, _tile(t_c, 256))
    yc, state_c = _rwkv_scan(rc, vc, kkc, lwc, kdc, bbc, zeros, nb)
    rx, vx, kkx, bonx, sgx, lwx, kdx, bbx = _rwkv_prep(xs, nb, g_pre, s1p_x, sh_x, p, _tile(t_x, 512))
    yx, _ = _rwkv_scan(rx, vx, kkx, lwx, kdx, bbx, state_c, nb)
    xs_new = _rwkv_out(yx, bonx, sgx, p, xs, nb, gate_x, g_post, _tile(t_x, 512))
    cs_new = _rwkv_out(yc, bonc, sgc, p, cs, nb, gate_c, g_post, _tile(t_c, 256)) if need_ctx else cs
    return xs_new, cs_new


def _conv_layer(xs, cs, nb, mods, g_pre, g_post, p, need_ctx):
    def run(x2, mod, t_len):
        s1p, sh, gate = mod
        u, sgate = _inproj(x2, nb, g_pre, s1p, sh, p['w_in'], p['b_in'], 3, 2, _conv_epi, _tile(t_len, 1024))
        uc = _dwconv(u, nb, p['dw'], p['dw_b'], _tile(t_len, 128))
        vspec = pl.BlockSpec((1, D_INNER), lambda b, i: (0, 0))
        return _outproj([uc, sgate], [(p['ln_w'], vspec), (p['ln_b'], vspec)], _conv_pro,
                        x2, nb, gate, g_post, p['w_out'], p['b_out'], _tile(t_len, 512))
    xs_new = run(xs, mods[0], xs.shape[0] // nb)
    cs_new = run(cs, mods[1], cs.shape[0] // nb) if need_ctx else cs
    return xs_new, cs_new


def _na_layer(xs, cs, nb, mods, g_pre, g_post, p, need_ctx):
    t_x, t_c = xs.shape[0] // nb, cs.shape[0] // nb
    (s1p_x, sh_x, gate_x), (s1p_c, sh_c, gate_c) = mods
    q, k, v, sg = _inproj(xs, nb, g_pre, s1p_x, sh_x, p['w_in'], None, 4, 4, _na_epi, _tile(t_x, 1024))
    qc, kc, vc, sgc = _inproj(cs, nb, g_pre, s1p_c, sh_c, p['w_in'], None, 4, 4, _na_epi, _tile(t_c, 256))
    o = _na_attention(q, k, v, kc, vc, _na_bias_table(p['rpb'], t_x // GRID_W), nb)
    xs_new = _outproj([o, sg], [], _na_pro, xs, nb, gate_x, g_post, p['w_out'], None, _tile(t_x, 512))
    cs_new = cs
    if need_ctx:
        oc = _ctx_attention(qc, kc, vc, nb)
        cs_new = _outproj([oc, sgc], [], _na_pro, cs, nb, gate_c, g_post, p['w_out'], None, _tile(t_c, 256))
    return xs_new, cs_new


def kernel(x, c, ctx, c_ctx, ada_w, ada_b, norm_pre, norm_post, rw_mu, rw_w_in, rw_w0, rw_w1, rw_w2, rw_a0, rw_a1, rw_a2, rw_k_k, rw_k_a, rw_r_k, rw_ln_w, rw_ln_b, rw_w_out, cf_w_in, cf_b_in, cf_dw, cf_dw_b, cf_ln_w, cf_ln_b, cf_w_out, cf_b_out, na_w_in, na_rpb, na_w_out):
    nb, t_len, _ = x.shape
    ctx_len = ctx.shape[1]
    depth = ada_w.shape[0]
    xs = x.reshape(nb * t_len, D_MODEL)
    cs = ctx.reshape(nb * ctx_len, D_MODEL)

    cc = jnp.zeros((8, D_MODEL), F32).at[:nb].set(c).at[nb].set(c_ctx)
    mod = _modulation(cc, ada_w, ada_b)
    bd = jnp.asarray(np.kron(np.eye(LANES // HEAD_DIM), np.ones((HEAD_DIM, HEAD_DIM))), BF16)

    for i in range(depth):
        kind, j = i % 3, i // 3
        need_ctx = i < depth - 1
        shift, scale, gate = (mod[i, :, s * D_MODEL:(s + 1) * D_MODEL] for s in range(3))
        mods = []
        for sel in (slice(0, nb), slice(nb, nb + 1)):
            mods.append(tuple(jnp.broadcast_to(m[sel], (nb, D_MODEL)).reshape(nb, 1, D_MODEL)
                              for m in (1.0 + scale, shift, gate)))
        g_pre = norm_pre[i].reshape(1, D_MODEL)
        g_post = norm_post[i].reshape(1, D_MODEL)
        if kind == 0:
            p = dict(mu=rw_mu[j], w_in=rw_w_in[j].astype(BF16),
                     w0=rw_w0[j].reshape(2, 1, D_INNER), w1=rw_w1[j].astype(BF16), w2=rw_w2[j].astype(BF16),
                     a0=rw_a0[j].reshape(2, 1, D_INNER), a1=rw_a1[j].astype(BF16), a2=rw_a2[j].astype(BF16),
                     k_k=rw_k_k[j].reshape(1, D_INNER), k_a=rw_k_a[j].reshape(1, D_INNER),
                     r_k=rw_r_k[j].reshape(1, D_INNER),
                     ln_w=rw_ln_w[j].reshape(N_GROUPS, 1, GROUP_W), ln_b=rw_ln_b[j].reshape(N_GROUPS, 1, GROUP_W),
                     w_out=rw_w_out[j].astype(BF16).reshape(N_GROUPS, GROUP_W, D_MODEL), bd=bd)
            xs, cs = _rwkv_layer(xs, cs, nb, mods, g_pre, g_post, p, need_ctx)
        elif kind == 1:
            p = dict(w_in=cf_w_in[j].astype(BF16), b_in=cf_b_in[j].reshape(1, 3 * D_INNER),
                     dw=cf_dw[j], dw_b=cf_dw_b[j].reshape(1, D_INNER),
                     ln_w=cf_ln_w[j].reshape(1, D_INNER), ln_b=cf_ln_b[j].reshape(1, D_INNER),
                     w_out=cf_w_out[j].astype(BF16), b_out=cf_b_out[j].reshape(1, D_MODEL))
            xs, cs = _conv_layer(xs, cs, nb, mods, g_pre, g_post, p, need_ctx)
        else:
            p = dict(w_in=na_w_in[j].astype(BF16), rpb=na_rpb[j], w_out=na_w_out[j].astype(BF16))
            xs, cs = _na_layer(xs, cs, nb, mods, g_pre, g_post, p, need_ctx)
    return xs.reshape(nb, t_len, D_MODEL)
```

```python
import functools
import math

import jax
import jax.numpy as jnp
import numpy as np
from jax import lax
from jax.experimental import pallas as pl
from jax.experimental.pallas import tpu as pltpu

F32 = jnp.float32
BF16 = jnp.bfloat16

D_MODEL = 1024
D_INNER = 2048
N_HEADS = 32
HEAD_DIM = 64
LORA = 64
GRID_W = 64
CONV_WIDTH = 31
NA_ROWS = 8
NA_COLS = 16
RMS_EPS = 1e-6
LN_EPS = 1e-5
GN_EPS = 6.4e-4

LANES = 128
GROUP_W = 256
N_GROUPS = D_INNER // GROUP_W
HEADS_PER_PAIR = LANES // HEAD_DIM
CHUNK = 64
CONV_HALO = 16
NA_UNROLL = 4
NEG_BIG = -1e30
VMEM_LIMIT = 56 * 1024 * 1024


def _cparams(sem):
    return pltpu.CompilerParams(dimension_semantics=sem, vmem_limit_bytes=VMEM_LIMIT)


def _dot(a, b):
    return jnp.dot(a, b, preferred_element_type=F32)


def _dot_t(a, b):
    return lax.dot_general(a, b, (((1,), (1,)), ((), ())), preferred_element_type=F32)


def _split_bf16(x):
    hi = x.astype(BF16)
    lo = (x - hi.astype(F32)).astype(BF16)
    return hi, lo


def _sigmoid(x):
    return 1.0 / (1.0 + jnp.exp(-x))


def _silu(x):
    return x * _sigmoid(x)


def _modnorm(x, g, s1p, sh):
    ms = jnp.mean(x * x, axis=-1, keepdims=True)
    return (x * lax.rsqrt(ms + RMS_EPS) * g) * s1p + sh


def _post(xres, ox, gate, gpost):
    ms = jnp.mean(ox * ox, axis=-1, keepdims=True)
    return xres + gate * (ox * lax.rsqrt(ms + RMS_EPS) * gpost)


def _head_sum(x, bd):
    outs = [_dot(x[:, s * LANES:(s + 1) * LANES].astype(BF16), bd) for s in range(x.shape[1] // LANES)]
    return jnp.concatenate(outs, axis=1) if len(outs) > 1 else outs[0]


def _mod_kernel(c_ref, w_ref, b_ref, o_ref):
    a_hi, a_lo = _split_bf16(_silu(c_ref[...]))
    w_hi, w_lo = _split_bf16(w_ref[...])
    o_ref[...] = _dot(a_hi, w_hi) + _dot(a_hi, w_lo) + _dot(a_lo, w_hi) + b_ref[...]


def _modulation(cc, ada_w, ada_b):
    depth = ada_w.shape[0]
    tn = 1024
    return pl.pallas_call(
        _mod_kernel,
        grid=(depth, 3 * D_MODEL // tn),
        in_specs=[pl.BlockSpec((8, D_MODEL), lambda l, j: (0, 0)),
                  pl.BlockSpec((None, D_MODEL, tn), lambda l, j: (l, 0, j)),
                  pl.BlockSpec((None, 1, tn), lambda l, j: (l, 0, j))],
        out_specs=pl.BlockSpec((None, 8, tn), lambda l, j: (l, 0, j)),
        out_shape=jax.ShapeDtypeStruct((depth, 8, 3 * D_MODEL), F32),
        name="adaln_mod",
        compiler_params=_cparams(("parallel", "parallel")),
    )(cc, ada_w, ada_b.reshape(depth, 1, 3 * D_MODEL))


def _inproj_kernel(ns, has_bias, epi, x_ref, g_ref, s_ref, sh_ref, *rest):
    w_refs = rest[:ns]
    rest = rest[ns:]
    b_refs = rest[:ns] if has_bias else None
    rest = rest[ns:] if has_bias else rest
    out_refs, hn_ref = rest[:-1], rest[-1]

    @pl.when(pl.program_id(2) == 0)
    def _():
        hn_ref[...] = _modnorm(x_ref[...], g_ref[...], s_ref[...], sh_ref[...]).astype(BF16)

    hn = hn_ref[...]
    ys = []
    for s in range(ns):
        y = _dot(hn, w_refs[s][...])
        if has_bias:
            y = y + b_refs[s][...]
        ys.append(y)
    for o_ref, o in zip(out_refs, epi(ys)):
        o_ref[...] = o.astype(o_ref.dtype)


def _inproj(x2, nb, g_pre, s1p, sh, w_bf, bias, ns, n_out, epi, tm, tn=512):
    rows = x2.shape[0]
    nt = rows // nb // tm
    nj = D_INNER // tn
    row_map = lambda b, i, j: (b * nt + i, 0)
    vec_map = lambda b, i, j: (b, 0, 0)
    in_specs = [pl.BlockSpec((tm, D_MODEL), row_map),
                pl.BlockSpec((1, D_MODEL), lambda b, i, j: (0, 0)),
                pl.BlockSpec((None, 1, D_MODEL), vec_map),
                pl.BlockSpec((None, 1, D_MODEL), vec_map)]
    args = [x2, g_pre, s1p, sh]
    for s in range(ns):
        in_specs.append(pl.BlockSpec((D_MODEL, tn), functools.partial(lambda s, b, i, j: (0, s * nj + j), s)))
        args.append(w_bf)
    if bias is not None:
        for s in range(ns):
            in_specs.append(pl.BlockSpec((1, tn), functools.partial(lambda s, b, i, j: (0, s * nj + j), s)))
            args.append(bias)
    out_spec = pl.BlockSpec((tm, tn), lambda b, i, j: (b * nt + i, j))
    return pl.pallas_call(
        functools.partial(_inproj_kernel, ns, bias is not None, epi),
        grid=(nb, nt, nj),
        in_specs=in_specs,
        out_specs=[out_spec] * n_out,
        out_shape=[jax.ShapeDtypeStruct((rows, D_INNER), BF16)] * n_out,
        scratch_shapes=[pltpu.VMEM((tm, D_MODEL), BF16)],
        name="inproj%d" % ns,
        compiler_params=_cparams(("parallel", "parallel", "arbitrary")),
    )(*args)


def _conv_epi(ys):
    u, ug, gt = ys
    return [u * _sigmoid(ug), _silu(gt)]


def _na_epi(ys):
    q, k, v, g = ys
    return [q * (HEAD_DIM ** -0.5), k, v, _silu(g)]


def _outproj_kernel(pro, has_bias, nin, *refs):
    in_refs = refs[:nin]
    x_ref, gate_ref, gpost_ref, w_ref = refs[nin:nin + 4]
    b_ref = refs[nin + 4] if has_bias else None
    o_ref = refs[-1]
    a = pro(*in_refs)
    ox = _dot(a, w_ref[...])
    if has_bias:
        ox = ox + b_ref[...]
    o_ref[...] = _post(x_ref[...], ox, gate_ref[...], gpost_ref[...])


def _outproj(ins, in_specs_extra, pro, x2, nb, gate, g_post, w_bf, bias, tm):
    rows = x2.shape[0]
    nt = rows // nb // tm
    row_map = lambda b, i: (b * nt + i, 0)
    in_specs = [pl.BlockSpec((tm, D_INNER), row_map) for _ in ins]
    args = list(ins)
    for arr, spec in in_specs_extra:
        in_specs.append(spec)
        args.append(arr)
    nin = len(args)
    in_specs += [pl.BlockSpec((tm, D_MODEL), row_map),
                 pl.BlockSpec((None, 1, D_MODEL), lambda b, i: (b, 0, 0)),
                 pl.BlockSpec((1, D_MODEL), lambda b, i: (0, 0)),
                 pl.BlockSpec((D_INNER, D_MODEL), lambda b, i: (0, 0))]
    args += [x2, gate, g_post, w_bf]
    if bias is not None:
        in_specs.append(pl.BlockSpec((1, D_MODEL), lambda b, i: (0, 0)))
        args.append(bias)
    return pl.pallas_call(
        functools.partial(_outproj_kernel, pro, bias is not None, nin),
        grid=(nb, nt),
        in_specs=in_specs,
        out_specs=pl.BlockSpec((tm, D_MODEL), row_map),
        out_shape=jax.ShapeDtypeStruct((rows, D_MODEL), F32),
        name="outproj_" + pro.__name__.strip("_"),
        compiler_params=_cparams(("parallel", "parallel")),
    )(*args)


def _na_pro(o_ref, sg_ref):
    return o_ref[...] * sg_ref[...]


def _conv_pro(u_ref, sg_ref, lnw_ref, lnb_ref):
    u = u_ref[...].astype(F32)
    mu = jnp.mean(u, axis=-1, keepdims=True)
    d = u - mu
    var = jnp.mean(d * d, axis=-1, keepdims=True)
    y = d * lax.rsqrt(var + LN_EPS) * lnw_ref[...] + lnb_ref[...]
    return (_silu(y) * sg_ref[...].astype(F32)).astype(BF16)


def _dwconv_kernel(tm, rb, nt, u_ref, up_ref, un_ref, dw_ref, db_ref, o_ref, buf_ref):
    i = pl.program_id(1)
    keep_prev = (i > 0).astype(F32)
    keep_next = (i < nt - 1).astype(F32)
    buf_ref[0:CONV_HALO, :] = up_ref[...].astype(F32) * keep_prev
    buf_ref[CONV_HALO:CONV_HALO + tm, :] = u_ref[...].astype(F32)
    buf_ref[CONV_HALO + tm:2 * CONV_HALO + tm, :] = un_ref[...].astype(F32) * keep_next
    dw = dw_ref[...]
    off = CONV_HALO - CONV_WIDTH // 2
    span = rb + 2 * CONV_HALO
    for r0 in range(0, tm, rb):
        acc = jnp.zeros((rb, u_ref.shape[1]), F32) + db_ref[...]
        win = buf_ref[r0:r0 + span, :]
        for s in range(8):
            shifted = win if s == 0 else pltpu.roll(win, span - s, axis=0)
            for a in range(span // 8):
                k = 8 * a + s - off
                if 0 <= k < CONV_WIDTH:
                    acc = acc + shifted[8 * a:8 * a + rb, :] * dw[k:k + 1, :]
        o_ref[r0:r0 + rb, :] = acc.astype(o_ref.dtype)


def _dwconv(u2, nb, dw, dw_b, tm, tc=512, rb=32):
    rows = u2.shape[0]
    t_len = rows // nb
    nt = t_len // tm
    nj = D_INNER // tc
    hb = tm // CONV_HALO
    nhb = rows // CONV_HALO
    prev_map = lambda b, i, j: (jnp.maximum((b * nt + i) * hb - 1, 0), j)
    next_map = lambda b, i, j: (jnp.minimum((b * nt + i + 1) * hb, nhb - 1), j)
    return pl.pallas_call(
        functools.partial(_dwconv_kernel, tm, rb, nt),
        grid=(nb, nt, nj),
        in_specs=[pl.BlockSpec((tm, tc), lambda b, i, j: (b * nt + i, j)),
                  pl.BlockSpec((CONV_HALO, tc), prev_map),
                  pl.BlockSpec((CONV_HALO, tc), next_map),
                  pl.BlockSpec((CONV_WIDTH, tc), lambda b, i, j: (0, j)),
                  pl.BlockSpec((1, tc), lambda b, i, j: (0, j))],
        out_specs=pl.BlockSpec((tm, tc), lambda b, i, j: (b * nt + i, j)),
        out_shape=jax.ShapeDtypeStruct((rows, D_INNER), BF16),
        scratch_shapes=[pltpu.VMEM((tm + 2 * CONV_HALO, tc), F32)],
        name="dwconv",
        compiler_params=_cparams(("parallel", "parallel", "parallel")),
    )(u2, u2, u2, dw, dw_b)


def _stack_heads(q, lane_lo):
    zero = jnp.zeros_like(q)
    return jnp.concatenate([jnp.where(lane_lo, q, zero), jnp.where(lane_lo, zero, q)], axis=0)


def _na_kernel(n_rows, q_ref, k_ref, v_ref, kc_ref, vc_ref, bias_ref, o_ref):
    kr = min(NA_ROWS, n_rows)
    nloc = kr * GRID_W
    lane_lo = lax.broadcasted_iota(jnp.int32, (GRID_W, LANES), 1) < HEAD_DIM
    kc = kc_ref[...]
    vc = vc_ref[...]

    def rows(it, carry):
        rr = [it * NA_UNROLL + u for u in range(NA_UNROLL)]
        start = [jnp.clip(r - kr // 2, 0, n_rows - kr) for r in rr]
        q0 = [pl.multiple_of(r * GRID_W, GRID_W) for r in rr]
        k0 = [pl.multiple_of(s * GRID_W, GRID_W) for s in start]
        qm = [_stack_heads(q_ref[pl.ds(q, GRID_W), :], lane_lo) for q in q0]
        s_loc = [_dot_t(a, k_ref[pl.ds(k, nloc), :]) for a, k in zip(qm, k0)]
        s_ctx = [_dot_t(a, kc) for a in qm]
        s_loc = [s + bias_ref[r - st, 0] for s, r, st in zip(s_loc, rr, start)]
        m = [jnp.maximum(jnp.max(a, axis=-1, keepdims=True), jnp.max(b, axis=-1, keepdims=True))
             for a, b in zip(s_loc, s_ctx)]
        p_loc = [jnp.exp(a - mm) for a, mm in zip(s_loc, m)]
        p_ctx = [jnp.exp(a - mm) for a, mm in zip(s_ctx, m)]
        den = [jnp.sum(a, axis=-1, keepdims=True) + jnp.sum(b, axis=-1, keepdims=True)
               for a, b in zip(p_loc, p_ctx)]
        o2 = [_dot(a.astype(BF16), v_ref[pl.ds(k, nloc), :]) + _dot(b.astype(BF16), vc)
              for a, b, k in zip(p_loc, p_ctx, k0)]
        for o, d, q in zip(o2, den, q0):
            o = o / d
            o_ref[pl.ds(q, GRID_W), :] = jnp.where(lane_lo, o[:GRID_W], o[GRID_W:]).astype(o_ref.dtype)
        return carry

    lax.fori_loop(0, n_rows // NA_UNROLL, rows, 0)


def _na_attention(q, k, v, kc, vc, bias8, nb):
    rows = q.shape[0]
    t_len = rows // nb
    n_rows = t_len // GRID_W
    ctx_len = kc.shape[0] // nb
    nhp = D_INNER // LANES
    kr = min(NA_ROWS, n_rows)
    big = pl.BlockSpec((t_len, LANES), lambda b, h: (b, h))
    cspec = pl.BlockSpec((ctx_len, LANES), lambda b, h: (b, h))
    return pl.pallas_call(
        functools.partial(_na_kernel, n_rows),
        grid=(nb, nhp),
        in_specs=[big, big, big, cspec, cspec,
                  pl.BlockSpec((NA_ROWS, 1, LANES, kr * GRID_W), lambda b, h: (0, h, 0, 0))],
        out_specs=big,
        out_shape=jax.ShapeDtypeStruct((rows, D_INNER), BF16),
        name="na_attn",
        compiler_params=_cparams(("parallel", "parallel")),
    )(q, k, v, kc, vc, bias8)


def _ctx_attn_kernel(q_ref, k_ref, v_ref, o_ref):
    n = q_ref.shape[0]
    lane_lo = lax.broadcasted_iota(jnp.int32, (n, LANES), 1) < HEAD_DIM
    qm = _stack_heads(q_ref[...], lane_lo)
    s = _dot_t(qm, k_ref[...])
    m = jnp.max(s, axis=-1, keepdims=True)
    p = jnp.exp(s - m)
    den = jnp.sum(p, axis=-1, keepdims=True)
    o2 = _dot(p.astype(BF16), v_ref[...]) / den
    o_ref[...] = jnp.where(lane_lo, o2[:n], o2[n:]).astype(o_ref.dtype)


def _ctx_attention(qc, kc, vc, nb):
    ctx_len = qc.shape[0] // nb
    spec = pl.BlockSpec((ctx_len, LANES), lambda b, h: (b, h))
    return pl.pallas_call(
        _ctx_attn_kernel,
        grid=(nb, D_INNER // LANES),
        in_specs=[spec, spec, spec],
        out_specs=spec,
        out_shape=jax.ShapeDtypeStruct(qc.shape, BF16),
        name="ctx_attn",
        compiler_params=_cparams(("parallel", "parallel")),
    )(qc, kc, vc)


def _na_bias_table(rpb, n_rows):
    kr = min(NA_ROWS, n_rows)
    cols = np.arange(GRID_W)
    cstart = np.clip(cols - NA_COLS // 2, 0, GRID_W - NA_COLS)
    jc = np.arange(GRID_W)
    inwin = (jc[None, :] >= cstart[:, None]) & (jc[None, :] < cstart[:, None] + NA_COLS)
    pad = jnp.pad(rpb, ((0, 0), (0, 0), (GRID_W, GRID_W)))
    off = GRID_W + NA_COLS - 1
    toep = jnp.stack([pad[:, :, off - c:off - c + GRID_W] for c in range(GRID_W)], axis=2)
    toep = jnp.where(inwin[None, None], toep, NEG_BIG)
    tab = jnp.stack([toep[:, NA_ROWS - 1 - d:NA_ROWS - 1 - d + kr] for d in range(NA_ROWS)])
    tab = tab.transpose(0, 1, 3, 2, 4)
    return tab.reshape(NA_ROWS, N_HEADS // 2, 2 * GRID_W, kr * GRID_W).astype(F32)


def _rwkv_prep_kernel(tm, nt, x_ref, xp_ref, xn_ref, g_ref, s_ref, sh_ref, mu_ref,
                      wr_ref, wk_ref, wv_ref, wg_ref, w1_ref, w2_ref, w0_ref,
                      a1_ref, a2_ref, a0_ref, kk_ref, ka_ref, rk_ref, bd_ref,
                      r_o, v_o, kkn_o, bon_o, sg_o, lw_o, kd_o, bb_o,
                      xs_ref, hw_ref, ha_ref):
    i = pl.program_id(1)

    @pl.when(pl.program_id(2) == 0)
    def _():
        g, s1p, sh = g_ref[...], s_ref[...], sh_ref[...]
        h = _modnorm(x_ref[...], g, s1p, sh)
        hp = _modnorm(xp_ref[...], g, s1p, sh)[7:8] * (i > 0).astype(F32)
        hn = _modnorm(xn_ref[...], g, s1p, sh)[0:1] * (i < nt - 1).astype(F32)
        rows = lax.broadcasted_iota(jnp.int32, (tm, 1), 0)
        prev = jnp.where(rows == 0, hp, pltpu.roll(h, 1, axis=0))
        nxt = jnp.where(rows == tm - 1, hn, pltpu.roll(h, tm - 1, axis=0))
        ts = 0.5 * (prev + nxt) - h
        for s in range(6):
            xs_ref[s] = (h + ts * mu_ref[s:s + 1, :]).astype(BF16)
        for z in range(2):
            hw_ref[z] = jnp.tanh(_dot(xs_ref[4], w1_ref[z])).astype(BF16)
            ha_ref[z] = _dot(xs_ref[5], a1_ref[z]).astype(BF16)

    bd = bd_ref[...]
    r = _dot(xs_ref[0], wr_ref[...])
    k = _dot(xs_ref[1], wk_ref[...])
    v = _dot(xs_ref[2], wv_ref[...])
    g = _dot(xs_ref[3], wg_ref[...])
    kkh = k * kk_ref[...]
    kk = kkh * lax.rsqrt(_head_sum(kkh * kkh, bd) + 1e-12)
    ksum = jnp.zeros_like(k)
    for z in range(2):
        w_raw = w0_ref[z] + _dot(hw_ref[z], w2_ref[z])
        lw_o[z] = -math.exp(-0.5) * _sigmoid(w_raw)
        a = _sigmoid(a0_ref[z] + _dot(ha_ref[z], a2_ref[z]))
        kd = k * (1.0 + (a - 1.0) * ka_ref[...])
        kd_o[z] = kd.astype(BF16)
        bb_o[z] = (kk * a).astype(BF16)
        ksum = ksum + kd
    r_o[...] = r.astype(BF16)
    v_o[...] = v.astype(BF16)
    kkn_o[...] = kk.astype(BF16)
    bon_o[...] = (_head_sum(r * ksum * rk_ref[...], bd) * v).astype(BF16)
    sg_o[...] = _silu(g).astype(BF16)


def _rwkv_prep(x2, nb, g_pre, s1p, sh, p, tm):
    rows = x2.shape[0]
    nt = rows // nb // tm
    tn = GROUP_W
    hb = tm // 8
    nhb = rows // 8
    c3 = lambda b, i, j: (0, 0)
    vec = lambda b, i, j: (b, 0, 0)
    colv = pl.BlockSpec((1, tn), lambda b, i, j: (0, j))
    in_specs = [
        pl.BlockSpec((tm, D_MODEL), lambda b, i, j: (b * nt + i, 0)),
        pl.BlockSpec((8, D_MODEL), lambda b, i, j: (jnp.maximum((b * nt + i) * hb - 1, 0), 0)),
        pl.BlockSpec((8, D_MODEL), lambda b, i, j: (jnp.minimum((b * nt + i + 1) * hb, nhb - 1), 0)),
        pl.BlockSpec((1, D_MODEL), c3),
        pl.BlockSpec((None, 1, D_MODEL), vec),
        pl.BlockSpec((None, 1, D_MODEL), vec),
        pl.BlockSpec((6, D_MODEL), c3),
    ]
    args = [x2, x2, x2, g_pre, s1p, sh, p['mu']]
    for s in range(4):
        in_specs.append(pl.BlockSpec((None, D_MODEL, tn), functools.partial(lambda s, b, i, j: (s, 0, j), s)))
        args.append(p['w_in'])
    lora_in = pl.BlockSpec((2, D_MODEL, LORA), lambda b, i, j: (0, 0, 0))
    lora_out = pl.BlockSpec((2, LORA, tn), lambda b, i, j: (0, 0, j))
    lora_b = pl.BlockSpec((2, 1, tn), lambda b, i, j: (0, 0, j))
    in_specs += [lora_in, lora_out, lora_b, lora_in, lora_out, lora_b, colv, colv, colv,
                 pl.BlockSpec((LANES, LANES), c3)]
    args += [p['w1'], p['w2'], p['w0'], p['a1'], p['a2'], p['a0'], p['k_k'], p['k_a'], p['r_k'], p['bd']]
    gm = pl.BlockSpec((None, tm, tn), lambda b, i, j: (j, b * nt + i, 0))
    gm2 = pl.BlockSpec((2, None, tm, tn), lambda b, i, j: (0, j, b * nt + i, 0))
    sh1 = jax.ShapeDtypeStruct((N_GROUPS, rows, tn), BF16)
    sh2 = jax.ShapeDtypeStruct((2, N_GROUPS, rows, tn), BF16)
    return pl.pallas_call(
        functools.partial(_rwkv_prep_kernel, tm, nt),
        grid=(nb, nt, N_GROUPS),
        in_specs=in_specs,
        out_specs=[gm, gm, gm, gm, gm, gm2, gm2, gm2],
        out_shape=[sh1, sh1, sh1, sh1, sh1, jax.ShapeDtypeStruct((2, N_GROUPS, rows, tn), F32), sh2, sh2],
        scratch_shapes=[pltpu.VMEM((6, tm, D_MODEL), BF16),
                        pltpu.VMEM((2, tm, LORA), BF16),
                        pltpu.VMEM((2, tm, LORA), BF16)],
        name="rwkv_prep",
        compiler_params=_cparams(("parallel", "parallel", "arbitrary")),
    )(*args)


def _block_diag(x_bf, bdm):
    return jnp.concatenate([x_bf] * HEADS_PER_PAIR, axis=0) * bdm


def _scan_kernel(r_ref, v_ref, kk_ref, lw_ref, kd_ref, bb_ref, sin_ref,
                 tri_ref, mk_ref, eye_ref, bdm_ref, y_ref, s_ref):
    @pl.when(pl.program_id(2) == 0)
    def _():
        s_ref[...] = sin_ref[...]

    tri = tri_ref[...]
    bdm = bdm_ref[...]
    bdm_f = bdm.astype(F32)

    def each(fn, *lists):
        return [fn(*args) for args in zip(*lists)]

    def bdiag(x):
        return _block_diag(x, bdm)

    def mm(a_list, b_list):
        return each(lambda a, b: _dot(a.astype(BF16), bdiag(b.astype(BF16))), a_list, b_list)

    grp = range(N_GROUPS * GROUP_W // LANES)

    def ld(ref, p):
        return ref[p // 2, :, (p % 2) * LANES:(p % 2 + 1) * LANES]

    ms, mi = mk_ref[0], mk_ref[1]
    lw = [ld(lw_ref, g) for g in grp]
    cs = each(lambda x: sum(_dot(tri, p) for p in _split_bf16(x)), lw)
    tot = each(lambda x: jnp.sum(x, axis=0, keepdims=True), lw)
    at = [(-ld(kk_ref, g).astype(F32) * jnp.exp(cs[g] - lw[g])).astype(BF16) for g in grp]
    rt = [(ld(r_ref, g).astype(F32) * jnp.exp(cs[g])).astype(BF16) for g in grp]
    lhs = each(lambda a, b: jnp.concatenate([a, b], axis=0), at, rt)
    a1 = [_dot_t(lhs[g], bdiag((ld(bb_ref, g).astype(F32) * jnp.exp(-cs[g])).astype(BF16))) for g in grp]
    a2 = [_dot_t(lhs[g], bdiag((ld(kd_ref, g).astype(F32) * jnp.exp(-cs[g])).astype(BF16))) for g in grp]
    n = each(lambda a: a[:CHUNK] * ms, a1)
    a_rb = each(lambda a: (a[CHUNK:] * mi).astype(BF16), a1)
    a_k = each(lambda a: jnp.concatenate([a[:CHUNK] * ms, a[CHUNK:] * mi], axis=0).astype(BF16), a2)
    st = [ld(s_ref, g) for g in grp]
    from_s = [_dot_t(lhs[g], bdiag(st[g].astype(BF16))) for g in grp]
    from_v = [_dot(a_k[g], bdiag(ld(v_ref, g))) for g in grp]

    base = mk_ref[2]
    pw = each(lambda x: x * base, n)
    t_inv = each(lambda x: eye_ref[...] + x, pw)
    for _ in range(2):
        pw = mm(pw, pw)
        t_inv = each(jnp.add, t_inv, mm(t_inv, pw))
    for lvl in range(3):
        pair = mk_ref[3 + lvl]
        xm = mm(each(lambda x: x * pair, n), t_inv)
        t_inv = each(jnp.add, t_inv, mm(t_inv, xm))

    rhs_u = [from_s[g][:CHUNK] + from_v[g][:CHUNK] for g in grp]
    u = mm(t_inv, rhs_u)
    y_rb = mm(a_rb, u)
    for g in grp:
        y = from_s[g][CHUNK:] + y_rb[g] + from_v[g][CHUNK:]
        y_ref[g // 2, :, (g % 2) * LANES:(g % 2 + 1) * LANES] = y.astype(y_ref.dtype)

    for g in grp:
        e_rem = jnp.exp(tot[g] - cs[g])
        uv_t = jnp.concatenate([u[g], ld(v_ref, g).astype(F32)], axis=0).T.astype(BF16)
        bk = jnp.concatenate([ld(bb_ref, g).astype(F32) * e_rem, ld(kd_ref, g).astype(F32) * e_rem],
                             axis=0).astype(BF16)
        full = _dot(uv_t, bk) * bdm_f
        upd = full[0:CHUNK]
        for hh in range(1, HEADS_PER_PAIR):
            upd = upd + full[hh * CHUNK:(hh + 1) * CHUNK]
        s_ref[g // 2, :, (g % 2) * LANES:(g % 2 + 1) * LANES] = st[g] * jnp.exp(tot[g]) + upd


def _scan_consts():
    t = np.arange(CHUNK)
    tt, jj = t[:, None], t[None, :]
    lower = jj <= tt
    strict = jj < tt
    base = strict & (tt // 8 == jj // 8)
    pairs = [(tt // s == jj // s + 1) & (tt // (2 * s) == jj // (2 * s)) for s in (8, 16, 32)]
    fwd = np.stack([strict, lower, base] + pairs)
    both = np.stack([fwd, fwd.transpose(0, 2, 1)]).astype(np.float32)
    masks = np.tile(both, (1, 1, 1, HEADS_PER_PAIR))
    tri = np.stack([lower, lower.T]).astype(np.float32)
    eye = np.tile(np.eye(CHUNK, dtype=np.float32), (1, HEADS_PER_PAIR))
    blk = np.arange(LANES) // HEAD_DIM
    bdm = (blk[:, None] == blk[None, :]).astype(np.float32)
    return jnp.asarray(tri, BF16), jnp.asarray(masks, F32), jnp.asarray(eye, F32), jnp.asarray(bdm, BF16)


def _rwkv_scan(r, v, kk, lw, kd, bb, s_in, nb):
    rows = r.shape[1]
    nc = rows // nb // CHUNK
    tri, masks, eye, bdm = _scan_consts()
    blk = lambda d, b, c: b * nc + c + d * (nc - 1 - 2 * c)
    shared = pl.BlockSpec((N_GROUPS, CHUNK, GROUP_W), lambda d, b, c: (0, blk(d, b, c), 0))
    perdir = pl.BlockSpec((None, N_GROUPS, CHUNK, GROUP_W), lambda d, b, c: (d, 0, blk(d, b, c), 0))
    sspec = pl.BlockSpec((None, None, N_GROUPS, HEAD_DIM, GROUP_W), lambda d, b, c: (d, b, 0, 0, 0))
    dsel = lambda shape: pl.BlockSpec((None,) + shape, lambda d, b, c: (d, 0, 0))
    return pl.pallas_call(
        _scan_kernel,
        grid=(2, nb, nc),
        in_specs=[shared, shared, shared, perdir, perdir, perdir, sspec,
                  dsel((CHUNK, CHUNK)),
                  pl.BlockSpec((None, 6, CHUNK, LANES), lambda d, b, c: (d, 0, 0, 0)),
                  pl.BlockSpec((CHUNK, LANES), lambda d, b, c: (0, 0)),
                  pl.BlockSpec((LANES, LANES), lambda d, b, c: (0, 0))],
        out_specs=[perdir, sspec],
        out_shape=[jax.ShapeDtypeStruct((2, N_GROUPS, rows, GROUP_W), BF16),
                   jax.ShapeDtypeStruct(s_in.shape, F32)],
        name="rwkv_scan",
        compiler_params=_cparams(("parallel", "parallel", "arbitrary")),
    )(r, v, kk, lw, kd, bb, s_in, tri, masks, eye, bdm)


def _rwkv_out_kernel(y_ref, bon_ref, sg_ref, lnw_ref, lnb_ref, bd_ref,
                     x_ref, gate_ref, gpost_ref, w_ref, o_ref):
    bd = bd_ref[...]
    acc = jnp.zeros(o_ref.shape, F32)
    for g in range(N_GROUPS):
        yy = y_ref[0, g].astype(F32) + y_ref[1, g].astype(F32)
        mu = _head_sum(yy, bd) * (1.0 / HEAD_DIM)
        d = yy - mu
        var = _head_sum(d * d, bd) * (1.0 / HEAD_DIM)
        yn = d * lax.rsqrt(var + GN_EPS) * lnw_ref[g] + lnb_ref[g]
        o = (yn + bon_ref[g].astype(F32)) * sg_ref[g].astype(F32)
        acc = acc + _dot(o.astype(BF16), w_ref[g])
    o_ref[...] = _post(x_ref[...], acc, gate_ref[...], gpost_ref[...])


def _rwkv_out(y, bon, sg, p, x2, nb, gate, g_post, tm):
    rows = x2.shape[0]
    nt = rows // nb // tm
    row_map = lambda b, i: (b * nt + i, 0)
    gm = pl.BlockSpec((N_GROUPS, tm, GROUP_W), lambda b, i: (0, b * nt + i, 0))
    pv = pl.BlockSpec((N_GROUPS, 1, GROUP_W), lambda b, i: (0, 0, 0))
    return pl.pallas_call(
        _rwkv_out_kernel,
        grid=(nb, nt),
        in_specs=[pl.BlockSpec((2, N_GROUPS, tm, GROUP_W), lambda b, i: (0, 0, b * nt + i, 0)),
                  gm, gm, pv, pv,
                  pl.BlockSpec((LANES, LANES), lambda b, i: (0, 0)),
                  pl.BlockSpec((tm, D_MODEL), row_map),
                  pl.BlockSpec((None, 1, D_MODEL), lambda b, i: (b, 0, 0)),
                  pl.BlockSpec((1, D_MODEL), lambda b, i: (0, 0)),
                  pl.BlockSpec((N_GROUPS, GROUP_W, D_MODEL), lambda b, i: (0, 0, 0))],
        out_specs=pl.BlockSpec((tm, D_MODEL), row_map),
        out_shape=jax.ShapeDtypeStruct((rows, D_MODEL), F32),
        name="rwkv_out",
        compiler_params=_cparams(("parallel", "parallel")),
    )(y, bon, sg, p['ln_w'], p['ln_b'], p['bd'], x2, gate, g_post, p['w_out'])


def _tile(t_len, pref):
    return pref if t_len % pref == 0 else t_len


def _rwkv_layer(xs, cs, nb, mods, g_pre, g_post, p, need_ctx):
    t_x, t_c = xs.shape[0] // nb, cs.shape[0] // nb
    (s1p_x, sh_x, gate_x), (s1p_c, sh_c, gate_c) = mods
    zeros = jnp.zeros((2, nb, N_GROUPS, HEAD_DIM, GROUP_W), F32)
    rc, vc, kkc, bonc, sgc, lwc, kdc, bbc = _rwkv_prep(cs, nb, g_pre, s1p_c, sh_c, p, _tile(t_c, 256))
    yc, state_c = _rwkv_scan(rc, vc, kkc, lwc, kdc, bbc, zeros, nb)
    rx, vx, kkx, bonx, sgx, lwx, kdx, bbx = _rwkv_prep(xs, nb, g_pre, s1p_x, sh_x, p, _tile(t_x, 1024))
    yx, _ = _rwkv_scan(rx, vx, kkx, lwx, kdx, bbx, state_c, nb)
    xs_new = _rwkv_out(yx, bonx, sgx, p, xs, nb, gate_x, g_post, _tile(t_x, 512))
    cs_new = _rwkv_out(yc, bonc, sgc, p, cs, nb, gate_c, g_post, _tile(t_c, 256)) if need_ctx else cs
    return xs_new, cs_new


def _conv_layer(xs, cs, nb, mods, g_pre, g_post, p, need_ctx):
    def run(x2, mod, t_len):
        s1p, sh, gate = mod
        u, sgate = _inproj(x2, nb, g_pre, s1p, sh, p['w_in'], p['b_in'], 3, 2, _conv_epi, _tile(t_len, 1024))
        uc = _dwconv(u, nb, p['dw'], p['dw_b'], _tile(t_len, 128))
        vspec = pl.BlockSpec((1, D_INNER), lambda b, i: (0, 0))
        return _outproj([uc, sgate], [(p['ln_w'], vspec), (p['ln_b'], vspec)], _conv_pro,
                        x2, nb, gate, g_post, p['w_out'], p['b_out'], _tile(t_len, 512))
    xs_new = run(xs, mods[0], xs.shape[0] // nb)
    cs_new = run(cs, mods[1], cs.shape[0] // nb) if need_ctx else cs
    return xs_new, cs_new


def _na_layer(xs, cs, nb, mods, g_pre, g_post, p, need_ctx):
    t_x, t_c = xs.shape[0] // nb, cs.shape[0] // nb
    (s1p_x, sh_x, gate_x), (s1p_c, sh_c, gate_c) = mods
    q, k, v, sg = _inproj(xs, nb, g_pre, s1p_x, sh_x, p['w_in'], None, 4, 4, _na_epi, _tile(t_x, 1024))
    qc, kc, vc, sgc = _inproj(cs, nb, g_pre, s1p_c, sh_c, p['w_in'], None, 4, 4, _na_epi, _tile(t_c, 256))
    o = _na_attention(q, k, v, kc, vc, _na_bias_table(p['rpb'], t_x // GRID_W), nb)
    xs_new = _outproj([o, sg], [], _na_pro, xs, nb, gate_x, g_post, p['w_out'], None, _tile(t_x, 512))
    cs_new = cs
    if need_ctx:
        oc = _ctx_attention(qc, kc, vc, nb)
        cs_new = _outproj([oc, sgc], [], _na_pro, cs, nb, gate_c, g_post, p['w_out'], None, _tile(t_c, 256))
    return xs_new, cs_new


def kernel(x, c, ctx, c_ctx, ada_w, ada_b, norm_pre, norm_post, rw_mu, rw_w_in, rw_w0, rw_w1, rw_w2, rw_a0, rw_a1, rw_a2, rw_k_k, rw_k_a, rw_r_k, rw_ln_w, rw_ln_b, rw_w_out, cf_w_in, cf_b_in, cf_dw, cf_dw_b, cf_ln_w, cf_ln_b, cf_w_out, cf_b_out, na_w_in, na_rpb, na_w_out):
    nb, t_len, _ = x.shape
    ctx_len = ctx.shape[1]
    depth = ada_w.shape[0]
    xs = x.reshape(nb * t_len, D_MODEL)
    cs = ctx.reshape(nb * ctx_len, D_MODEL)

    cc = jnp.zeros((8, D_MODEL), F32).at[:nb].set(c).at[nb].set(c_ctx)
    mod = _modulation(cc, ada_w, ada_b)
    bd = jnp.asarray(np.kron(np.eye(LANES // HEAD_DIM), np.ones((HEAD_DIM, HEAD_DIM))), BF16)

    for i in range(depth):
        kind, j = i % 3, i // 3
        need_ctx = i < depth - 1
        shift, scale, gate = (mod[i, :, s * D_MODEL:(s + 1) * D_MODEL] for s in range(3))
        mods = []
        for sel in (slice(0, nb), slice(nb, nb + 1)):
            mods.append(tuple(jnp.broadcast_to(m[sel], (nb, D_MODEL)).reshape(nb, 1, D_MODEL)
                              for m in (1.0 + scale, shift, gate)))
        g_pre = norm_pre[i].reshape(1, D_MODEL)
        g_post = norm_post[i].reshape(1, D_MODEL)
        if kind == 0:
            p = dict(mu=rw_mu[j], w_in=rw_w_in[j].astype(BF16),
                     w0=rw_w0[j].reshape(2, 1, D_INNER), w1=rw_w1[j].astype(BF16), w2=rw_w2[j].astype(BF16),
                     a0=rw_a0[j].reshape(2, 1, D_INNER), a1=rw_a1[j].astype(BF16), a2=rw_a2[j].astype(BF16),
                     k_k=rw_k_k[j].reshape(1, D_INNER), k_a=rw_k_a[j].reshape(1, D_INNER),
                     r_k=rw_r_k[j].reshape(1, D_INNER),
                     ln_w=rw_ln_w[j].reshape(N_GROUPS, 1, GROUP_W), ln_b=rw_ln_b[j].reshape(N_GROUPS, 1, GROUP_W),
                     w_out=rw_w_out[j].astype(BF16).reshape(N_GROUPS, GROUP_W, D_MODEL), bd=bd)
            xs, cs = _rwkv_layer(xs, cs, nb, mods, g_pre, g_post, p, need_ctx)
        elif kind == 1:
            p = dict(w_in=cf_w_in[j].astype(BF16), b_in=cf_b_in[j].reshape(1, 3 * D_INNER),
                     dw=cf_dw[j], dw_b=cf_dw_b[j].reshape(1, D_INNER),
                     ln_w=cf_ln_w[j].reshape(1, D_INNER), ln_b=cf_ln_b[j].reshape(1, D_INNER),
                     w_out=cf_w_out[j].astype(BF16), b_out=cf_b_out[j].reshape(1, D_MODEL))
            xs, cs = _conv_layer(xs, cs, nb, mods, g_pre, g_post, p, need_ctx)
        else:
            p = dict(w_in=na_w_in[j].astype(BF16), rpb=na_rpb[j], w_out=na_w_out[j].astype(BF16))
            xs, cs = _na_layer(xs, cs, nb, mods, g_pre, g_post, p, need_ctx)
    return xs.reshape(nb, t_len, D_MODEL)
```

```python
import functools
import math

import jax
import jax.numpy as jnp
import numpy as np
from jax import lax
from jax.experimental import pallas as pl
from jax.experimental.pallas import tpu as pltpu

F32 = jnp.float32
BF16 = jnp.bfloat16

D_MODEL = 1024
D_INNER = 2048
N_HEADS = 32
HEAD_DIM = 64
LORA = 64
GRID_W = 64
CONV_WIDTH = 31
NA_ROWS = 8
NA_COLS = 16
RMS_EPS = 1e-6
LN_EPS = 1e-5
GN_EPS = 6.4e-4

LANES = 128
GROUP_W = 256
N_GROUPS = D_INNER // GROUP_W
HEADS_PER_PAIR = LANES // HEAD_DIM
CHUNK = 64
CONV_HALO = 16
NA_UNROLL = 4
PIPE_ROWS = 256
NEG_BIG = -1e30
VMEM_LIMIT = 56 * 1024 * 1024


def _cparams(sem):
    return pltpu.CompilerParams(dimension_semantics=sem, vmem_limit_bytes=VMEM_LIMIT)


def _dot(a, b):
    return jnp.dot(a, b, preferred_element_type=F32)


def _dot_t(a, b):
    return lax.dot_general(a, b, (((1,), (1,)), ((), ())), preferred_element_type=F32)


def _split_bf16(x):
    hi = x.astype(BF16)
    lo = (x - hi.astype(F32)).astype(BF16)
    return hi, lo


def _sigmoid(x):
    return 1.0 / (1.0 + jnp.exp(-x))


def _silu(x):
    return x * _sigmoid(x)


def _modnorm(x, g, s1p, sh):
    ms = jnp.mean(x * x, axis=-1, keepdims=True)
    return (x * lax.rsqrt(ms + RMS_EPS) * g) * s1p + sh


def _post(xres, ox, gate, gpost):
    ms = jnp.mean(ox * ox, axis=-1, keepdims=True)
    return xres + gate * (ox * lax.rsqrt(ms + RMS_EPS) * gpost)


def _software_pipeline(nblk, project, finish):
    pending = project(0)
    for q in range(1, nblk):
        ahead = project(q)
        finish(q - 1, pending)
        pending = ahead
    finish(nblk - 1, pending)


def _head_sum(x, bd):
    outs = [_dot(x[:, s * LANES:(s + 1) * LANES].astype(BF16), bd) for s in range(x.shape[1] // LANES)]
    return jnp.concatenate(outs, axis=1) if len(outs) > 1 else outs[0]


def _mod_kernel(c_ref, w_ref, b_ref, o_ref):
    a_hi, a_lo = _split_bf16(_silu(c_ref[...]))
    w_hi, w_lo = _split_bf16(w_ref[...])
    o_ref[...] = _dot(a_hi, w_hi) + _dot(a_hi, w_lo) + _dot(a_lo, w_hi) + b_ref[...]


def _modulation(cc, ada_w, ada_b):
    depth = ada_w.shape[0]
    tn = 1024
    return pl.pallas_call(
        _mod_kernel,
        grid=(depth, 3 * D_MODEL // tn),
        in_specs=[pl.BlockSpec((8, D_MODEL), lambda l, j: (0, 0)),
                  pl.BlockSpec((None, D_MODEL, tn), lambda l, j: (l, 0, j)),
                  pl.BlockSpec((None, 1, tn), lambda l, j: (l, 0, j))],
        out_specs=pl.BlockSpec((None, 8, tn), lambda l, j: (l, 0, j)),
        out_shape=jax.ShapeDtypeStruct((depth, 8, 3 * D_MODEL), F32),
        name="adaln_mod",
        compiler_params=_cparams(("parallel", "parallel")),
    )(cc, ada_w, ada_b.reshape(depth, 1, 3 * D_MODEL))


def _inproj_kernel(ns, has_bias, epi, x_ref, g_ref, s_ref, sh_ref, *rest):
    w_refs = rest[:ns]
    rest = rest[ns:]
    b_refs = rest[:ns] if has_bias else None
    rest = rest[ns:] if has_bias else rest
    out_refs, hn_ref = rest[:-1], rest[-1]

    @pl.when(pl.program_id(2) == 0)
    def _():
        hn_ref[...] = _modnorm(x_ref[...], g_ref[...], s_ref[...], sh_ref[...]).astype(BF16)

    tm = hn_ref.shape[0]
    rb = min(tm, PIPE_ROWS)

    def project(q):
        hn = hn_ref[q * rb:(q + 1) * rb, :]
        return [_dot(hn, w_refs[s][...]) for s in range(ns)]

    def finish(q, ys):
        if has_bias:
            ys = [y + b_refs[s][...] for s, y in enumerate(ys)]
        for o_ref, o in zip(out_refs, epi(ys)):
            o_ref[q * rb:(q + 1) * rb, :] = o.astype(o_ref.dtype)

    _software_pipeline(tm // rb, project, finish)


def _inproj(x2, nb, g_pre, s1p, sh, w_bf, bias, ns, n_out, epi, tm, tn=512):
    rows = x2.shape[0]
    nt = rows // nb // tm
    nj = D_INNER // tn
    row_map = lambda b, i, j: (b * nt + i, 0)
    vec_map = lambda b, i, j: (b, 0, 0)
    in_specs = [pl.BlockSpec((tm, D_MODEL), row_map),
                pl.BlockSpec((1, D_MODEL), lambda b, i, j: (0, 0)),
                pl.BlockSpec((None, 1, D_MODEL), vec_map),
                pl.BlockSpec((None, 1, D_MODEL), vec_map)]
    args = [x2, g_pre, s1p, sh]
    for s in range(ns):
        in_specs.append(pl.BlockSpec((D_MODEL, tn), functools.partial(lambda s, b, i, j: (0, s * nj + j), s)))
        args.append(w_bf)
    if bias is not None:
        for s in range(ns):
            in_specs.append(pl.BlockSpec((1, tn), functools.partial(lambda s, b, i, j: (0, s * nj + j), s)))
            args.append(bias)
    out_spec = pl.BlockSpec((tm, tn), lambda b, i, j: (b * nt + i, j))
    return pl.pallas_call(
        functools.partial(_inproj_kernel, ns, bias is not None, epi),
        grid=(nb, nt, nj),
        in_specs=in_specs,
        out_specs=[out_spec] * n_out,
        out_shape=[jax.ShapeDtypeStruct((rows, D_INNER), BF16)] * n_out,
        scratch_shapes=[pltpu.VMEM((tm, D_MODEL), BF16)],
        name="inproj%d" % ns,
        compiler_params=_cparams(("parallel", "parallel", "arbitrary")),
    )(*args)


def _conv_epi(ys):
    u, ug, gt = ys
    return [u * _sigmoid(ug), _silu(gt)]


def _na_epi(ys):
    q, k, v, g = ys
    return [q * (HEAD_DIM ** -0.5), k, v, _silu(g)]


def _outproj_kernel(pro, has_bias, nin, *refs):
    in_refs = refs[:nin]
    x_ref, gate_ref, gpost_ref, w_ref = refs[nin:nin + 4]
    b_ref = refs[nin + 4] if has_bias else None
    o_ref = refs[-1]
    tm = o_ref.shape[0]
    rb = min(tm, PIPE_ROWS)

    def project(q):
        return _dot(pro(slice(q * rb, (q + 1) * rb), *in_refs), w_ref[...])

    def finish(q, ox):
        sl = slice(q * rb, (q + 1) * rb)
        if has_bias:
            ox = ox + b_ref[...]
        o_ref[sl, :] = _post(x_ref[sl, :], ox, gate_ref[...], gpost_ref[...])

    _software_pipeline(tm // rb, project, finish)


def _outproj(ins, in_specs_extra, pro, x2, nb, gate, g_post, w_bf, bias, tm):
    rows = x2.shape[0]
    nt = rows // nb // tm
    row_map = lambda b, i: (b * nt + i, 0)
    in_specs = [pl.BlockSpec((tm, D_INNER), row_map) for _ in ins]
    args = list(ins)
    for arr, spec in in_specs_extra:
        in_specs.append(spec)
        args.append(arr)
    nin = len(args)
    in_specs += [pl.BlockSpec((tm, D_MODEL), row_map),
                 pl.BlockSpec((None, 1, D_MODEL), lambda b, i: (b, 0, 0)),
                 pl.BlockSpec((1, D_MODEL), lambda b, i: (0, 0)),
                 pl.BlockSpec((D_INNER, D_MODEL), lambda b, i: (0, 0))]
    args += [x2, gate, g_post, w_bf]
    if bias is not None:
        in_specs.append(pl.BlockSpec((1, D_MODEL), lambda b, i: (0, 0)))
        args.append(bias)
    return pl.pallas_call(
        functools.partial(_outproj_kernel, pro, bias is not None, nin),
        grid=(nb, nt),
        in_specs=in_specs,
        out_specs=pl.BlockSpec((tm, D_MODEL), row_map),
        out_shape=jax.ShapeDtypeStruct((rows, D_MODEL), F32),
        name="outproj_" + pro.__name__.strip("_"),
        compiler_params=_cparams(("parallel", "parallel")),
    )(*args)


def _na_pro(sl, o_ref, sg_ref):
    return o_ref[sl, :] * sg_ref[sl, :]


def _conv_pro(sl, u_ref, sg_ref, lnw_ref, lnb_ref):
    u = u_ref[sl, :].astype(F32)
    mu = jnp.mean(u, axis=-1, keepdims=True)
    d = u - mu
    var = jnp.mean(d * d, axis=-1, keepdims=True)
    y = d * lax.rsqrt(var + LN_EPS) * lnw_ref[...] + lnb_ref[...]
    return (_silu(y) * sg_ref[sl, :].astype(F32)).astype(BF16)


def _dwconv_kernel(tm, rb, nt, u_ref, up_ref, un_ref, dw_ref, db_ref, o_ref, buf_ref):
    i = pl.program_id(1)
    keep_prev = (i > 0).astype(F32)
    keep_next = (i < nt - 1).astype(F32)
    buf_ref[0:CONV_HALO, :] = up_ref[...].astype(F32) * keep_prev
    buf_ref[CONV_HALO:CONV_HALO + tm, :] = u_ref[...].astype(F32)
    buf_ref[CONV_HALO + tm:2 * CONV_HALO + tm, :] = un_ref[...].astype(F32) * keep_next
    dw = dw_ref[...]
    off = CONV_HALO - CONV_WIDTH // 2
    span = rb + 2 * CONV_HALO
    for r0 in range(0, tm, rb):
        acc = jnp.zeros((rb, u_ref.shape[1]), F32) + db_ref[...]
        win = buf_ref[r0:r0 + span, :]
        for s in range(8):
            shifted = win if s == 0 else pltpu.roll(win, span - s, axis=0)
            for a in range(span // 8):
                k = 8 * a + s - off
                if 0 <= k < CONV_WIDTH:
                    acc = acc + shifted[8 * a:8 * a + rb, :] * dw[k:k + 1, :]
        o_ref[r0:r0 + rb, :] = acc.astype(o_ref.dtype)


def _dwconv(u2, nb, dw, dw_b, tm, tc=512, rb=32):
    rows = u2.shape[0]
    t_len = rows // nb
    nt = t_len // tm
    nj = D_INNER // tc
    hb = tm // CONV_HALO
    nhb = rows // CONV_HALO
    prev_map = lambda b, i, j: (jnp.maximum((b * nt + i) * hb - 1, 0), j)
    next_map = lambda b, i, j: (jnp.minimum((b * nt + i + 1) * hb, nhb - 1), j)
    return pl.pallas_call(
        functools.partial(_dwconv_kernel, tm, rb, nt),
        grid=(nb, nt, nj),
        in_specs=[pl.BlockSpec((tm, tc), lambda b, i, j: (b * nt + i, j)),
                  pl.BlockSpec((CONV_HALO, tc), prev_map),
                  pl.BlockSpec((CONV_HALO, tc), next_map),
                  pl.BlockSpec((CONV_WIDTH, tc), lambda b, i, j: (0, j)),
                  pl.BlockSpec((1, tc), lambda b, i, j: (0, j))],
        out_specs=pl.BlockSpec((tm, tc), lambda b, i, j: (b * nt + i, j)),
        out_shape=jax.ShapeDtypeStruct((rows, D_INNER), BF16),
        scratch_shapes=[pltpu.VMEM((tm + 2 * CONV_HALO, tc), F32)],
        name="dwconv",
        compiler_params=_cparams(("parallel", "parallel", "parallel")),
    )(u2, u2, u2, dw, dw_b)


def _stack_heads(q, lane_lo):
    zero = jnp.zeros_like(q)
    return jnp.concatenate([jnp.where(lane_lo, q, zero), jnp.where(lane_lo, zero, q)], axis=0)


def _na_kernel(n_rows, q_ref, k_ref, v_ref, kc_ref, vc_ref, bias_ref, o_ref):
    kr = min(NA_ROWS, n_rows)
    nloc = kr * GRID_W
    lane_lo = lax.broadcasted_iota(jnp.int32, (GRID_W, LANES), 1) < HEAD_DIM
    kc = kc_ref[...]
    vc = vc_ref[...]

    def rows(it, carry):
        rr = [it * NA_UNROLL + u for u in range(NA_UNROLL)]
        start = [jnp.clip(r - kr // 2, 0, n_rows - kr) for r in rr]
        q0 = [pl.multiple_of(r * GRID_W, GRID_W) for r in rr]
        k0 = [pl.multiple_of(s * GRID_W, GRID_W) for s in start]
        qm = [_stack_heads(q_ref[pl.ds(q, GRID_W), :], lane_lo) for q in q0]
        s_loc = [_dot_t(a, k_ref[pl.ds(k, nloc), :]) for a, k in zip(qm, k0)]
        s_ctx_all = _dot_t(jnp.concatenate(qm, axis=0), kc)
        s_ctx = [s_ctx_all[u * LANES:(u + 1) * LANES] for u in range(NA_UNROLL)]
        s_loc = [s + bias_ref[r - st, 0] for s, r, st in zip(s_loc, rr, start)]
        m = [jnp.maximum(jnp.max(a, axis=-1, keepdims=True), jnp.max(b, axis=-1, keepdims=True))
             for a, b in zip(s_loc, s_ctx)]
        p_loc = [jnp.exp(a - mm) for a, mm in zip(s_loc, m)]
        p_ctx = [jnp.exp(a - mm) for a, mm in zip(s_ctx, m)]
        den = [jnp.sum(a, axis=-1, keepdims=True) + jnp.sum(b, axis=-1, keepdims=True)
               for a, b in zip(p_loc, p_ctx)]
        o_ctx = _dot(jnp.concatenate([b.astype(BF16) for b in p_ctx], axis=0), vc)
        o2 = [_dot(a.astype(BF16), v_ref[pl.ds(k, nloc), :]) + o_ctx[u * LANES:(u + 1) * LANES]
              for u, (a, k) in enumerate(zip(p_loc, k0))]
        for o, d, q in zip(o2, den, q0):
            o = o / d
            o_ref[pl.ds(q, GRID_W), :] = jnp.where(lane_lo, o[:GRID_W], o[GRID_W:]).astype(o_ref.dtype)
        return carry

    lax.fori_loop(0, n_rows // NA_UNROLL, rows, 0)


def _na_attention(q, k, v, kc, vc, bias8, nb):
    rows = q.shape[0]
    t_len = rows // nb
    n_rows = t_len // GRID_W
    ctx_len = kc.shape[0] // nb
    nhp = D_INNER // LANES
    kr = min(NA_ROWS, n_rows)
    big = pl.BlockSpec((t_len, LANES), lambda b, h: (b, h))
    cspec = pl.BlockSpec((ctx_len, LANES), lambda b, h: (b, h))
    return pl.pallas_call(
        functools.partial(_na_kernel, n_rows),
        grid=(nb, nhp),
        in_specs=[big, big, big, cspec, cspec,
                  pl.BlockSpec((NA_ROWS, 1, LANES, kr * GRID_W), lambda b, h: (0, h, 0, 0))],
        out_specs=big,
        out_shape=jax.ShapeDtypeStruct((rows, D_INNER), BF16),
        name="na_attn",
        compiler_params=_cparams(("parallel", "parallel")),
    )(q, k, v, kc, vc, bias8)


def _ctx_attn_kernel(q_ref, k_ref, v_ref, o_ref):
    n = q_ref.shape[0]
    lane_lo = lax.broadcasted_iota(jnp.int32, (n, LANES), 1) < HEAD_DIM
    qm = _stack_heads(q_ref[...], lane_lo)
    s = _dot_t(qm, k_ref[...])
    m = jnp.max(s, axis=-1, keepdims=True)
    p = jnp.exp(s - m)
    den = jnp.sum(p, axis=-1, keepdims=True)
    o2 = _dot(p.astype(BF16), v_ref[...]) / den
    o_ref[...] = jnp.where(lane_lo, o2[:n], o2[n:]).astype(o_ref.dtype)


def _ctx_attention(qc, kc, vc, nb):
    ctx_len = qc.shape[0] // nb
    spec = pl.BlockSpec((ctx_len, LANES), lambda b, h: (b, h))
    return pl.pallas_call(
        _ctx_attn_kernel,
        grid=(nb, D_INNER // LANES),
        in_specs=[spec, spec, spec],
        out_specs=spec,
        out_shape=jax.ShapeDtypeStruct(qc.shape, BF16),
        name="ctx_attn",
        compiler_params=_cparams(("parallel", "parallel")),
    )(qc, kc, vc)


def _na_bias_table(rpb, n_rows):
    kr = min(NA_ROWS, n_rows)
    cols = np.arange(GRID_W)
    cstart = np.clip(cols - NA_COLS // 2, 0, GRID_W - NA_COLS)
    jc = np.arange(GRID_W)
    inwin = (jc[None, :] >= cstart[:, None]) & (jc[None, :] < cstart[:, None] + NA_COLS)
    pad = jnp.pad(rpb, ((0, 0), (0, 0), (GRID_W, GRID_W)))
    off = GRID_W + NA_COLS - 1
    toep = jnp.stack([pad[:, :, off - c:off - c + GRID_W] for c in range(GRID_W)], axis=2)
    toep = jnp.where(inwin[None, None], toep, NEG_BIG)
    tab = jnp.stack([toep[:, NA_ROWS - 1 - d:NA_ROWS - 1 - d + kr] for d in range(NA_ROWS)])
    tab = tab.transpose(0, 1, 3, 2, 4)
    return tab.reshape(NA_ROWS, N_HEADS // 2, 2 * GRID_W, kr * GRID_W).astype(F32)


def _rwkv_prep_kernel(tm, nt, x_ref, xp_ref, xn_ref, g_ref, s_ref, sh_ref, mu_ref,
                      wr_ref, wk_ref, wv_ref, wg_ref, w1_ref, w2_ref, w0_ref,
                      a1_ref, a2_ref, a0_ref, kk_ref, ka_ref, rk_ref, bd_ref,
                      r_o, v_o, kkn_o, bon_o, sg_o, lw_o, kd_o, bb_o,
                      xs_ref, hw_ref, ha_ref):
    i = pl.program_id(1)

    @pl.when(pl.program_id(2) == 0)
    def _():
        g, s1p, sh = g_ref[...], s_ref[...], sh_ref[...]
        h = _modnorm(x_ref[...], g, s1p, sh)
        hp = _modnorm(xp_ref[...], g, s1p, sh)[7:8] * (i > 0).astype(F32)
        hn = _modnorm(xn_ref[...], g, s1p, sh)[0:1] * (i < nt - 1).astype(F32)
        rows = lax.broadcasted_iota(jnp.int32, (tm, 1), 0)
        prev = jnp.where(rows == 0, hp, pltpu.roll(h, 1, axis=0))
        nxt = jnp.where(rows == tm - 1, hn, pltpu.roll(h, tm - 1, axis=0))
        ts = 0.5 * (prev + nxt) - h
        for s in range(6):
            xs_ref[s] = (h + ts * mu_ref[s:s + 1, :]).astype(BF16)
        for z in range(2):
            hw_ref[z] = jnp.tanh(_dot(xs_ref[4], w1_ref[z])).astype(BF16)
            ha_ref[z] = _dot(xs_ref[5], a1_ref[z]).astype(BF16)

    bd = bd_ref[...]
    rb = min(tm, PIPE_ROWS)

    def project(q):
        sl = slice(q * rb, (q + 1) * rb)
        main = [_dot(xs_ref[s, sl, :], w[...]) for s, w in enumerate((wr_ref, wk_ref, wv_ref, wg_ref))]
        lora = [(_dot(hw_ref[z, sl, :], w2_ref[z]), _dot(ha_ref[z, sl, :], a2_ref[z])) for z in range(2)]
        return main, lora

    def finish(q, projected):
        sl = slice(q * rb, (q + 1) * rb)
        (r, k, v, g), lora = projected
        kkh = k * kk_ref[...]
        kk = kkh * lax.rsqrt(_head_sum(kkh * kkh, bd) + 1e-12)
        ksum = jnp.zeros_like(k)
        for z in range(2):
            w_raw = w0_ref[z] + lora[z][0]
            lw_o[z, sl, :] = -math.exp(-0.5) * _sigmoid(w_raw)
            a = _sigmoid(a0_ref[z] + lora[z][1])
            kd = k * (1.0 + (a - 1.0) * ka_ref[...])
            kd_o[z, sl, :] = kd.astype(BF16)
            bb_o[z, sl, :] = (kk * a).astype(BF16)
            ksum = ksum + kd
        r_o[sl, :] = r.astype(BF16)
        v_o[sl, :] = v.astype(BF16)
        kkn_o[sl, :] = kk.astype(BF16)
        bon_o[sl, :] = (_head_sum(r * ksum * rk_ref[...], bd) * v).astype(BF16)
        sg_o[sl, :] = _silu(g).astype(BF16)

    _software_pipeline(tm // rb, project, finish)


def _rwkv_prep(x2, nb, g_pre, s1p, sh, p, tm):
    rows = x2.shape[0]
    nt = rows // nb // tm
    tn = GROUP_W
    hb = tm // 8
    nhb = rows // 8
    c3 = lambda b, i, j: (0, 0)
    vec = lambda b, i, j: (b, 0, 0)
    colv = pl.BlockSpec((1, tn), lambda b, i, j: (0, j))
    in_specs = [
        pl.BlockSpec((tm, D_MODEL), lambda b, i, j: (b * nt + i, 0)),
        pl.BlockSpec((8, D_MODEL), lambda b, i, j: (jnp.maximum((b * nt + i) * hb - 1, 0), 0)),
        pl.BlockSpec((8, D_MODEL), lambda b, i, j: (jnp.minimum((b * nt + i + 1) * hb, nhb - 1), 0)),
        pl.BlockSpec((1, D_MODEL), c3),
        pl.BlockSpec((None, 1, D_MODEL), vec),
        pl.BlockSpec((None, 1, D_MODEL), vec),
        pl.BlockSpec((6, D_MODEL), c3),
    ]
    args = [x2, x2, x2, g_pre, s1p, sh, p['mu']]
    for s in range(4):
        in_specs.append(pl.BlockSpec((None, D_MODEL, tn), functools.partial(lambda s, b, i, j: (s, 0, j), s)))
        args.append(p['w_in'])
    lora_in = pl.BlockSpec((2, D_MODEL, LORA), lambda b, i, j: (0, 0, 0))
    lora_out = pl.BlockSpec((2, LORA, tn), lambda b, i, j: (0, 0, j))
    lora_b = pl.BlockSpec((2, 1, tn), lambda b, i, j: (0, 0, j))
    in_specs += [lora_in, lora_out, lora_b, lora_in, lora_out, lora_b, colv, colv, colv,
                 pl.BlockSpec((LANES, LANES), c3)]
    args += [p['w1'], p['w2'], p['w0'], p['a1'], p['a2'], p['a0'], p['k_k'], p['k_a'], p['r_k'], p['bd']]
    gm = pl.BlockSpec((None, tm, tn), lambda b, i, j: (j, b * nt + i, 0))
    gm2 = pl.BlockSpec((2, None, tm, tn), lambda b, i, j: (0, j, b * nt + i, 0))
    sh1 = jax.ShapeDtypeStruct((N_GROUPS, rows, tn), BF16)
    sh2 = jax.ShapeDtypeStruct((2, N_GROUPS, rows, tn), BF16)
    return pl.pallas_call(
        functools.partial(_rwkv_prep_kernel, tm, nt),
        grid=(nb, nt, N_GROUPS),
        in_specs=in_specs,
        out_specs=[gm, gm, gm, gm, gm, gm2, gm2, gm2],
        out_shape=[sh1, sh1, sh1, sh1, sh1, jax.ShapeDtypeStruct((2, N_GROUPS, rows, tn), F32), sh2, sh2],
        scratch_shapes=[pltpu.VMEM((6, tm, D_MODEL), BF16),
                        pltpu.VMEM((2, tm, LORA), BF16),
                        pltpu.VMEM((2, tm, LORA), BF16)],
        name="rwkv_prep",
        compiler_params=_cparams(("parallel", "parallel", "arbitrary")),
    )(*args)


def _block_diag(x_bf, bdm):
    return jnp.concatenate([x_bf] * HEADS_PER_PAIR, axis=0) * bdm


def _scan_kernel(r_ref, v_ref, kk_ref, lw_ref, kd_ref, bb_ref, sin_ref,
                 tri_ref, mk_ref, eye_ref, bdm_ref, y_ref, s_ref):
    @pl.when(pl.program_id(1) == 0)
    def _():
        s_ref[...] = sin_ref[...]

    nb = s_ref.shape[0]
    pairs = N_GROUPS * GROUP_W // LANES
    tri = tri_ref[...]
    bdm = bdm_ref[...]
    bdm_f = bdm.astype(F32)

    def each(fn, *lists):
        return [fn(*args) for args in zip(*lists)]

    def bdiag(x):
        return _block_diag(x, bdm)

    def mm(a_list, b_list):
        return each(lambda a, b: _dot(a.astype(BF16), bdiag(b.astype(BF16))), a_list, b_list)

    grp = range(nb * pairs)

    def at_chain(g):
        b, p = divmod(g, pairs)
        return (p // 2, b, slice(None), slice((p % 2) * LANES, (p % 2 + 1) * LANES))

    def ld(ref, g):
        return ref[at_chain(g)]

    def st_at(g):
        grp_i, b, rows_i, lanes_i = at_chain(g)
        return (b, grp_i, rows_i, lanes_i)

    ms, mi = mk_ref[0], mk_ref[1]
    lw = [ld(lw_ref, g) for g in grp]
    cs = each(lambda x: sum(_dot(tri, p) for p in _split_bf16(x)), lw)
    tot = each(lambda x: jnp.sum(x, axis=0, keepdims=True), lw)
    at = [(-ld(kk_ref, g).astype(F32) * jnp.exp(cs[g] - lw[g])).astype(BF16) for g in grp]
    rt = [(ld(r_ref, g).astype(F32) * jnp.exp(cs[g])).astype(BF16) for g in grp]
    lhs = each(lambda a, b: jnp.concatenate([a, b], axis=0), at, rt)
    def bdiag_t(x):
        return (jnp.concatenate([x] * HEADS_PER_PAIR, axis=0) * bdm_f).T.astype(BF16)

    a1 = [_dot(lhs[g], bdiag_t(ld(bb_ref, g).astype(F32) * jnp.exp(-cs[g]))) for g in grp]
    a2 = [_dot(lhs[g], bdiag_t(ld(kd_ref, g).astype(F32) * jnp.exp(-cs[g]))) for g in grp]
    n = each(lambda a: a[:CHUNK] * ms, a1)
    a_rb = each(lambda a: (a[CHUNK:] * mi).astype(BF16), a1)
    a_k = each(lambda a: jnp.concatenate([a[:CHUNK] * ms, a[CHUNK:] * mi], axis=0).astype(BF16), a2)
    st = [s_ref[st_at(g)] for g in grp]
    from_s = [_dot(lhs[g], bdiag_t(st[g])) for g in grp]
    from_v = [_dot(a_k[g], bdiag(ld(v_ref, g))) for g in grp]

    base = mk_ref[2]
    pw = each(lambda x: x * base, n)
    t_inv = each(lambda x: eye_ref[...] + x, pw)
    for _ in range(2):
        pw = mm(pw, pw)
        t_inv = each(jnp.add, t_inv, mm(t_inv, pw))
    for lvl in range(3):
        pair = mk_ref[3 + lvl]
        xm = mm(each(lambda x: x * pair, n), t_inv)
        t_inv = each(jnp.add, t_inv, mm(t_inv, xm))

    rhs_u = [from_s[g][:CHUNK] + from_v[g][:CHUNK] for g in grp]
    u = mm(t_inv, rhs_u)
    y_rb = mm(a_rb, u)
    for g in grp:
        y = from_s[g][CHUNK:] + y_rb[g] + from_v[g][CHUNK:]
        y_ref[at_chain(g)] = y.astype(y_ref.dtype)

    for g in grp:
        e_rem = jnp.exp(tot[g] - cs[g])
        uv_t = jnp.concatenate([u[g], ld(v_ref, g).astype(F32)], axis=0).T.astype(BF16)
        bk = jnp.concatenate([ld(bb_ref, g).astype(F32) * e_rem, ld(kd_ref, g).astype(F32) * e_rem],
                             axis=0).astype(BF16)
        full = _dot(uv_t, bk) * bdm_f
        upd = full[0:CHUNK]
        for hh in range(1, HEADS_PER_PAIR):
            upd = upd + full[hh * CHUNK:(hh + 1) * CHUNK]
        s_ref[st_at(g)] = st[g] * jnp.exp(tot[g]) + upd


def _scan_consts():
    t = np.arange(CHUNK)
    tt, jj = t[:, None], t[None, :]
    lower = jj <= tt
    strict = jj < tt
    base = strict & (tt // 8 == jj // 8)
    pairs = [(tt // s == jj // s + 1) & (tt // (2 * s) == jj // (2 * s)) for s in (8, 16, 32)]
    fwd = np.stack([strict, lower, base] + pairs)
    both = np.stack([fwd, fwd.transpose(0, 2, 1)]).astype(np.float32)
    masks = np.tile(both, (1, 1, 1, HEADS_PER_PAIR))
    tri = np.stack([lower, lower.T]).astype(np.float32)
    eye = np.tile(np.eye(CHUNK, dtype=np.float32), (1, HEADS_PER_PAIR))
    blk = np.arange(LANES) // HEAD_DIM
    bdm = (blk[:, None] == blk[None, :]).astype(np.float32)
    return jnp.asarray(tri, BF16), jnp.asarray(masks, F32), jnp.asarray(eye, F32), jnp.asarray(bdm, BF16)


def _rwkv_scan(r, v, kk, lw, kd, bb, s_in, nb):
    rows = r.shape[1]
    t_len = rows // nb
    nc = t_len // CHUNK
    tri, masks, eye, bdm = _scan_consts()
    split = lambda a: a.reshape(a.shape[:-2] + (nb, t_len, GROUP_W))
    blk = lambda d, c: c + d * (nc - 1 - 2 * c)
    shared = pl.BlockSpec((N_GROUPS, nb, CHUNK, GROUP_W), lambda d, c: (0, 0, blk(d, c), 0))
    perdir = pl.BlockSpec((None, N_GROUPS, nb, CHUNK, GROUP_W), lambda d, c: (d, 0, 0, blk(d, c), 0))
    sspec = pl.BlockSpec((None, nb, N_GROUPS, HEAD_DIM, GROUP_W), lambda d, c: (d, 0, 0, 0, 0))
    y, s_out = pl.pallas_call(
        _scan_kernel,
        grid=(2, nc),
        in_specs=[shared, shared, shared, perdir, perdir, perdir, sspec,
                  pl.BlockSpec((None, CHUNK, CHUNK), lambda d, c: (d, 0, 0)),
                  pl.BlockSpec((None, 6, CHUNK, LANES), lambda d, c: (d, 0, 0, 0)),
                  pl.BlockSpec((CHUNK, LANES), lambda d, c: (0, 0)),
                  pl.BlockSpec((LANES, LANES), lambda d, c: (0, 0))],
        out_specs=[perdir, sspec],
        out_shape=[jax.ShapeDtypeStruct((2, N_GROUPS, nb, t_len, GROUP_W), BF16),
                   jax.ShapeDtypeStruct(s_in.shape, F32)],
        name="rwkv_scan",
        compiler_params=_cparams(("parallel", "arbitrary")),
    )(split(r), split(v), split(kk), split(lw), split(kd), split(bb), s_in, tri, masks, eye, bdm)
    return y.reshape(2, N_GROUPS, rows, GROUP_W), s_out


def _rwkv_out_kernel(y_ref, bon_ref, sg_ref, lnw_ref, lnb_ref, bd_ref,
                     x_ref, gate_ref, gpost_ref, w_ref, o_ref):
    bd = bd_ref[...]
    acc = jnp.zeros(o_ref.shape, F32)
    for g in range(N_GROUPS):
        yy = y_ref[0, g].astype(F32) + y_ref[1, g].astype(F32)
        mu = _head_sum(yy, bd) * (1.0 / HEAD_DIM)
        d = yy - mu
        var = _head_sum(d * d, bd) * (1.0 / HEAD_DIM)
        yn = d * lax.rsqrt(var + GN_EPS) * lnw_ref[g] + lnb_ref[g]
        o = (yn + bon_ref[g].astype(F32)) * sg_ref[g].astype(F32)
        acc = acc + _dot(o.astype(BF16), w_ref[g])
    o_ref[...] = _post(x_ref[...], acc, gate_ref[...], gpost_ref[...])


def _rwkv_out(y, bon, sg, p, x2, nb, gate, g_post, tm):
    rows = x2.shape[0]
    nt = rows // nb // tm
    row_map = lambda b, i: (b * nt + i, 0)
    gm = pl.BlockSpec((N_GROUPS, tm, GROUP_W), lambda b, i: (0, b * nt + i, 0))
    pv = pl.BlockSpec((N_GROUPS, 1, GROUP_W), lambda b, i: (0, 0, 0))
    return pl.pallas_call(
        _rwkv_out_kernel,
        grid=(nb, nt),
        in_specs=[pl.BlockSpec((2, N_GROUPS, tm, GROUP_W), lambda b, i: (0, 0, b * nt + i, 0)),
                  gm, gm, pv, pv,
                  pl.BlockSpec((LANES, LANES), lambda b, i: (0, 0)),
                  pl.BlockSpec((tm, D_MODEL), row_map),
                  pl.BlockSpec((None, 1, D_MODEL), lambda b, i: (b, 0, 0)),
                  pl.BlockSpec((1, D_MODEL), lambda b, i: (0, 0)),
                  pl.BlockSpec((N_GROUPS, GROUP_W, D_MODEL), lambda b, i: (0, 0, 0))],
        out_specs=pl.BlockSpec((tm, D_MODEL), row_map),
        out_shape=jax.ShapeDtypeStruct((rows, D_MODEL), F32),
        name="rwkv_out",
        compiler_params=_cparams(("parallel", "parallel")),
    )(y, bon, sg, p['ln_w'], p['ln_b'], p['bd'], x2, gate, g_post, p['w_out'])


def _tile(t_len, pref):
    return pref if t_len % pref == 0 else t_len


def _rwkv_layer(xs, cs, nb, mods, g_pre, g_post, p, need_ctx):
    t_x, t_c = xs.shape[0] // nb, cs.shape[0] // nb
    (s1p_x, sh_x, gate_x), (s1p_c, sh_c, gate_c) = mods
    zeros = jnp.zeros((2, nb, N_GROUPS, HEAD_DIM, GROUP_W), F32)
    rc, vc, kkc, bonc, sgc, lwc, kdc, bbc = _rwkv_prep(cs, nb, g_pre, s1p_c, sh_c, p, _tile(t_c, 256))
    yc, state_c = _rwkv_scan(rc, vc, kkc, lwc, kdc, bbc, zeros, nb)
    rx, vx, kkx, bonx, sgx, lwx, kdx, bbx = _rwkv_prep(xs, nb, g_pre, s1p_x, sh_x, p, _tile(t_x, 1024))
    yx, _ = _rwkv_scan(rx, vx, kkx, lwx, kdx, bbx, state_c, nb)
    xs_new = _rwkv_out(yx, bonx, sgx, p, xs, nb, gate_x, g_post, _tile(t_x, 512))
    cs_new = _rwkv_out(yc, bonc, sgc, p, cs, nb, gate_c, g_post, _tile(t_c, 256)) if need_ctx else cs
    return xs_new, cs_new


def _conv_layer(xs, cs, nb, mods, g_pre, g_post, p, need_ctx):
    def run(x2, mod, t_len):
        s1p, sh, gate = mod
        u, sgate = _inproj(x2, nb, g_pre, s1p, sh, p['w_in'], p['b_in'], 3, 2, _conv_epi, _tile(t_len, 1024))
        uc = _dwconv(u, nb, p['dw'], p['dw_b'], _tile(t_len, 128))
        vspec = pl.BlockSpec((1, D_INNER), lambda b, i: (0, 0))
        return _outproj([uc, sgate], [(p['ln_w'], vspec), (p['ln_b'], vspec)], _conv_pro,
                        x2, nb, gate, g_post, p['w_out'], p['b_out'], _tile(t_len, 512))
    xs_new = run(xs, mods[0], xs.shape[0] // nb)
    cs_new = run(cs, mods[1], cs.shape[0] // nb) if need_ctx else cs
    return xs_new, cs_new


def _na_layer(xs, cs, nb, mods, g_pre, g_post, p, need_ctx):
    t_x, t_c = xs.shape[0] // nb, cs.shape[0] // nb
    (s1p_x, sh_x, gate_x), (s1p_c, sh_c, gate_c) = mods
    q, k, v, sg = _inproj(xs, nb, g_pre, s1p_x, sh_x, p['w_in'], None, 4, 4, _na_epi, _tile(t_x, 1024))
    qc, kc, vc, sgc = _inproj(cs, nb, g_pre, s1p_c, sh_c, p['w_in'], None, 4, 4, _na_epi, _tile(t_c, 256))
    o = _na_attention(q, k, v, kc, vc, _na_bias_table(p['rpb'], t_x // GRID_W), nb)
    xs_new = _outproj([o, sg], [], _na_pro, xs, nb, gate_x, g_post, p['w_out'], None, _tile(t_x, 512))
    cs_new = cs
    if need_ctx:
        oc = _ctx_attention(qc, kc, vc, nb)
        cs_new = _outproj([oc, sgc], [], _na_pro, cs, nb, gate_c, g_post, p['w_out'], None, _tile(t_c, 256))
    return xs_new, cs_new


def kernel(x, c, ctx, c_ctx, ada_w, ada_b, norm_pre, norm_post, rw_mu, rw_w_in, rw_w0, rw_w1, rw_w2, rw_a0, rw_a1, rw_a2, rw_k_k, rw_k_a, rw_r_k, rw_ln_w, rw_ln_b, rw_w_out, cf_w_in, cf_b_in, cf_dw, cf_dw_b, cf_ln_w, cf_ln_b, cf_w_out, cf_b_out, na_w_in, na_rpb, na_w_out):
    nb, t_len, _ = x.shape
    ctx_len = ctx.shape[1]
    depth = ada_w.shape[0]
    xs = x.reshape(nb * t_len, D_MODEL)
    cs = ctx.reshape(nb * ctx_len, D_MODEL)

    cc = jnp.zeros((8, D_MODEL), F32).at[:nb].set(c).at[nb].set(c_ctx)
    mod = _modulation(cc, ada_w, ada_b)
    bd = jnp.asarray(np.kron(np.eye(LANES // HEAD_DIM), np.ones((HEAD_DIM, HEAD_DIM))), BF16)

    for i in range(depth):
        kind, j = i % 3, i // 3
        need_ctx = i < depth - 1
        shift, scale, gate = (mod[i, :, s * D_MODEL:(s + 1) * D_MODEL] for s in range(3))
        mods = []
        for sel in (slice(0, nb), slice(nb, nb + 1)):
            mods.append(tuple(jnp.broadcast_to(m[sel], (nb, D_MODEL)).reshape(nb, 1, D_MODEL)
                              for m in (1.0 + scale, shift, gate)))
        g_pre = norm_pre[i].reshape(1, D_MODEL)
        g_post = norm_post[i].reshape(1, D_MODEL)
        if kind == 0:
            p = dict(mu=rw_mu[j], w_in=rw_w_in[j].astype(BF16),
                     w0=rw_w0[j].reshape(2, 1, D_INNER), w1=rw_w1[j].astype(BF16), w2=rw_w2[j].astype(BF16),
                     a0=rw_a0[j].reshape(2, 1, D_INNER), a1=rw_a1[j].astype(BF16), a2=rw_a2[j].astype(BF16),
                     k_k=rw_k_k[j].reshape(1, D_INNER), k_a=rw_k_a[j].reshape(1, D_INNER),
                     r_k=rw_r_k[j].reshape(1, D_INNER),
                     ln_w=rw_ln_w[j].reshape(N_GROUPS, 1, GROUP_W), ln_b=rw_ln_b[j].reshape(N_GROUPS, 1, GROUP_W),
                     w_out=rw_w_out[j].astype(BF16).reshape(N_GROUPS, GROUP_W, D_MODEL), bd=bd)
            xs, cs = _rwkv_layer(xs, cs, nb, mods, g_pre, g_post, p, need_ctx)
        elif kind == 1:
            p = dict(w_in=cf_w_in[j].astype(BF16), b_in=cf_b_in[j].reshape(1, 3 * D_INNER),
                     dw=cf_dw[j], dw_b=cf_dw_b[j].reshape(1, D_INNER),
                     ln_w=cf_ln_w[j].reshape(1, D_INNER), ln_b=cf_ln_b[j].reshape(1, D_INNER),
                     w_out=cf_w_out[j].astype(BF16), b_out=cf_b_out[j].reshape(1, D_MODEL))
            xs, cs = _conv_layer(xs, cs, nb, mods, g_pre, g_post, p, need_ctx)
        else:
            p = dict(w_in=na_w_in[j].astype(BF16), rpb=na_rpb[j], w_out=na_w_out[j].astype(BF16))
            xs, cs = _na_layer(xs, cs, nb, mods, g_pre, g_post, p, need_ctx)
    return xs.reshape(nb, t_len, D_MODEL)
```

```python
import functools
import math

import jax
import jax.numpy as jnp
import numpy as np
from jax import lax
from jax.experimental import pallas as pl
from jax.experimental.pallas import tpu as pltpu

F32 = jnp.float32
BF16 = jnp.bfloat16

D_MODEL = 1024
D_INNER = 2048
N_HEADS = 32
HEAD_DIM = 64
LORA = 64
GRID_W = 64
CONV_WIDTH = 31
NA_ROWS = 8
NA_COLS = 16
RMS_EPS = 1e-6
LN_EPS = 1e-5
GN_EPS = 6.4e-4

LANES = 128
GROUP_W = 256
N_GROUPS = D_INNER // GROUP_W
HEADS_PER_PAIR = LANES // HEAD_DIM
CHUNK = 64
CONV_HALO = 16
NA_UNROLL = 8
PIPE_ROWS = 256
NEG_BIG = -1e30
VMEM_LIMIT = 56 * 1024 * 1024


def _cparams(sem):
    return pltpu.CompilerParams(dimension_semantics=sem, vmem_limit_bytes=VMEM_LIMIT)


def _dot(a, b):
    return jnp.dot(a, b, preferred_element_type=F32)


def _dot_t(a, b):
    return lax.dot_general(a, b, (((1,), (1,)), ((), ())), preferred_element_type=F32)


def _split_bf16(x):
    hi = x.astype(BF16)
    lo = (x - hi.astype(F32)).astype(BF16)
    return hi, lo


def _sigmoid(x):
    return 1.0 / (1.0 + jnp.exp(-x))


def _silu(x):
    return x * _sigmoid(x)


def _modnorm(x, g, s1p, sh):
    ms = jnp.mean(x * x, axis=-1, keepdims=True)
    return (x * lax.rsqrt(ms + RMS_EPS) * g) * s1p + sh


def _post(xres, ox, gate, gpost):
    ms = jnp.mean(ox * ox, axis=-1, keepdims=True)
    return xres + gate * (ox * lax.rsqrt(ms + RMS_EPS) * gpost)


def _software_pipeline(nblk, project, finish):
    pending = project(0)
    for q in range(1, nblk):
        ahead = project(q)
        finish(q - 1, pending)
        pending = ahead
    finish(nblk - 1, pending)


def _head_sum(x, bd):
    return _dot(x.astype(BF16), bd)


def _mod_kernel(c_ref, w_ref, b_ref, o_ref):
    a_hi, a_lo = _split_bf16(_silu(c_ref[...]))
    w_hi, w_lo = _split_bf16(w_ref[...])
    o_ref[...] = _dot(a_hi, w_hi) + _dot(a_hi, w_lo) + _dot(a_lo, w_hi) + b_ref[...]


def _modulation(cc, ada_w, ada_b):
    depth = ada_w.shape[0]
    tn = 1024
    return pl.pallas_call(
        _mod_kernel,
        grid=(depth, 3 * D_MODEL // tn),
        in_specs=[pl.BlockSpec((8, D_MODEL), lambda l, j: (0, 0)),
                  pl.BlockSpec((None, D_MODEL, tn), lambda l, j: (l, 0, j)),
                  pl.BlockSpec((None, 1, tn), lambda l, j: (l, 0, j))],
        out_specs=pl.BlockSpec((None, 8, tn), lambda l, j: (l, 0, j)),
        out_shape=jax.ShapeDtypeStruct((depth, 8, 3 * D_MODEL), F32),
        name="adaln_mod",
        compiler_params=_cparams(("parallel", "parallel")),
    )(cc, ada_w, ada_b.reshape(depth, 1, 3 * D_MODEL))


def _inproj_kernel(ns, has_bias, epi, x_ref, g_ref, s_ref, sh_ref, *rest):
    w_refs = rest[:ns]
    rest = rest[ns:]
    b_refs = rest[:ns] if has_bias else None
    rest = rest[ns:] if has_bias else rest
    out_refs, hn_ref = rest[:-1], rest[-1]

    @pl.when(pl.program_id(2) == 0)
    def _():
        hn_ref[...] = _modnorm(x_ref[...], g_ref[...], s_ref[...], sh_ref[...]).astype(BF16)

    tm = hn_ref.shape[0]
    rb = min(tm, PIPE_ROWS)

    def project(q):
        hn = hn_ref[q * rb:(q + 1) * rb, :]
        return [_dot(hn, w_refs[s][...]) for s in range(ns)]

    def finish(q, ys):
        if has_bias:
            ys = [y + b_refs[s][...] for s, y in enumerate(ys)]
        for o_ref, o in zip(out_refs, epi(ys)):
            o_ref[q * rb:(q + 1) * rb, :] = o.astype(o_ref.dtype)

    _software_pipeline(tm // rb, project, finish)


def _inproj(x2, nb, g_pre, s1p, sh, w_bf, bias, ns, n_out, epi, tm, tn=512):
    rows = x2.shape[0]
    nt = rows // nb // tm
    nj = D_INNER // tn
    row_map = lambda b, i, j: (b * nt + i, 0)
    vec_map = lambda b, i, j: (b, 0, 0)
    in_specs = [pl.BlockSpec((tm, D_MODEL), row_map),
                pl.BlockSpec((1, D_MODEL), lambda b, i, j: (0, 0)),
                pl.BlockSpec((None, 1, D_MODEL), vec_map),
                pl.BlockSpec((None, 1, D_MODEL), vec_map)]
    args = [x2, g_pre, s1p, sh]
    for s in range(ns):
        in_specs.append(pl.BlockSpec((D_MODEL, tn), functools.partial(lambda s, b, i, j: (0, s * nj + j), s)))
        args.append(w_bf)
    if bias is not None:
        for s in range(ns):
            in_specs.append(pl.BlockSpec((1, tn), functools.partial(lambda s, b, i, j: (0, s * nj + j), s)))
            args.append(bias)
    out_spec = pl.BlockSpec((tm, tn), lambda b, i, j: (b * nt + i, j))
    return pl.pallas_call(
        functools.partial(_inproj_kernel, ns, bias is not None, epi),
        grid=(nb, nt, nj),
        in_specs=in_specs,
        out_specs=[out_spec] * n_out,
        out_shape=[jax.ShapeDtypeStruct((rows, D_INNER), BF16)] * n_out,
        scratch_shapes=[pltpu.VMEM((tm, D_MODEL), BF16)],
        name="inproj%d" % ns,
        compiler_params=_cparams(("parallel", "parallel", "arbitrary")),
    )(*args)


def _conv_epi(ys):
    u, ug, gt = ys
    return [u * _sigmoid(ug), _silu(gt)]


def _na_epi(ys):
    q, k, v, g = ys
    return [q * (HEAD_DIM ** -0.5), k, v, _silu(g)]


def _outproj_kernel(pro, has_bias, nin, *refs):
    in_refs = refs[:nin]
    x_ref, gate_ref, gpost_ref, w_ref = refs[nin:nin + 4]
    b_ref = refs[nin + 4] if has_bias else None
    o_ref = refs[-1]
    tm = o_ref.shape[0]
    rb = min(tm, PIPE_ROWS)

    def project(q):
        return _dot(pro(slice(q * rb, (q + 1) * rb), *in_refs), w_ref[...])

    def finish(q, ox):
        sl = slice(q * rb, (q + 1) * rb)
        if has_bias:
            ox = ox + b_ref[...]
        o_ref[sl, :] = _post(x_ref[sl, :], ox, gate_ref[...], gpost_ref[...])

    _software_pipeline(tm // rb, project, finish)


def _outproj(ins, in_specs_extra, pro, x2, nb, gate, g_post, w_bf, bias, tm):
    rows = x2.shape[0]
    nt = rows // nb // tm
    row_map = lambda b, i: (b * nt + i, 0)
    in_specs = [pl.BlockSpec((tm, D_INNER), row_map) for _ in ins]
    args = list(ins)
    for arr, spec in in_specs_extra:
        in_specs.append(spec)
        args.append(arr)
    nin = len(args)
    in_specs += [pl.BlockSpec((tm, D_MODEL), row_map),
                 pl.BlockSpec((None, 1, D_MODEL), lambda b, i: (b, 0, 0)),
                 pl.BlockSpec((1, D_MODEL), lambda b, i: (0, 0)),
                 pl.BlockSpec((D_INNER, D_MODEL), lambda b, i: (0, 0))]
    args += [x2, gate, g_post, w_bf]
    if bias is not None:
        in_specs.append(pl.BlockSpec((1, D_MODEL), lambda b, i: (0, 0)))
        args.append(bias)
    return pl.pallas_call(
        functools.partial(_outproj_kernel, pro, bias is not None, nin),
        grid=(nb, nt),
        in_specs=in_specs,
        out_specs=pl.BlockSpec((tm, D_MODEL), row_map),
        out_shape=jax.ShapeDtypeStruct((rows, D_MODEL), F32),
        name="outproj_" + pro.__name__.strip("_"),
        compiler_params=_cparams(("parallel", "parallel")),
    )(*args)


def _na_pro(sl, o_ref, sg_ref):
    return o_ref[sl, :] * sg_ref[sl, :]


def _conv_pro(sl, u_ref, sg_ref, lnw_ref, lnb_ref):
    u = u_ref[sl, :].astype(F32)
    mu = jnp.mean(u, axis=-1, keepdims=True)
    d = u - mu
    var = jnp.mean(d * d, axis=-1, keepdims=True)
    y = d * lax.rsqrt(var + LN_EPS) * lnw_ref[...] + lnb_ref[...]
    return (_silu(y) * sg_ref[sl, :].astype(F32)).astype(BF16)


def _dwconv_kernel(tm, rb, nt, u_ref, up_ref, un_ref, dw_ref, db_ref, o_ref, buf_ref):
    i = pl.program_id(1)
    keep_prev = (i > 0).astype(F32)
    keep_next = (i < nt - 1).astype(F32)
    buf_ref[0:CONV_HALO, :] = up_ref[...].astype(F32) * keep_prev
    buf_ref[CONV_HALO:CONV_HALO + tm, :] = u_ref[...].astype(F32)
    buf_ref[CONV_HALO + tm:2 * CONV_HALO + tm, :] = un_ref[...].astype(F32) * keep_next
    dw = dw_ref[...]
    off = CONV_HALO - CONV_WIDTH // 2
    span = rb + 2 * CONV_HALO
    for r0 in range(0, tm, rb):
        acc = jnp.zeros((rb, u_ref.shape[1]), F32) + db_ref[...]
        win = buf_ref[r0:r0 + span, :]
        for s in range(8):
            shifted = win if s == 0 else pltpu.roll(win, span - s, axis=0)
            for a in range(span // 8):
                k = 8 * a + s - off
                if 0 <= k < CONV_WIDTH:
                    acc = acc + shifted[8 * a:8 * a + rb, :] * dw[k:k + 1, :]
        o_ref[r0:r0 + rb, :] = acc.astype(o_ref.dtype)


def _dwconv(u2, nb, dw, dw_b, tm, tc=512, rb=32):
    rows = u2.shape[0]
    t_len = rows // nb
    nt = t_len // tm
    nj = D_INNER // tc
    hb = tm // CONV_HALO
    nhb = rows // CONV_HALO
    prev_map = lambda b, i, j: (jnp.maximum((b * nt + i) * hb - 1, 0), j)
    next_map = lambda b, i, j: (jnp.minimum((b * nt + i + 1) * hb, nhb - 1), j)
    return pl.pallas_call(
        functools.partial(_dwconv_kernel, tm, rb, nt),
        grid=(nb, nt, nj),
        in_specs=[pl.BlockSpec((tm, tc), lambda b, i, j: (b * nt + i, j)),
                  pl.BlockSpec((CONV_HALO, tc), prev_map),
                  pl.BlockSpec((CONV_HALO, tc), next_map),
                  pl.BlockSpec((CONV_WIDTH, tc), lambda b, i, j: (0, j)),
                  pl.BlockSpec((1, tc), lambda b, i, j: (0, j))],
        out_specs=pl.BlockSpec((tm, tc), lambda b, i, j: (b * nt + i, j)),
        out_shape=jax.ShapeDtypeStruct((rows, D_INNER), BF16),
        scratch_shapes=[pltpu.VMEM((tm + 2 * CONV_HALO, tc), F32)],
        name="dwconv",
        compiler_params=_cparams(("parallel", "parallel", "parallel")),
    )(u2, u2, u2, dw, dw_b)


def _stack_heads(q, lane_lo):
    zero = jnp.zeros_like(q)
    return jnp.concatenate([jnp.where(lane_lo, q, zero), jnp.where(lane_lo, zero, q)], axis=0)


def _na_kernel(n_rows, q_ref, k_ref, v_ref, kc_ref, vc_ref, bias_ref, o_ref):
    kr = min(NA_ROWS, n_rows)
    nloc = kr * GRID_W
    lane_lo = lax.broadcasted_iota(jnp.int32, (GRID_W, LANES), 1) < HEAD_DIM
    kc = kc_ref[...]
    vc = vc_ref[...]

    def rows(it, carry):
        rr = [it * NA_UNROLL + u for u in range(NA_UNROLL)]
        start = [jnp.clip(r - kr // 2, 0, n_rows - kr) for r in rr]
        q0 = [pl.multiple_of(r * GRID_W, GRID_W) for r in rr]
        k0 = [pl.multiple_of(s * GRID_W, GRID_W) for s in start]
        qm = [_stack_heads(q_ref[pl.ds(q, GRID_W), :], lane_lo) for q in q0]
        s_loc = [_dot_t(a, k_ref[pl.ds(k, nloc), :]) for a, k in zip(qm, k0)]
        s_ctx_all = _dot_t(jnp.concatenate(qm, axis=0), kc)
        s_ctx = [s_ctx_all[u * LANES:(u + 1) * LANES] for u in range(NA_UNROLL)]
        s_loc = [s + bias_ref[r - st, 0] for s, r, st in zip(s_loc, rr, start)]
        m = [jnp.maximum(jnp.max(a, axis=-1, keepdims=True), jnp.max(b, axis=-1, keepdims=True))
             for a, b in zip(s_loc, s_ctx)]
        p_loc = [jnp.exp(a - mm) for a, mm in zip(s_loc, m)]
        p_ctx = [jnp.exp(a - mm) for a, mm in zip(s_ctx, m)]
        den = [jnp.sum(a, axis=-1, keepdims=True) + jnp.sum(b, axis=-1, keepdims=True)
               for a, b in zip(p_loc, p_ctx)]
        o_ctx = _dot(jnp.concatenate([b.astype(BF16) for b in p_ctx], axis=0), vc)
        o2 = [_dot(a.astype(BF16), v_ref[pl.ds(k, nloc), :]) + o_ctx[u * LANES:(u + 1) * LANES]
              for u, (a, k) in enumerate(zip(p_loc, k0))]
        for o, d, q in zip(o2, den, q0):
            o = o / d
            o_ref[pl.ds(q, GRID_W), :] = jnp.where(lane_lo, o[:GRID_W], o[GRID_W:]).astype(o_ref.dtype)
        return carry

    lax.fori_loop(0, n_rows // NA_UNROLL, rows, 0)


def _na_attention(q, k, v, kc, vc, bias8, nb):
    rows = q.shape[0]
    t_len = rows // nb
    n_rows = t_len // GRID_W
    ctx_len = kc.shape[0] // nb
    nhp = D_INNER // LANES
    kr = min(NA_ROWS, n_rows)
    big = pl.BlockSpec((t_len, LANES), lambda b, h: (b, h))
    cspec = pl.BlockSpec((ctx_len, LANES), lambda b, h: (b, h))
    return pl.pallas_call(
        functools.partial(_na_kernel, n_rows),
        grid=(nb, nhp),
        in_specs=[big, big, big, cspec, cspec,
                  pl.BlockSpec((NA_ROWS, 1, LANES, kr * GRID_W), lambda b, h: (0, h, 0, 0))],
        out_specs=big,
        out_shape=jax.ShapeDtypeStruct((rows, D_INNER), BF16),
        name="na_attn",
        compiler_params=_cparams(("parallel", "parallel")),
    )(q, k, v, kc, vc, bias8)


def _ctx_attn_kernel(q_ref, k_ref, v_ref, o_ref):
    n = q_ref.shape[0]
    lane_lo = lax.broadcasted_iota(jnp.int32, (n, LANES), 1) < HEAD_DIM
    qm = _stack_heads(q_ref[...], lane_lo)
    s = _dot_t(qm, k_ref[...])
    m = jnp.max(s, axis=-1, keepdims=True)
    p = jnp.exp(s - m)
    den = jnp.sum(p, axis=-1, keepdims=True)
    o2 = _dot(p.astype(BF16), v_ref[...]) / den
    o_ref[...] = jnp.where(lane_lo, o2[:n], o2[n:]).astype(o_ref.dtype)


def _ctx_attention(qc, kc, vc, nb):
    ctx_len = qc.shape[0] // nb
    spec = pl.BlockSpec((ctx_len, LANES), lambda b, h: (b, h))
    return pl.pallas_call(
        _ctx_attn_kernel,
        grid=(nb, D_INNER // LANES),
        in_specs=[spec, spec, spec],
        out_specs=spec,
        out_shape=jax.ShapeDtypeStruct(qc.shape, BF16),
        name="ctx_attn",
        compiler_params=_cparams(("parallel", "parallel")),
    )(qc, kc, vc)


def _na_bias_table(rpb, n_rows):
    kr = min(NA_ROWS, n_rows)
    cols = np.arange(GRID_W)
    cstart = np.clip(cols - NA_COLS // 2, 0, GRID_W - NA_COLS)
    jc = np.arange(GRID_W)
    inwin = (jc[None, :] >= cstart[:, None]) & (jc[None, :] < cstart[:, None] + NA_COLS)
    pad = jnp.pad(rpb, ((0, 0), (0, 0), (GRID_W, GRID_W)))
    off = GRID_W + NA_COLS - 1
    toep = jnp.stack([pad[:, :, off - c:off - c + GRID_W] for c in range(GRID_W)], axis=1)
    toep = jnp.where(inwin[None, :, None, :], toep, NEG_BIG)
    tab = jnp.stack([toep[:, :, NA_ROWS - 1 - d:NA_ROWS - 1 - d + kr] for d in range(NA_ROWS)])
    return tab.reshape(NA_ROWS, N_HEADS // 2, 2 * GRID_W, kr * GRID_W).astype(F32)


def _rwkv_prep_kernel(tm, nt, x_ref, xp_ref, xn_ref, g_ref, s_ref, sh_ref, mu_ref,
                      wr_ref, wk_ref, wv_ref, wg_ref, w1_ref, w2_ref, w0_ref,
                      a1_ref, a2_ref, a0_ref, kk_ref, ka_ref, rk_ref, bd_ref,
                      r_o, v_o, kkn_o, bon_o, sg_o, lw_o, kd_o, bb_o,
                      xs_ref, hw_ref, ha_ref):
    i = pl.program_id(1)

    @pl.when(pl.program_id(2) == 0)
    def _():
        g, s1p, sh = g_ref[...], s_ref[...], sh_ref[...]
        h = _modnorm(x_ref[...], g, s1p, sh)
        hp = _modnorm(xp_ref[...], g, s1p, sh)[7:8] * (i > 0).astype(F32)
        hn = _modnorm(xn_ref[...], g, s1p, sh)[0:1] * (i < nt - 1).astype(F32)
        rows = lax.broadcasted_iota(jnp.int32, (tm, 1), 0)
        prev = jnp.where(rows == 0, hp, pltpu.roll(h, 1, axis=0))
        nxt = jnp.where(rows == tm - 1, hn, pltpu.roll(h, tm - 1, axis=0))
        ts = 0.5 * (prev + nxt) - h
        for s in range(6):
            xs_ref[s] = (h + ts * mu_ref[s:s + 1, :]).astype(BF16)
        for z in range(2):
            hw_ref[z] = jnp.tanh(_dot(xs_ref[4], w1_ref[z])).astype(BF16)
            ha_ref[z] = _dot(xs_ref[5], a1_ref[z]).astype(BF16)

    bd = bd_ref[...]
    rb = min(tm, PIPE_ROWS)

    def project(q):
        sl = slice(q * rb, (q + 1) * rb)
        main = [_dot(xs_ref[s, sl, :], w[...]) for s, w in enumerate((wr_ref, wk_ref, wv_ref, wg_ref))]
        lora = [(_dot(hw_ref[z, sl, :], w2_ref[z]), _dot(ha_ref[z, sl, :], a2_ref[z])) for z in range(2)]
        return main, lora

    def finish(q, projected):
        sl = slice(q * rb, (q + 1) * rb)
        (r, k, v, g), lora = projected
        kkh = k * kk_ref[...]
        kk = kkh * lax.rsqrt(_head_sum(kkh * kkh, bd) + 1e-12)
        ksum = jnp.zeros_like(k)
        k_scaled = k * ka_ref[...]
        k_rest = k - k_scaled
        for z in range(2):
            w_raw = w0_ref[z] + lora[z][0]
            lw_o[z, sl, :] = -math.exp(-0.5) * _sigmoid(w_raw)
            a = _sigmoid(a0_ref[z] + lora[z][1])
            kd = k_rest + k_scaled * a
            kd_o[z, sl, :] = kd.astype(BF16)
            bb_o[z, sl, :] = (kk * a).astype(BF16)
            ksum = ksum + kd
        r_o[sl, :] = r.astype(BF16)
        v_o[sl, :] = v.astype(BF16)
        kkn_o[sl, :] = kk.astype(BF16)
        bon_o[sl, :] = (_head_sum(r * ksum * rk_ref[...], bd) * v).astype(BF16)
        sg_o[sl, :] = _silu(g).astype(BF16)

    _software_pipeline(tm // rb, project, finish)


def _rwkv_prep(x2, nb, g_pre, s1p, sh, p, tm):
    rows = x2.shape[0]
    nt = rows // nb // tm
    tn = GROUP_W
    hb = tm // 8
    nhb = rows // 8
    c3 = lambda b, i, j: (0, 0)
    vec = lambda b, i, j: (b, 0, 0)
    colv = pl.BlockSpec((1, tn), lambda b, i, j: (0, j))
    in_specs = [
        pl.BlockSpec((tm, D_MODEL), lambda b, i, j: (b * nt + i, 0)),
        pl.BlockSpec((8, D_MODEL), lambda b, i, j: (jnp.maximum((b * nt + i) * hb - 1, 0), 0)),
        pl.BlockSpec((8, D_MODEL), lambda b, i, j: (jnp.minimum((b * nt + i + 1) * hb, nhb - 1), 0)),
        pl.BlockSpec((1, D_MODEL), c3),
        pl.BlockSpec((None, 1, D_MODEL), vec),
        pl.BlockSpec((None, 1, D_MODEL), vec),
        pl.BlockSpec((6, D_MODEL), c3),
    ]
    args = [x2, x2, x2, g_pre, s1p, sh, p['mu']]
    for s in range(4):
        in_specs.append(pl.BlockSpec((None, D_MODEL, tn), functools.partial(lambda s, b, i, j: (s, 0, j), s)))
        args.append(p['w_in'])
    lora_in = pl.BlockSpec((2, D_MODEL, LORA), lambda b, i, j: (0, 0, 0))
    lora_out = pl.BlockSpec((2, LORA, tn), lambda b, i, j: (0, 0, j))
    lora_b = pl.BlockSpec((2, 1, tn), lambda b, i, j: (0, 0, j))
    in_specs += [lora_in, lora_out, lora_b, lora_in, lora_out, lora_b, colv, colv, colv,
                 pl.BlockSpec((GROUP_W, GROUP_W), c3)]
    args += [p['w1'], p['w2'], p['w0'], p['a1'], p['a2'], p['a0'], p['k_k'], p['k_a'], p['r_k'], p['bd']]
    gm = pl.BlockSpec((None, tm, tn), lambda b, i, j: (j, b * nt + i, 0))
    gm2 = pl.BlockSpec((2, None, tm, tn), lambda b, i, j: (0, j, b * nt + i, 0))
    sh1 = jax.ShapeDtypeStruct((N_GROUPS, rows, tn), BF16)
    sh2 = jax.ShapeDtypeStruct((2, N_GROUPS, rows, tn), BF16)
    return pl.pallas_call(
        functools.partial(_rwkv_prep_kernel, tm, nt),
        grid=(nb, nt, N_GROUPS),
        in_specs=in_specs,
        out_specs=[gm, gm, gm, gm, gm, gm2, gm2, gm2],
        out_shape=[sh1, sh1, sh1, sh1, sh1, jax.ShapeDtypeStruct((2, N_GROUPS, rows, tn), F32), sh2, sh2],
        scratch_shapes=[pltpu.VMEM((6, tm, D_MODEL), BF16),
                        pltpu.VMEM((2, tm, LORA), BF16),
                        pltpu.VMEM((2, tm, LORA), BF16)],
        name="rwkv_prep",
        compiler_params=_cparams(("parallel", "parallel", "arbitrary")),
    )(*args)


def _block_diag(x_bf, bdm):
    return jnp.concatenate([x_bf] * HEADS_PER_PAIR, axis=0) * bdm


def _scan_kernel(r_ref, v_ref, kk_ref, lw_ref, kd_ref, bb_ref, sin_ref,
                 tri_ref, mk_ref, eye_ref, bdm_ref, y_ref, s_ref):
    @pl.when(pl.program_id(1) == 0)
    def _():
        s_ref[...] = sin_ref[...]

    nb = s_ref.shape[0]
    pairs = N_GROUPS * GROUP_W // LANES
    tri = tri_ref[...]
    bdm = bdm_ref[...]
    bdm_f = bdm.astype(F32)

    def each(fn, *lists):
        return [fn(*args) for args in zip(*lists)]

    def bdiag(x):
        return _block_diag(x, bdm)

    def mm(a_list, b_list):
        return each(lambda a, b: _dot(a.astype(BF16), bdiag(b.astype(BF16))), a_list, b_list)

    grp = range(nb * pairs)

    def at_chain(g):
        b, p = divmod(g, pairs)
        return (p // 2, b, slice(None), slice((p % 2) * LANES, (p % 2 + 1) * LANES))

    def ld(ref, g):
        return ref[at_chain(g)]

    def st_at(g):
        grp_i, b, rows_i, lanes_i = at_chain(g)
        return (b, grp_i, rows_i, lanes_i)

    ms, mi = mk_ref[0], mk_ref[1]
    lw = [ld(lw_ref, g) for g in grp]
    cs = each(lambda x: sum(_dot(tri, p) for p in _split_bf16(x)), lw)
    tot = each(lambda x: jnp.sum(x, axis=0, keepdims=True), lw)
    at = [(-ld(kk_ref, g).astype(F32) * jnp.exp(cs[g] - lw[g])).astype(BF16) for g in grp]
    rt = [(ld(r_ref, g).astype(F32) * jnp.exp(cs[g])).astype(BF16) for g in grp]
    lhs = each(lambda a, b: jnp.concatenate([a, b], axis=0), at, rt)
    def bdiag_t(x):
        return (jnp.concatenate([x] * HEADS_PER_PAIR, axis=0) * bdm_f).T.astype(BF16)

    a1 = [_dot(lhs[g], bdiag_t(ld(bb_ref, g).astype(F32) * jnp.exp(-cs[g]))) for g in grp]
    a2 = [_dot(lhs[g], bdiag_t(ld(kd_ref, g).astype(F32) * jnp.exp(-cs[g]))) for g in grp]
    n = each(lambda a: a[:CHUNK] * ms, a1)
    a_rb = each(lambda a: (a[CHUNK:] * mi).astype(BF16), a1)
    a_k = each(lambda a: jnp.concatenate([a[:CHUNK] * ms, a[CHUNK:] * mi], axis=0).astype(BF16), a2)
    st = [s_ref[st_at(g)] for g in grp]
    from_s = [_dot(lhs[g], bdiag_t(st[g])) for g in grp]
    from_v = [_dot(a_k[g], bdiag(ld(v_ref, g))) for g in grp]

    base = mk_ref[2]
    pw = each(lambda x: x * base, n)
    t_inv = each(lambda x: eye_ref[...] + x, pw)
    for _ in range(2):
        pw = mm(pw, pw)
        t_inv = each(jnp.add, t_inv, mm(t_inv, pw))
    for lvl in range(3):
        pair = mk_ref[3 + lvl]
        xm = mm(each(lambda x: x * pair, n), t_inv)
        t_inv = each(jnp.add, t_inv, mm(t_inv, xm))

    rhs_u = [from_s[g][:CHUNK] + from_v[g][:CHUNK] for g in grp]
    u = mm(t_inv, rhs_u)
    y_rb = mm(a_rb, u)
    for g in grp:
        y = from_s[g][CHUNK:] + y_rb[g] + from_v[g][CHUNK:]
        y_ref[at_chain(g)] = y.astype(y_ref.dtype)

    for g in grp:
        e_rem = jnp.exp(tot[g] - cs[g])
        uv_t = jnp.concatenate([u[g], ld(v_ref, g).astype(F32)], axis=0).T.astype(BF16)
        bk = jnp.concatenate([ld(bb_ref, g).astype(F32) * e_rem, ld(kd_ref, g).astype(F32) * e_rem],
                             axis=0).astype(BF16)
        full = _dot(uv_t, bk) * bdm_f
        upd = full[0:CHUNK]
        for hh in range(1, HEADS_PER_PAIR):
            upd = upd + full[hh * CHUNK:(hh + 1) * CHUNK]
        s_ref[st_at(g)] = st[g] * jnp.exp(tot[g]) + upd


def _scan_consts():
    t = np.arange(CHUNK)
    tt, jj = t[:, None], t[None, :]
    lower = jj <= tt
    strict = jj < tt
    base = strict & (tt // 8 == jj // 8)
    pairs = [(tt // s == jj // s + 1) & (tt // (2 * s) == jj // (2 * s)) for s in (8, 16, 32)]
    fwd = np.stack([strict, lower, base] + pairs)
    both = np.stack([fwd, fwd.transpose(0, 2, 1)]).astype(np.float32)
    masks = np.tile(both, (1, 1, 1, HEADS_PER_PAIR))
    tri = np.stack([lower, lower.T]).astype(np.float32)
    eye = np.tile(np.eye(CHUNK, dtype=np.float32), (1, HEADS_PER_PAIR))
    blk = np.arange(LANES) // HEAD_DIM
    bdm = (blk[:, None] == blk[None, :]).astype(np.float32)
    return jnp.asarray(tri, BF16), jnp.asarray(masks, F32), jnp.asarray(eye, F32), jnp.asarray(bdm, BF16)


def _rwkv_scan(r, v, kk, lw, kd, bb, s_in, nb):
    rows = r.shape[1]
    t_len = rows // nb
    nc = t_len // CHUNK
    tri, masks, eye, bdm = _scan_consts()
    split = lambda a: a.reshape(a.shape[:-2] + (nb, t_len, GROUP_W))
    blk = lambda d, c: c + d * (nc - 1 - 2 * c)
    shared = pl.BlockSpec((N_GROUPS, nb, CHUNK, GROUP_W), lambda d, c: (0, 0, blk(d, c), 0))
    perdir = pl.BlockSpec((None, N_GROUPS, nb, CHUNK, GROUP_W), lambda d, c: (d, 0, 0, blk(d, c), 0))
    sspec = pl.BlockSpec((None, nb, N_GROUPS, HEAD_DIM, GROUP_W), lambda d, c: (d, 0, 0, 0, 0))
    y, s_out = pl.pallas_call(
        _scan_kernel,
        grid=(2, nc),
        in_specs=[shared, shared, shared, perdir, perdir, perdir, sspec,
                  pl.BlockSpec((None, CHUNK, CHUNK), lambda d, c: (d, 0, 0)),
                  pl.BlockSpec((None, 6, CHUNK, LANES), lambda d, c: (d, 0, 0, 0)),
                  pl.BlockSpec((CHUNK, LANES), lambda d, c: (0, 0)),
                  pl.BlockSpec((LANES, LANES), lambda d, c: (0, 0))],
        out_specs=[perdir, sspec],
        out_shape=[jax.ShapeDtypeStruct((2, N_GROUPS, nb, t_len, GROUP_W), BF16),
                   jax.ShapeDtypeStruct(s_in.shape, F32)],
        name="rwkv_scan",
        compiler_params=_cparams(("parallel", "arbitrary")),
    )(split(r), split(v), split(kk), split(lw), split(kd), split(bb), s_in, tri, masks, eye, bdm)
    return y.reshape(2, N_GROUPS, rows, GROUP_W), s_out


def _rwkv_out_kernel(y_ref, bon_ref, sg_ref, lnw_ref, lnb_ref, bd_ref,
                     x_ref, gate_ref, gpost_ref, w_ref, o_ref):
    bd = bd_ref[...] * (1.0 / HEAD_DIM)
    grp = range(N_GROUPS)
    yy = [y_ref[0, g].astype(F32) + y_ref[1, g].astype(F32) for g in grp]
    d = [yy[g] - _head_sum(yy[g], bd) for g in grp]
    var = [_head_sum(d[g] * d[g], bd) for g in grp]
    acc = jnp.zeros(o_ref.shape, F32)
    for g in grp:
        yn = d[g] * lax.rsqrt(var[g] + GN_EPS) * lnw_ref[g] + lnb_ref[g]
        o = (yn + bon_ref[g].astype(F32)) * sg_ref[g].astype(F32)
        acc = acc + _dot(o.astype(BF16), w_ref[g])
    o_ref[...] = _post(x_ref[...], acc, gate_ref[...], gpost_ref[...])


def _rwkv_out(y, bon, sg, p, x2, nb, gate, g_post, tm):
    rows = x2.shape[0]
    nt = rows // nb // tm
    row_map = lambda b, i: (b * nt + i, 0)
    gm = pl.BlockSpec((N_GROUPS, tm, GROUP_W), lambda b, i: (0, b * nt + i, 0))
    pv = pl.BlockSpec((N_GROUPS, 1, GROUP_W), lambda b, i: (0, 0, 0))
    return pl.pallas_call(
        _rwkv_out_kernel,
        grid=(nb, nt),
        in_specs=[pl.BlockSpec((2, N_GROUPS, tm, GROUP_W), lambda b, i: (0, 0, b * nt + i, 0)),
                  gm, gm, pv, pv,
                  pl.BlockSpec((GROUP_W, GROUP_W), lambda b, i: (0, 0)),
                  pl.BlockSpec((tm, D_MODEL), row_map),
                  pl.BlockSpec((None, 1, D_MODEL), lambda b, i: (b, 0, 0)),
                  pl.BlockSpec((1, D_MODEL), lambda b, i: (0, 0)),
                  pl.BlockSpec((N_GROUPS, GROUP_W, D_MODEL), lambda b, i: (0, 0, 0))],
        out_specs=pl.BlockSpec((tm, D_MODEL), row_map),
        out_shape=jax.ShapeDtypeStruct((rows, D_MODEL), F32),
        name="rwkv_out",
        compiler_params=_cparams(("parallel", "parallel")),
    )(y, bon, sg, p['ln_w'], p['ln_b'], p['bd'], x2, gate, g_post, p['w_out'])


def _tile(t_len, pref):
    return pref if t_len % pref == 0 else t_len


def _rwkv_layer(xs, cs, nb, mods, g_pre, g_post, p, need_ctx):
    t_x, t_c = xs.shape[0] // nb, cs.shape[0] // nb
    (s1p_x, sh_x, gate_x), (s1p_c, sh_c, gate_c) = mods
    zeros = jnp.zeros((2, nb, N_GROUPS, HEAD_DIM, GROUP_W), F32)
    rc, vc, kkc, bonc, sgc, lwc, kdc, bbc = _rwkv_prep(cs, nb, g_pre, s1p_c, sh_c, p, _tile(t_c, 256))
    yc, state_c = _rwkv_scan(rc, vc, kkc, lwc, kdc, bbc, zeros, nb)
    rx, vx, kkx, bonx, sgx, lwx, kdx, bbx = _rwkv_prep(xs, nb, g_pre, s1p_x, sh_x, p, _tile(t_x, 1024))
    yx, _ = _rwkv_scan(rx, vx, kkx, lwx, kdx, bbx, state_c, nb)
    xs_new = _rwkv_out(yx, bonx, sgx, p, xs, nb, gate_x, g_post, _tile(t_x, 512))
    cs_new = _rwkv_out(yc, bonc, sgc, p, cs, nb, gate_c, g_post, _tile(t_c, 256)) if need_ctx else cs
    return xs_new, cs_new


def _conv_layer(xs, cs, nb, mods, g_pre, g_post, p, need_ctx):
    def run(x2, mod, t_len):
        s1p, sh, gate = mod
        u, sgate = _inproj(x2, nb, g_pre, s1p, sh, p['w_in'], p['b_in'], 3, 2, _conv_epi, _tile(t_len, 1024))
        uc = _dwconv(u, nb, p['dw'], p['dw_b'], _tile(t_len, 128))
        vspec = pl.BlockSpec((1, D_INNER), lambda b, i: (0, 0))
        return _outproj([uc, sgate], [(p['ln_w'], vspec), (p['ln_b'], vspec)], _conv_pro,
                        x2, nb, gate, g_post, p['w_out'], p['b_out'], _tile(t_len, 512))
    xs_new = run(xs, mods[0], xs.shape[0] // nb)
    cs_new = run(cs, mods[1], cs.shape[0] // nb) if need_ctx else cs
    return xs_new, cs_new


def _na_layer(xs, cs, nb, mods, g_pre, g_post, p, need_ctx):
    t_x, t_c = xs.shape[0] // nb, cs.shape[0] // nb
    (s1p_x, sh_x, gate_x), (s1p_c, sh_c, gate_c) = mods
    q, k, v, sg = _inproj(xs, nb, g_pre, s1p_x, sh_x, p['w_in'], None, 4, 4, _na_epi, _tile(t_x, 1024))
    qc, kc, vc, sgc = _inproj(cs, nb, g_pre, s1p_c, sh_c, p['w_in'], None, 4, 4, _na_epi, _tile(t_c, 256))
    o = _na_attention(q, k, v, kc, vc, _na_bias_table(p['rpb'], t_x // GRID_W), nb)
    xs_new = _outproj([o, sg], [], _na_pro, xs, nb, gate_x, g_post, p['w_out'], None, _tile(t_x, 512))
    cs_new = cs
    if need_ctx:
        oc = _ctx_attention(qc, kc, vc, nb)
        cs_new = _outproj([oc, sgc], [], _na_pro, cs, nb, gate_c, g_post, p['w_out'], None, _tile(t_c, 256))
    return xs_new, cs_new


def kernel(x, c, ctx, c_ctx, ada_w, ada_b, norm_pre, norm_post, rw_mu, rw_w_in, rw_w0, rw_w1, rw_w2, rw_a0, rw_a1, rw_a2, rw_k_k, rw_k_a, rw_r_k, rw_ln_w, rw_ln_b, rw_w_out, cf_w_in, cf_b_in, cf_dw, cf_dw_b, cf_ln_w, cf_ln_b, cf_w_out, cf_b_out, na_w_in, na_rpb, na_w_out):
    nb, t_len, _ = x.shape
    ctx_len = ctx.shape[1]
    depth = ada_w.shape[0]
    xs = x.reshape(nb * t_len, D_MODEL)
    cs = ctx.reshape(nb * ctx_len, D_MODEL)

    cc = jnp.zeros((8, D_MODEL), F32).at[:nb].set(c).at[nb].set(c_ctx)
    mod = _modulation(cc, ada_w, ada_b)
    bd = jnp.asarray(np.kron(np.eye(GROUP_W // HEAD_DIM), np.ones((HEAD_DIM, HEAD_DIM))), BF16)

    for i in range(depth):
        kind, j = i % 3, i // 3
        need_ctx = i < depth - 1
        shift, scale, gate = (mod[i, :, s * D_MODEL:(s + 1) * D_MODEL] for s in range(3))
        mods = []
        for sel in (slice(0, nb), slice(nb, nb + 1)):
            mods.append(tuple(jnp.broadcast_to(m[sel], (nb, D_MODEL)).reshape(nb, 1, D_MODEL)
                              for m in (1.0 + scale, shift, gate)))
        g_pre = norm_pre[i].reshape(1, D_MODEL)
        g_post = norm_post[i].reshape(1, D_MODEL)
        if kind == 0:
            p = dict(mu=rw_mu[j], w_in=rw_w_in[j].astype(BF16),
                     w0=rw_w0[j].reshape(2, 1, D_INNER), w1=rw_w1[j].astype(BF16), w2=rw_w2[j].astype(BF16),
                     a0=rw_a0[j].reshape(2, 1, D_INNER), a1=rw_a1[j].astype(BF16), a2=rw_a2[j].astype(BF16),
                     k_k=rw_k_k[j].reshape(1, D_INNER), k_a=rw_k_a[j].reshape(1, D_INNER),
                     r_k=rw_r_k[j].reshape(1, D_INNER),
                     ln_w=rw_ln_w[j].reshape(N_GROUPS, 1, GROUP_W), ln_b=rw_ln_b[j].reshape(N_GROUPS, 1, GROUP_W),
                     w_out=rw_w_out[j].astype(BF16).reshape(N_GROUPS, GROUP_W, D_MODEL), bd=bd)
            xs, cs = _rwkv_layer(xs, cs, nb, mods, g_pre, g_post, p, need_ctx)
        elif kind == 1:
            p = dict(w_in=cf_w_in[j].astype(BF16), b_in=cf_b_in[j].reshape(1, 3 * D_INNER),
                     dw=cf_dw[j], dw_b=cf_dw_b[j].reshape(1, D_INNER),
                     ln_w=cf_ln_w[j].reshape(1, D_INNER), ln_b=cf_ln_b[j].reshape(1, D_INNER),
                     w_out=cf_w_out[j].astype(BF16), b_out=cf_b_out[j].reshape(1, D_MODEL))
            xs, cs = _conv_layer(xs, cs, nb, mods, g_pre, g_post, p, need_ctx)
        else:
            p = dict(w_in=na_w_in[j].astype(BF16), rpb=na_rpb[j], w_out=na_w_out[j].astype(BF16))
            xs, cs = _na_layer(xs, cs, nb, mods, g_pre, g_post, p, need_ctx)
    return xs.reshape(nb, t_len, D_MODEL)
```

```python
import functools
import math

import jax
import jax.numpy as jnp
import numpy as np
from jax import lax
from jax.experimental import pallas as pl
from jax.experimental.pallas import tpu as pltpu

F32 = jnp.float32
BF16 = jnp.bfloat16

D_MODEL = 1024
D_INNER = 2048
N_HEADS = 32
HEAD_DIM = 64
LORA = 64
GRID_W = 64
CONV_WIDTH = 31
NA_ROWS = 8
NA_COLS = 16
RMS_EPS = 1e-6
LN_EPS = 1e-5
GN_EPS = 6.4e-4

LANES = 128
GROUP_W = 256
N_GROUPS = D_INNER // GROUP_W
HEADS_PER_PAIR = LANES // HEAD_DIM
CHUNK = 64
CONV_HALO = 16
NA_UNROLL = 8
PIPE_ROWS = 256
NEG_BIG = -1e30
VMEM_LIMIT = 56 * 1024 * 1024


def _cparams(sem):
    return pltpu.CompilerParams(dimension_semantics=sem, vmem_limit_bytes=VMEM_LIMIT)


def _dot(a, b):
    return jnp.dot(a, b, preferred_element_type=F32)


def _dot_t(a, b):
    return lax.dot_general(a, b, (((1,), (1,)), ((), ())), preferred_element_type=F32)


def _split_bf16(x):
    hi = x.astype(BF16)
    lo = (x - hi.astype(F32)).astype(BF16)
    return hi, lo


def _sigmoid(x):
    return 1.0 / (1.0 + jnp.exp(-x))


def _silu(x):
    return x * _sigmoid(x)


def _modnorm(x, g, s1p, sh):
    ms = jnp.mean(x * x, axis=-1, keepdims=True)
    return (x * lax.rsqrt(ms + RMS_EPS) * g) * s1p + sh


def _post(xres, ox, gate, gpost):
    ms = jnp.mean(ox * ox, axis=-1, keepdims=True)
    return xres + gate * (ox * lax.rsqrt(ms + RMS_EPS) * gpost)


def _software_pipeline(nblk, project, finish):
    pending = project(0)
    for q in range(1, nblk):
        ahead = project(q)
        finish(q - 1, pending)
        pending = ahead
    finish(nblk - 1, pending)


def _head_sum(x, bd):
    return _dot(x.astype(BF16), bd)


def _mod_kernel(c_ref, w_ref, b_ref, o_ref):
    a_hi, a_lo = _split_bf16(_silu(c_ref[...]))
    w_hi, w_lo = _split_bf16(w_ref[...])
    o_ref[...] = _dot(a_hi, w_hi) + _dot(a_hi, w_lo) + _dot(a_lo, w_hi) + b_ref[...]


def _modulation(cc, ada_w, ada_b):
    depth = ada_w.shape[0]
    tn = 1024
    return pl.pallas_call(
        _mod_kernel,
        grid=(depth, 3 * D_MODEL // tn),
        in_specs=[pl.BlockSpec((8, D_MODEL), lambda l, j: (0, 0)),
                  pl.BlockSpec((None, D_MODEL, tn), lambda l, j: (l, 0, j)),
                  pl.BlockSpec((None, 1, tn), lambda l, j: (l, 0, j))],
        out_specs=pl.BlockSpec((None, 8, tn), lambda l, j: (l, 0, j)),
        out_shape=jax.ShapeDtypeStruct((depth, 8, 3 * D_MODEL), F32),
        name="adaln_mod",
        compiler_params=_cparams(("parallel", "parallel")),
    )(cc, ada_w, ada_b.reshape(depth, 1, 3 * D_MODEL))


def _inproj_kernel(ns, has_bias, epi, x_ref, g_ref, s_ref, sh_ref, *rest):
    w_refs = rest[:ns]
    rest = rest[ns:]
    b_refs = rest[:ns] if has_bias else None
    rest = rest[ns:] if has_bias else rest
    out_refs, hn_ref = rest[:-1], rest[-1]

    @pl.when(pl.program_id(2) == 0)
    def _():
        hn_ref[...] = _modnorm(x_ref[...], g_ref[...], s_ref[...], sh_ref[...]).astype(BF16)

    tm = hn_ref.shape[0]
    rb = min(tm, PIPE_ROWS)

    def project(q):
        hn = hn_ref[q * rb:(q + 1) * rb, :]
        return [_dot(hn, w_refs[s][...]) for s in range(ns)]

    def finish(q, ys):
        if has_bias:
            ys = [y + b_refs[s][...] for s, y in enumerate(ys)]
        for o_ref, o in zip(out_refs, epi(ys)):
            o_ref[q * rb:(q + 1) * rb, :] = o.astype(o_ref.dtype)

    _software_pipeline(tm // rb, project, finish)


def _inproj(x2, nb, g_pre, s1p, sh, w_bf, bias, ns, n_out, epi, tm, tn=512):
    rows = x2.shape[0]
    nt = rows // nb // tm
    nj = D_INNER // tn
    row_map = lambda b, i, j: (b * nt + i, 0)
    vec_map = lambda b, i, j: (b, 0, 0)
    in_specs = [pl.BlockSpec((tm, D_MODEL), row_map),
                pl.BlockSpec((1, D_MODEL), lambda b, i, j: (0, 0)),
                pl.BlockSpec((None, 1, D_MODEL), vec_map),
                pl.BlockSpec((None, 1, D_MODEL), vec_map)]
    args = [x2, g_pre, s1p, sh]
    for s in range(ns):
        in_specs.append(pl.BlockSpec((D_MODEL, tn), functools.partial(lambda s, b, i, j: (0, s * nj + j), s)))
        args.append(w_bf)
    if bias is not None:
        for s in range(ns):
            in_specs.append(pl.BlockSpec((1, tn), functools.partial(lambda s, b, i, j: (0, s * nj + j), s)))
            args.append(bias)
    out_spec = pl.BlockSpec((tm, tn), lambda b, i, j: (b * nt + i, j))
    return pl.pallas_call(
        functools.partial(_inproj_kernel, ns, bias is not None, epi),
        grid=(nb, nt, nj),
        in_specs=in_specs,
        out_specs=[out_spec] * n_out,
        out_shape=[jax.ShapeDtypeStruct((rows, D_INNER), BF16)] * n_out,
        scratch_shapes=[pltpu.VMEM((tm, D_MODEL), BF16)],
        name="inproj%d" % ns,
        compiler_params=_cparams(("parallel", "parallel", "arbitrary")),
    )(*args)


def _conv_epi(ys):
    u, ug, gt = ys
    return [u * _sigmoid(ug), _silu(gt)]


def _na_epi(ys):
    q, k, v, g = ys
    return [q * (HEAD_DIM ** -0.5), k, v, _silu(g)]


def _outproj_kernel(pro, has_bias, nin, *refs):
    in_refs = refs[:nin]
    x_ref, gate_ref, gpost_ref, w_ref = refs[nin:nin + 4]
    b_ref = refs[nin + 4] if has_bias else None
    o_ref = refs[-1]
    tm = o_ref.shape[0]
    rb = min(tm, PIPE_ROWS)

    def project(q):
        return _dot(pro(slice(q * rb, (q + 1) * rb), *in_refs), w_ref[...])

    def finish(q, ox):
        sl = slice(q * rb, (q + 1) * rb)
        if has_bias:
            ox = ox + b_ref[...]
        o_ref[sl, :] = _post(x_ref[sl, :], ox, gate_ref[...], gpost_ref[...])

    _software_pipeline(tm // rb, project, finish)


def _outproj(ins, in_specs_extra, pro, x2, nb, gate, g_post, w_bf, bias, tm):
    rows = x2.shape[0]
    nt = rows // nb // tm
    row_map = lambda b, i: (b * nt + i, 0)
    in_specs = [pl.BlockSpec((tm, D_INNER), row_map) for _ in ins]
    args = list(ins)
    for arr, spec in in_specs_extra:
        in_specs.append(spec)
        args.append(arr)
    nin = len(args)
    in_specs += [pl.BlockSpec((tm, D_MODEL), row_map),
                 pl.BlockSpec((None, 1, D_MODEL), lambda b, i: (b, 0, 0)),
                 pl.BlockSpec((1, D_MODEL), lambda b, i: (0, 0)),
                 pl.BlockSpec((D_INNER, D_MODEL), lambda b, i: (0, 0))]
    args += [x2, gate, g_post, w_bf]
    if bias is not None:
        in_specs.append(pl.BlockSpec((1, D_MODEL), lambda b, i: (0, 0)))
        args.append(bias)
    return pl.pallas_call(
        functools.partial(_outproj_kernel, pro, bias is not None, nin),
        grid=(nb, nt),
        in_specs=in_specs,
        out_specs=pl.BlockSpec((tm, D_MODEL), row_map),
        out_shape=jax.ShapeDtypeStruct((rows, D_MODEL), F32),
        name="outproj_" + pro.__name__.strip("_"),
        compiler_params=_cparams(("parallel", "parallel")),
    )(*args)


def _na_pro(sl, o_ref, sg_ref):
    return o_ref[sl, :] * sg_ref[sl, :]


def _conv_pro(sl, u_ref, sg_ref, lnw_ref, lnb_ref):
    u = u_ref[sl, :].astype(F32)
    mu = jnp.mean(u, axis=-1, keepdims=True)
    d = u - mu
    var = jnp.mean(d * d, axis=-1, keepdims=True)
    y = d * lax.rsqrt(var + LN_EPS) * lnw_ref[...] + lnb_ref[...]
    return (_silu(y) * sg_ref[sl, :].astype(F32)).astype(BF16)


def _dwconv_kernel(tm, rb, nt, u_ref, up_ref, un_ref, dw_ref, db_ref, o_ref, buf_ref):
    i = pl.program_id(1)
    keep_prev = (i > 0).astype(F32)
    keep_next = (i < nt - 1).astype(F32)
    buf_ref[0:CONV_HALO, :] = up_ref[...].astype(F32) * keep_prev
    buf_ref[CONV_HALO:CONV_HALO + tm, :] = u_ref[...].astype(F32)
    buf_ref[CONV_HALO + tm:2 * CONV_HALO + tm, :] = un_ref[...].astype(F32) * keep_next
    dw = dw_ref[...]
    off = CONV_HALO - CONV_WIDTH // 2
    span = rb + 2 * CONV_HALO
    for r0 in range(0, tm, rb):
        acc = jnp.zeros((rb, u_ref.shape[1]), F32) + db_ref[...]
        win = buf_ref[r0:r0 + span, :]
        for s in range(8):
            shifted = win if s == 0 else pltpu.roll(win, span - s, axis=0)
            for a in range(span // 8):
                k = 8 * a + s - off
                if 0 <= k < CONV_WIDTH:
                    acc = acc + shifted[8 * a:8 * a + rb, :] * dw[k:k + 1, :]
        o_ref[r0:r0 + rb, :] = acc.astype(o_ref.dtype)


def _dwconv(u2, nb, dw, dw_b, tm, tc=512, rb=32):
    rows = u2.shape[0]
    t_len = rows // nb
    nt = t_len // tm
    nj = D_INNER // tc
    hb = tm // CONV_HALO
    nhb = rows // CONV_HALO
    prev_map = lambda b, i, j: (jnp.maximum((b * nt + i) * hb - 1, 0), j)
    next_map = lambda b, i, j: (jnp.minimum((b * nt + i + 1) * hb, nhb - 1), j)
    return pl.pallas_call(
        functools.partial(_dwconv_kernel, tm, rb, nt),
        grid=(nb, nt, nj),
        in_specs=[pl.BlockSpec((tm, tc), lambda b, i, j: (b * nt + i, j)),
                  pl.BlockSpec((CONV_HALO, tc), prev_map),
                  pl.BlockSpec((CONV_HALO, tc), next_map),
                  pl.BlockSpec((CONV_WIDTH, tc), lambda b, i, j: (0, j)),
                  pl.BlockSpec((1, tc), lambda b, i, j: (0, j))],
        out_specs=pl.BlockSpec((tm, tc), lambda b, i, j: (b * nt + i, j)),
        out_shape=jax.ShapeDtypeStruct((rows, D_INNER), BF16),
        scratch_shapes=[pltpu.VMEM((tm + 2 * CONV_HALO, tc), F32)],
        name="dwconv",
        compiler_params=_cparams(("parallel", "parallel", "parallel")),
    )(u2, u2, u2, dw, dw_b)


def _stack_heads(q, lane_lo):
    zero = jnp.zeros_like(q)
    return jnp.concatenate([jnp.where(lane_lo, q, zero), jnp.where(lane_lo, zero, q)], axis=0)


def _na_kernel(n_rows, q_ref, k_ref, v_ref, kc_ref, vc_ref, bias_ref, o_ref):
    kr = min(NA_ROWS, n_rows)
    nloc = kr * GRID_W
    lane_lo = lax.broadcasted_iota(jnp.int32, (GRID_W, LANES), 1) < HEAD_DIM
    kc = kc_ref[...]
    vc = vc_ref[...]

    def rows(it, carry):
        rr = [it * NA_UNROLL + u for u in range(NA_UNROLL)]
        start = [jnp.clip(r - kr // 2, 0, n_rows - kr) for r in rr]
        q0 = [pl.multiple_of(r * GRID_W, GRID_W) for r in rr]
        k0 = [pl.multiple_of(s * GRID_W, GRID_W) for s in start]
        qm = [_stack_heads(q_ref[pl.ds(q, GRID_W), :], lane_lo) for q in q0]
        s_loc = [_dot_t(a, k_ref[pl.ds(k, nloc), :]) for a, k in zip(qm, k0)]
        s_ctx_all = _dot_t(jnp.concatenate(qm, axis=0), kc)
        s_ctx = [s_ctx_all[u * LANES:(u + 1) * LANES] for u in range(NA_UNROLL)]
        s_loc = [s + bias_ref[r - st, 0] for s, r, st in zip(s_loc, rr, start)]
        m = [jnp.maximum(jnp.max(a, axis=-1, keepdims=True), jnp.max(b, axis=-1, keepdims=True))
             for a, b in zip(s_loc, s_ctx)]
        p_loc = [jnp.exp(a - mm) for a, mm in zip(s_loc, m)]
        p_ctx = [jnp.exp(a - mm) for a, mm in zip(s_ctx, m)]
        den = [jnp.sum(a, axis=-1, keepdims=True) + jnp.sum(b, axis=-1, keepdims=True)
               for a, b in zip(p_loc, p_ctx)]
        o_ctx = _dot(jnp.concatenate([b.astype(BF16) for b in p_ctx], axis=0), vc)
        o2 = [_dot(a.astype(BF16), v_ref[pl.ds(k, nloc), :]) + o_ctx[u * LANES:(u + 1) * LANES]
              for u, (a, k) in enumerate(zip(p_loc, k0))]
        for o, d, q in zip(o2, den, q0):
            o = o / d
            o_ref[pl.ds(q, GRID_W), :] = jnp.where(lane_lo, o[:GRID_W], o[GRID_W:]).astype(o_ref.dtype)
        return carry

    lax.fori_loop(0, n_rows // NA_UNROLL, rows, 0)


def _na_attention(q, k, v, kc, vc, bias8, nb):
    rows = q.shape[0]
    t_len = rows // nb
    n_rows = t_len // GRID_W
    ctx_len = kc.shape[0] // nb
    nhp = D_INNER // LANES
    kr = min(NA_ROWS, n_rows)
    big = pl.BlockSpec((t_len, LANES), lambda b, h: (b, h))
    cspec = pl.BlockSpec((ctx_len, LANES), lambda b, h: (b, h))
    return pl.pallas_call(
        functools.partial(_na_kernel, n_rows),
        grid=(nb, nhp),
        in_specs=[big, big, big, cspec, cspec,
                  pl.BlockSpec((NA_ROWS, 1, LANES, kr * GRID_W), lambda b, h: (0, h, 0, 0))],
        out_specs=big,
        out_shape=jax.ShapeDtypeStruct((rows, D_INNER), BF16),
        name="na_attn",
        compiler_params=_cparams(("parallel", "parallel")),
    )(q, k, v, kc, vc, bias8)


def _ctx_attn_kernel(q_ref, k_ref, v_ref, o_ref):
    n = q_ref.shape[0]
    lane_lo = lax.broadcasted_iota(jnp.int32, (n, LANES), 1) < HEAD_DIM
    qm = _stack_heads(q_ref[...], lane_lo)
    s = _dot_t(qm, k_ref[...])
    m = jnp.max(s, axis=-1, keepdims=True)
    p = jnp.exp(s - m)
    den = jnp.sum(p, axis=-1, keepdims=True)
    o2 = _dot(p.astype(BF16), v_ref[...]) / den
    o_ref[...] = jnp.where(lane_lo, o2[:n], o2[n:]).astype(o_ref.dtype)


def _ctx_attention(qc, kc, vc, nb):
    ctx_len = qc.shape[0] // nb
    spec = pl.BlockSpec((ctx_len, LANES), lambda b, h: (b, h))
    return pl.pallas_call(
        _ctx_attn_kernel,
        grid=(nb, D_INNER // LANES),
        in_specs=[spec, spec, spec],
        out_specs=spec,
        out_shape=jax.ShapeDtypeStruct(qc.shape, BF16),
        name="ctx_attn",
        compiler_params=_cparams(("parallel", "parallel")),
    )(qc, kc, vc)


def _na_bias_table(rpb, n_rows):
    kr = min(NA_ROWS, n_rows)
    cols = np.arange(GRID_W)
    cstart = np.clip(cols - NA_COLS // 2, 0, GRID_W - NA_COLS)
    jc = np.arange(GRID_W)
    inwin = (jc[None, :] >= cstart[:, None]) & (jc[None, :] < cstart[:, None] + NA_COLS)
    pad = jnp.pad(rpb, ((0, 0), (0, 0), (GRID_W, GRID_W)))
    off = GRID_W + NA_COLS - 1
    toep = jnp.stack([pad[:, :, off - c:off - c + GRID_W] for c in range(GRID_W)], axis=1)
    toep = jnp.where(inwin[None, :, None, :], toep, NEG_BIG)
    tab = jnp.stack([toep[:, :, NA_ROWS - 1 - d:NA_ROWS - 1 - d + kr] for d in range(NA_ROWS)])
    return tab.reshape(NA_ROWS, N_HEADS // 2, 2 * GRID_W, kr * GRID_W).astype(F32)


def _rwkv_prep_kernel(tm, nt, x_ref, xp_ref, xn_ref, g_ref, s_ref, sh_ref, mu_ref,
                      wr_ref, wk_ref, wv_ref, wg_ref, w1_ref, w2_ref, w0_ref,
                      a1_ref, a2_ref, a0_ref, kk_ref, ka_ref, rk_ref, bd_ref,
                      r_o, v_o, kkn_o, bon_o, sg_o, lw_o, kd_o, bb_o,
                      xs_ref, hw_ref, ha_ref):
    i = pl.program_id(1)

    @pl.when(pl.program_id(2) == 0)
    def _():
        g, s1p, sh = g_ref[...], s_ref[...], sh_ref[...]
        h = _modnorm(x_ref[...], g, s1p, sh)
        hp = _modnorm(xp_ref[...], g, s1p, sh)[7:8] * (i > 0).astype(F32)
        hn = _modnorm(xn_ref[...], g, s1p, sh)[0:1] * (i < nt - 1).astype(F32)
        rows = lax.broadcasted_iota(jnp.int32, (tm, 1), 0)
        prev = jnp.where(rows == 0, hp, pltpu.roll(h, 1, axis=0))
        nxt = jnp.where(rows == tm - 1, hn, pltpu.roll(h, tm - 1, axis=0))
        ts = 0.5 * (prev + nxt) - h
        for s in range(6):
            xs_ref[s] = (h + ts * mu_ref[s:s + 1, :]).astype(BF16)
        for z in range(2):
            hw_ref[z] = jnp.tanh(_dot(xs_ref[4], w1_ref[z])).astype(BF16)
            ha_ref[z] = _dot(xs_ref[5], a1_ref[z]).astype(BF16)

    bd = bd_ref[...]
    rb = min(tm, PIPE_ROWS)

    def project(q):
        sl = slice(q * rb, (q + 1) * rb)
        main = [_dot(xs_ref[s, sl, :], w[...]) for s, w in enumerate((wr_ref, wk_ref, wv_ref, wg_ref))]
        lora = [(_dot(hw_ref[z, sl, :], w2_ref[z]), _dot(ha_ref[z, sl, :], a2_ref[z])) for z in range(2)]
        return main, lora

    def finish(q, projected):
        sl = slice(q * rb, (q + 1) * rb)
        (r, k, v, g), lora = projected
        kkh = k * kk_ref[...]
        kk = kkh * lax.rsqrt(_head_sum(kkh * kkh, bd) + 1e-12)
        ksum = jnp.zeros_like(k)
        k_scaled = k * ka_ref[...]
        k_rest = k - k_scaled
        for z in range(2):
            w_raw = w0_ref[z] + lora[z][0]
            lw_o[z, sl, :] = -math.exp(-0.5) * _sigmoid(w_raw)
            a = _sigmoid(a0_ref[z] + lora[z][1])
            kd = k_rest + k_scaled * a
            kd_o[z, sl, :] = kd.astype(BF16)
            bb_o[z, sl, :] = (kk * a).astype(BF16)
            ksum = ksum + kd
        r_o[sl, :] = r.astype(BF16)
        v_o[sl, :] = v.astype(BF16)
        kkn_o[sl, :] = kk.astype(BF16)
        bon_o[sl, :] = (_head_sum(r * ksum * rk_ref[...], bd) * v).astype(BF16)
        sg_o[sl, :] = _silu(g).astype(BF16)

    _software_pipeline(tm // rb, project, finish)


def _rwkv_prep(x2, nb, g_pre, s1p, sh, p, tm):
    rows = x2.shape[0]
    nt = rows // nb // tm
    tn = GROUP_W
    hb = tm // 8
    nhb = rows // 8
    c3 = lambda b, i, j: (0, 0)
    vec = lambda b, i, j: (b, 0, 0)
    colv = pl.BlockSpec((1, tn), lambda b, i, j: (0, j))
    in_specs = [
        pl.BlockSpec((tm, D_MODEL), lambda b, i, j: (b * nt + i, 0)),
        pl.BlockSpec((8, D_MODEL), lambda b, i, j: (jnp.maximum((b * nt + i) * hb - 1, 0), 0)),
        pl.BlockSpec((8, D_MODEL), lambda b, i, j: (jnp.minimum((b * nt + i + 1) * hb, nhb - 1), 0)),
        pl.BlockSpec((1, D_MODEL), c3),
        pl.BlockSpec((None, 1, D_MODEL), vec),
        pl.BlockSpec((None, 1, D_MODEL), vec),
        pl.BlockSpec((6, D_MODEL), c3),
    ]
    args = [x2, x2, x2, g_pre, s1p, sh, p['mu']]
    for s in range(4):
        in_specs.append(pl.BlockSpec((None, D_MODEL, tn), functools.partial(lambda s, b, i, j: (s, 0, j), s)))
        args.append(p['w_in'])
    lora_in = pl.BlockSpec((2, D_MODEL, LORA), lambda b, i, j: (0, 0, 0))
    lora_out = pl.BlockSpec((2, LORA, tn), lambda b, i, j: (0, 0, j))
    lora_b = pl.BlockSpec((2, 1, tn), lambda b, i, j: (0, 0, j))
    in_specs += [lora_in, lora_out, lora_b, lora_in, lora_out, lora_b, colv, colv, colv,
                 pl.BlockSpec((GROUP_W, GROUP_W), c3)]
    args += [p['w1'], p['w2'], p['w0'], p['a1'], p['a2'], p['a0'], p['k_k'], p['k_a'], p['r_k'], p['bd']]
    gm = pl.BlockSpec((None, tm, tn), lambda b, i, j: (j, b * nt + i, 0))
    gm2 = pl.BlockSpec((2, None, tm, tn), lambda b, i, j: (0, j, b * nt + i, 0))
    sh1 = jax.ShapeDtypeStruct((N_GROUPS, rows, tn), BF16)
    sh2 = jax.ShapeDtypeStruct((2, N_GROUPS, rows, tn), BF16)
    return pl.pallas_call(
        functools.partial(_rwkv_prep_kernel, tm, nt),
        grid=(nb, nt, N_GROUPS),
        in_specs=in_specs,
        out_specs=[gm, gm, gm, gm, gm, gm2, gm2, gm2],
        out_shape=[sh1, sh1, sh1, sh1, sh1, jax.ShapeDtypeStruct((2, N_GROUPS, rows, tn), F32), sh2, sh2],
        scratch_shapes=[pltpu.VMEM((6, tm, D_MODEL), BF16),
                        pltpu.VMEM((2, tm, LORA), BF16),
                        pltpu.VMEM((2, tm, LORA), BF16)],
        name="rwkv_prep",
        compiler_params=_cparams(("parallel", "parallel", "arbitrary")),
    )(*args)


def _block_diag(x_bf, bdm):
    return jnp.concatenate([x_bf] * HEADS_PER_PAIR, axis=0) * bdm


def _scan_kernel(r_ref, v_ref, kk_ref, lw_ref, kd_ref, bb_ref, sin_ref,
                 tri_ref, mk_ref, eye_ref, bdm_ref, y_ref, s_ref):
    @pl.when(pl.program_id(1) == 0)
    def _():
        s_ref[...] = sin_ref[...]

    nb = s_ref.shape[0]
    pairs = N_GROUPS * GROUP_W // LANES
    tri = tri_ref[...]
    bdm = bdm_ref[...]
    bdm_f = bdm.astype(F32)

    def each(fn, *lists):
        return [fn(*args) for args in zip(*lists)]

    def bdiag(x):
        return _block_diag(x, bdm)

    def mm(a_list, b_list):
        return each(lambda a, b: _dot(a.astype(BF16), bdiag(b.astype(BF16))), a_list, b_list)

    grp = range(nb * pairs)

    def at_chain(g):
        b, p = divmod(g, pairs)
        return (p // 2, b, slice(None), slice((p % 2) * LANES, (p % 2 + 1) * LANES))

    def ld(ref, g):
        return ref[at_chain(g)]

    def st_at(g):
        grp_i, b, rows_i, lanes_i = at_chain(g)
        return (b, grp_i, rows_i, lanes_i)

    ms, mi = mk_ref[0], mk_ref[1]
    lw = [ld(lw_ref, g) for g in grp]
    cs = each(lambda x: _dot(tri, jnp.concatenate(_split_bf16(x), axis=1)), lw)
    cs = each(lambda x: x[:, :LANES] + x[:, LANES:], cs)
    tot = each(lambda x: jnp.sum(x, axis=0, keepdims=True), lw)
    at = [(-ld(kk_ref, g).astype(F32) * jnp.exp(cs[g] - lw[g])).astype(BF16) for g in grp]
    rt = [(ld(r_ref, g).astype(F32) * jnp.exp(cs[g])).astype(BF16) for g in grp]
    lhs = each(lambda a, b: jnp.concatenate([a, b], axis=0), at, rt)
    def bdiag_t(x):
        return (jnp.concatenate([x] * HEADS_PER_PAIR, axis=0) * bdm_f).T.astype(BF16)

    a12 = [_dot(lhs[g], jnp.concatenate([bdiag_t(ld(bb_ref, g).astype(F32) * jnp.exp(-cs[g])),
                                         bdiag_t(ld(kd_ref, g).astype(F32) * jnp.exp(-cs[g]))], axis=1))
           for g in grp]
    a1 = [a[:, :LANES] for a in a12]
    a2 = [a[:, LANES:] for a in a12]
    n = each(lambda a: a[:CHUNK] * ms, a1)
    a_rb = each(lambda a: (a[CHUNK:] * mi).astype(BF16), a1)
    a_k = each(lambda a: jnp.concatenate([a[:CHUNK] * ms, a[CHUNK:] * mi], axis=0).astype(BF16), a2)
    st = [s_ref[st_at(g)] for g in grp]
    from_s = [_dot(lhs[g], bdiag_t(st[g])) for g in grp]
    from_v = [_dot(a_k[g], bdiag(ld(v_ref, g))) for g in grp]

    base = mk_ref[2]
    pw = each(lambda x: x * base, n)
    t_inv = each(lambda x: eye_ref[...] + x, pw)
    pw = mm(pw, pw)
    both = each(lambda t, p: _dot(jnp.concatenate([t, p], axis=0).astype(BF16), bdiag(p.astype(BF16))),
                t_inv, pw)
    t_inv = each(lambda t, b: t + b[:CHUNK], t_inv, both)
    t_inv = each(jnp.add, t_inv, mm(t_inv, each(lambda b: b[CHUNK:], both)))
    def mm2(a_list, b_list, rhs_list):
        return each(lambda a, b, r: _dot(jnp.concatenate([a, b], axis=0).astype(BF16), bdiag(r.astype(BF16))),
                    a_list, b_list, rhs_list)

    pair8, pair16, pair32 = mk_ref[3], mk_ref[4], mk_ref[5]
    xm = mm(each(lambda x: x * pair8, n), t_inv)
    t_inv = each(jnp.add, t_inv, mm(t_inv, xm))
    xz = mm2(each(lambda x: x * pair16, n), each(lambda x: x * pair32, n), t_inv)
    step = mm2(t_inv, each(lambda b: b[CHUNK:], xz), each(lambda b: b[:CHUNK], xz))
    t_inv = each(lambda t, s: t + s[:CHUNK], t_inv, step)
    x32 = each(lambda z, s: z[CHUNK:] + s[CHUNK:], xz, step)
    t_inv = each(jnp.add, t_inv, mm(t_inv, x32))

    rhs_u = [from_s[g][:CHUNK] + from_v[g][:CHUNK] for g in grp]
    u = mm(t_inv, rhs_u)
    y_rb = mm(a_rb, u)
    for g in grp:
        y = from_s[g][CHUNK:] + y_rb[g] + from_v[g][CHUNK:]
        y_ref[at_chain(g)] = y.astype(y_ref.dtype)

    for g in grp:
        e_rem = jnp.exp(tot[g] - cs[g])
        uv_t = jnp.concatenate([u[g], ld(v_ref, g).astype(F32)], axis=0).T.astype(BF16)
        bk = jnp.concatenate([ld(bb_ref, g).astype(F32) * e_rem, ld(kd_ref, g).astype(F32) * e_rem],
                             axis=0).astype(BF16)
        full = _dot(uv_t, bk) * bdm_f
        upd = full[0:CHUNK]
        for hh in range(1, HEADS_PER_PAIR):
            upd = upd + full[hh * CHUNK:(hh + 1) * CHUNK]
        s_ref[st_at(g)] = st[g] * jnp.exp(tot[g]) + upd


def _scan_consts():
    t = np.arange(CHUNK)
    tt, jj = t[:, None], t[None, :]
    lower = jj <= tt
    strict = jj < tt
    base = strict & (tt // 8 == jj // 8)
    pairs = [(tt // s == jj // s + 1) & (tt // (2 * s) == jj // (2 * s)) for s in (8, 16, 32)]
    fwd = np.stack([strict, lower, base] + pairs)
    both = np.stack([fwd, fwd.transpose(0, 2, 1)]).astype(np.float32)
    masks = np.tile(both, (1, 1, 1, HEADS_PER_PAIR))
    tri = np.stack([lower, lower.T]).astype(np.float32)
    eye = np.tile(np.eye(CHUNK, dtype=np.float32), (1, HEADS_PER_PAIR))
    blk = np.arange(LANES) // HEAD_DIM
    bdm = (blk[:, None] == blk[None, :]).astype(np.float32)
    return jnp.asarray(tri, BF16), jnp.asarray(masks, F32), jnp.asarray(eye, F32), jnp.asarray(bdm, BF16)


def _rwkv_scan(r, v, kk, lw, kd, bb, s_in, nb):
    rows = r.shape[1]
    t_len = rows // nb
    nc = t_len // CHUNK
    tri, masks, eye, bdm = _scan_consts()
    split = lambda a: a.reshape(a.shape[:-2] + (nb, t_len, GROUP_W))
    blk = lambda d, c: c + d * (nc - 1 - 2 * c)
    shared = pl.BlockSpec((N_GROUPS, nb, CHUNK, GROUP_W), lambda d, c: (0, 0, blk(d, c), 0))
    perdir = pl.BlockSpec((None, N_GROUPS, nb, CHUNK, GROUP_W), lambda d, c: (d, 0, 0, blk(d, c), 0))
    sspec = pl.BlockSpec((None, nb, N_GROUPS, HEAD_DIM, GROUP_W), lambda d, c: (d, 0, 0, 0, 0))
    y, s_out = pl.pallas_call(
        _scan_kernel,
        grid=(2, nc),
        in_specs=[shared, shared, shared, perdir, perdir, perdir, sspec,
                  pl.BlockSpec((None, CHUNK, CHUNK), lambda d, c: (d, 0, 0)),
                  pl.BlockSpec((None, 6, CHUNK, LANES), lambda d, c: (d, 0, 0, 0)),
                  pl.BlockSpec((CHUNK, LANES), lambda d, c: (0, 0)),
                  pl.BlockSpec((LANES, LANES), lambda d, c: (0, 0))],
        out_specs=[perdir, sspec],
        out_shape=[jax.ShapeDtypeStruct((2, N_GROUPS, nb, t_len, GROUP_W), BF16),
                   jax.ShapeDtypeStruct(s_in.shape, F32)],
        name="rwkv_scan",
        compiler_params=_cparams(("parallel", "arbitrary")),
    )(split(r), split(v), split(kk), split(lw), split(kd), split(bb), s_in, tri, masks, eye, bdm)
    return y.reshape(2, N_GROUPS, rows, GROUP_W), s_out


def _rwkv_out_kernel(y_ref, bon_ref, sg_ref, lnw_ref, lnb_ref, bd_ref,
                     x_ref, gate_ref, gpost_ref, w_ref, o_ref):
    bd = bd_ref[...] * (1.0 / HEAD_DIM)
    grp = range(N_GROUPS)
    yy = [y_ref[0, g].astype(F32) + y_ref[1, g].astype(F32) for g in grp]
    d = [yy[g] - _head_sum(yy[g], bd) for g in grp]
    var = [_head_sum(d[g] * d[g], bd) for g in grp]
    acc = jnp.zeros(o_ref.shape, F32)
    for g in grp:
        yn = d[g] * lax.rsqrt(var[g] + GN_EPS) * lnw_ref[g] + lnb_ref[g]
        o = (yn + bon_ref[g].astype(F32)) * sg_ref[g].astype(F32)
        acc = acc + _dot(o.astype(BF16), w_ref[g])
    o_ref[...] = _post(x_ref[...], acc, gate_ref[...], gpost_ref[...])


def _rwkv_out(y, bon, sg, p, x2, nb, gate, g_post, tm):
    rows = x2.shape[0]
    nt = rows // nb // tm
    row_map = lambda b, i: (b * nt + i, 0)
    gm = pl.BlockSpec((N_GROUPS, tm, GROUP_W), lambda b, i: (0, b * nt + i, 0))
    pv = pl.BlockSpec((N_GROUPS, 1, GROUP_W), lambda b, i: (0, 0, 0))
    return pl.pallas_call(
        _rwkv_out_kernel,
        grid=(nb, nt),
        in_specs=[pl.BlockSpec((2, N_GROUPS, tm, GROUP_W), lambda b, i: (0, 0, b * nt + i, 0)),
                  gm, gm, pv, pv,
                  pl.BlockSpec((GROUP_W, GROUP_W), lambda b, i: (0, 0)),
                  pl.BlockSpec((tm, D_MODEL), row_map),
                  pl.BlockSpec((None, 1, D_MODEL), lambda b, i: (b, 0, 0)),
                  pl.BlockSpec((1, D_MODEL), lambda b, i: (0, 0)),
                  pl.BlockSpec((N_GROUPS, GROUP_W, D_MODEL), lambda b, i: (0, 0, 0))],
        out_specs=pl.BlockSpec((tm, D_MODEL), row_map),
        out_shape=jax.ShapeDtypeStruct((rows, D_MODEL), F32),
        name="rwkv_out",
        compiler_params=_cparams(("parallel", "parallel")),
    )(y, bon, sg, p['ln_w'], p['ln_b'], p['bd'], x2, gate, g_post, p['w_out'])


def _tile(t_len, pref):
    return pref if t_len % pref == 0 else t_len


def _rwkv_layer(xs, cs, nb, mods, g_pre, g_post, p, need_ctx):
    t_x, t_c = xs.shape[0] // nb, cs.shape[0] // nb
    (s1p_x, sh_x, gate_x), (s1p_c, sh_c, gate_c) = mods
    zeros = jnp.zeros((2, nb, N_GROUPS, HEAD_DIM, GROUP_W), F32)
    rc, vc, kkc, bonc, sgc, lwc, kdc, bbc = _rwkv_prep(cs, nb, g_pre, s1p_c, sh_c, p, _tile(t_c, 256))
    yc, state_c = _rwkv_scan(rc, vc, kkc, lwc, kdc, bbc, zeros, nb)
    rx, vx, kkx, bonx, sgx, lwx, kdx, bbx = _rwkv_prep(xs, nb, g_pre, s1p_x, sh_x, p, _tile(t_x, 1024))
    yx, _ = _rwkv_scan(rx, vx, kkx, lwx, kdx, bbx, state_c, nb)
    xs_new = _rwkv_out(yx, bonx, sgx, p, xs, nb, gate_x, g_post, _tile(t_x, 512))
    cs_new = _rwkv_out(yc, bonc, sgc, p, cs, nb, gate_c, g_post, _tile(t_c, 256)) if need_ctx else cs
    return xs_new, cs_new


def _conv_layer(xs, cs, nb, mods, g_pre, g_post, p, need_ctx):
    def run(x2, mod, t_len):
        s1p, sh, gate = mod
        u, sgate = _inproj(x2, nb, g_pre, s1p, sh, p['w_in'], p['b_in'], 3, 2, _conv_epi, _tile(t_len, 1024))
        uc = _dwconv(u, nb, p['dw'], p['dw_b'], _tile(t_len, 128))
        vspec = pl.BlockSpec((1, D_INNER), lambda b, i: (0, 0))
        return _outproj([uc, sgate], [(p['ln_w'], vspec), (p['ln_b'], vspec)], _conv_pro,
                        x2, nb, gate, g_post, p['w_out'], p['b_out'], _tile(t_len, 512))
    xs_new = run(xs, mods[0], xs.shape[0] // nb)
    cs_new = run(cs, mods[1], cs.shape[0] // nb) if need_ctx else cs
    return xs_new, cs_new


def _na_layer(xs, cs, nb, mods, g_pre, g_post, p, need_ctx):
    t_x, t_c = xs.shape[0] // nb, cs.shape[0] // nb
    (s1p_x, sh_x, gate_x), (s1p_c, sh_c, gate_c) = mods
    q, k, v, sg = _inproj(xs, nb, g_pre, s1p_x, sh_x, p['w_in'], None, 4, 4, _na_epi, _tile(t_x, 1024))
    qc, kc, vc, sgc = _inproj(cs, nb, g_pre, s1p_c, sh_c, p['w_in'], None, 4, 4, _na_epi, _tile(t_c, 256))
    o = _na_attention(q, k, v, kc, vc, _na_bias_table(p['rpb'], t_x // GRID_W), nb)
    xs_new = _outproj([o, sg], [], _na_pro, xs, nb, gate_x, g_post, p['w_out'], None, _tile(t_x, 512))
    cs_new = cs
    if need_ctx:
        oc = _ctx_attention(qc, kc, vc, nb)
        cs_new = _outproj([oc, sgc], [], _na_pro, cs, nb, gate_c, g_post, p['w_out'], None, _tile(t_c, 256))
    return xs_new, cs_new


def kernel(x, c, ctx, c_ctx, ada_w, ada_b, norm_pre, norm_post, rw_mu, rw_w_in, rw_w0, rw_w1, rw_w2, rw_a0, rw_a1, rw_a2, rw_k_k, rw_k_a, rw_r_k, rw_ln_w, rw_ln_b, rw_w_out, cf_w_in, cf_b_in, cf_dw, cf_dw_b, cf_ln_w, cf_ln_b, cf_w_out, cf_b_out, na_w_in, na_rpb, na_w_out):
    nb, t_len, _ = x.shape
    ctx_len = ctx.shape[1]
    depth = ada_w.shape[0]
    xs = x.reshape(nb * t_len, D_MODEL)
    cs = ctx.reshape(nb * ctx_len, D_MODEL)

    cc = jnp.zeros((8, D_MODEL), F32).at[:nb].set(c).at[nb].set(c_ctx)
    mod = _modulation(cc, ada_w, ada_b)
    bd = jnp.asarray(np.kron(np.eye(GROUP_W // HEAD_DIM), np.ones((HEAD_DIM, HEAD_DIM))), BF16)

    for i in range(depth):
        kind, j = i % 3, i // 3
        need_ctx = i < depth - 1
        shift, scale, gate = (mod[i, :, s * D_MODEL:(s + 1) * D_MODEL] for s in range(3))
        mods = []
        for sel in (slice(0, nb), slice(nb, nb + 1)):
            mods.append(tuple(jnp.broadcast_to(m[sel], (nb, D_MODEL)).reshape(nb, 1, D_MODEL)
                              for m in (1.0 + scale, shift, gate)))
        g_pre = norm_pre[i].reshape(1, D_MODEL)
        g_post = norm_post[i].reshape(1, D_MODEL)
        if kind == 0:
            p = dict(mu=rw_mu[j], w_in=rw_w_in[j].astype(BF16),
                     w0=rw_w0[j].reshape(2, 1, D_INNER), w1=rw_w1[j].astype(BF16), w2=rw_w2[j].astype(BF16),
                     a0=rw_a0[j].reshape(2, 1, D_INNER), a1=rw_a1[j].astype(BF16), a2=rw_a2[j].astype(BF16),
                     k_k=rw_k_k[j].reshape(1, D_INNER), k_a=rw_k_a[j].reshape(1, D_INNER),
                     r_k=rw_r_k[j].reshape(1, D_INNER),
                     ln_w=rw_ln_w[j].reshape(N_GROUPS, 1, GROUP_W), ln_b=rw_ln_b[j].reshape(N_GROUPS, 1, GROUP_W),
                     w_out=rw_w_out[j].astype(BF16).reshape(N_GROUPS, GROUP_W, D_MODEL), bd=bd)
            xs, cs = _rwkv_layer(xs, cs, nb, mods, g_pre, g_post, p, need_ctx)
        elif kind == 1:
            p = dict(w_in=cf_w_in[j].astype(BF16), b_in=cf_b_in[j].reshape(1, 3 * D_INNER),
                     dw=cf_dw[j], dw_b=cf_dw_b[j].reshape(1, D_INNER),
                     ln_w=cf_ln_w[j].reshape(1, D_INNER), ln_b=cf_ln_b[j].reshape(1, D_INNER),
                     w_out=cf_w_out[j].astype(BF16), b_out=cf_b_out[j].reshape(1, D_MODEL))
            xs, cs = _conv_layer(xs, cs, nb, mods, g_pre, g_post, p, need_ctx)
        else:
            p = dict(w_in=na_w_in[j].astype(BF16), rpb=na_rpb[j], w_out=na_w_out[j].astype(BF16))
            xs, cs = _na_layer(xs, cs, nb, mods, g_pre, g_post, p, need_ctx)
    return xs.reshape(nb, t_len, D_MODEL)
```

```python
import functools
import math

import jax
import jax.numpy as jnp
import numpy as np
from jax import lax
from jax.experimental import pallas as pl
from jax.experimental.pallas import tpu as pltpu

F32 = jnp.float32
BF16 = jnp.bfloat16

D_MODEL = 1024
D_INNER = 2048
N_HEADS = 32
HEAD_DIM = 64
LORA = 64
GRID_W = 64
CONV_WIDTH = 31
NA_ROWS = 8
NA_COLS = 16
RMS_EPS = 1e-6
LN_EPS = 1e-5
GN_EPS = 6.4e-4

LANES = 128
GROUP_W = 256
N_GROUPS = D_INNER // GROUP_W
HEADS_PER_PAIR = LANES // HEAD_DIM
CHUNK = 64
CONV_HALO = 16
NA_UNROLL = 8
PIPE_ROWS = 256
NEG_BIG = -1e30
VMEM_LIMIT = 56 * 1024 * 1024


def _cparams(sem):
    return pltpu.CompilerParams(dimension_semantics=sem, vmem_limit_bytes=VMEM_LIMIT)


def _dot(a, b):
    return jnp.dot(a, b, preferred_element_type=F32)


def _dot_t(a, b):
    return lax.dot_general(a, b, (((1,), (1,)), ((), ())), preferred_element_type=F32)


def _split_bf16(x):
    hi = x.astype(BF16)
    lo = (x - hi.astype(F32)).astype(BF16)
    return hi, lo


def _sigmoid(x):
    return 1.0 / (1.0 + jnp.exp(-x))


def _silu(x):
    return x * _sigmoid(x)


def _modnorm(x, g, s1p, sh):
    ms = jnp.mean(x * x, axis=-1, keepdims=True)
    return (x * lax.rsqrt(ms + RMS_EPS) * g) * s1p + sh


def _post(xres, ox, gate, gpost):
    ms = jnp.mean(ox * ox, axis=-1, keepdims=True)
    return xres + gate * (ox * lax.rsqrt(ms + RMS_EPS) * gpost)


def _software_pipeline(nblk, project, finish):
    pending = project(0)
    for q in range(1, nblk):
        ahead = project(q)
        finish(q - 1, pending)
        pending = ahead
    finish(nblk - 1, pending)


def _head_sum(x, bd):
    return _dot(x.astype(BF16), bd)


def _mod_kernel(c_ref, w_ref, b_ref, o_ref):
    a_hi, a_lo = _split_bf16(_silu(c_ref[...]))
    w_hi, w_lo = _split_bf16(w_ref[...])
    o_ref[...] = _dot(a_hi, w_hi) + _dot(a_hi, w_lo) + _dot(a_lo, w_hi) + b_ref[...]


def _modulation(cc, ada_w, ada_b):
    depth = ada_w.shape[0]
    tn = 1024
    return pl.pallas_call(
        _mod_kernel,
        grid=(depth, 3 * D_MODEL // tn),
        in_specs=[pl.BlockSpec((8, D_MODEL), lambda l, j: (0, 0)),
                  pl.BlockSpec((None, D_MODEL, tn), lambda l, j: (l, 0, j)),
                  pl.BlockSpec((None, 1, tn), lambda l, j: (l, 0, j))],
        out_specs=pl.BlockSpec((None, 8, tn), lambda l, j: (l, 0, j)),
        out_shape=jax.ShapeDtypeStruct((depth, 8, 3 * D_MODEL), F32),
        name="adaln_mod",
        compiler_params=_cparams(("parallel", "parallel")),
    )(cc, ada_w, ada_b.reshape(depth, 1, 3 * D_MODEL))


def _inproj_kernel(ns, has_bias, epi, x_ref, g_ref, s_ref, sh_ref, *rest):
    w_refs = rest[:ns]
    rest = rest[ns:]
    b_refs = rest[:ns] if has_bias else None
    rest = rest[ns:] if has_bias else rest
    out_refs, hn_ref = rest[:-1], rest[-1]

    @pl.when(pl.program_id(2) == 0)
    def _():
        hn_ref[...] = _modnorm(x_ref[...], g_ref[...], s_ref[...], sh_ref[...]).astype(BF16)

    tm = hn_ref.shape[0]
    rb = min(tm, PIPE_ROWS)

    def project(q):
        hn = hn_ref[q * rb:(q + 1) * rb, :]
        return [_dot(hn, w_refs[s][...]) for s in range(ns)]

    def finish(q, ys):
        if has_bias:
            ys = [y + b_refs[s][...] for s, y in enumerate(ys)]
        for o_ref, o in zip(out_refs, epi(ys)):
            o_ref[q * rb:(q + 1) * rb, :] = o.astype(o_ref.dtype)

    _software_pipeline(tm // rb, project, finish)


def _inproj(x2, nb, g_pre, s1p, sh, w_bf, bias, ns, n_out, epi, tm, tn=512):
    rows = x2.shape[0]
    nt = rows // nb // tm
    nj = D_INNER // tn
    row_map = lambda b, i, j: (b * nt + i, 0)
    vec_map = lambda b, i, j: (b, 0, 0)
    in_specs = [pl.BlockSpec((tm, D_MODEL), row_map),
                pl.BlockSpec((1, D_MODEL), lambda b, i, j: (0, 0)),
                pl.BlockSpec((None, 1, D_MODEL), vec_map),
                pl.BlockSpec((None, 1, D_MODEL), vec_map)]
    args = [x2, g_pre, s1p, sh]
    for s in range(ns):
        in_specs.append(pl.BlockSpec((D_MODEL, tn), functools.partial(lambda s, b, i, j: (0, s * nj + j), s)))
        args.append(w_bf)
    if bias is not None:
        for s in range(ns):
            in_specs.append(pl.BlockSpec((1, tn), functools.partial(lambda s, b, i, j: (0, s * nj + j), s)))
            args.append(bias)
    out_spec = pl.BlockSpec((tm, tn), lambda b, i, j: (b * nt + i, j))
    return pl.pallas_call(
        functools.partial(_inproj_kernel, ns, bias is not None, epi),
        grid=(nb, nt, nj),
        in_specs=in_specs,
        out_specs=[out_spec] * n_out,
        out_shape=[jax.ShapeDtypeStruct((rows, D_INNER), BF16)] * n_out,
        scratch_shapes=[pltpu.VMEM((tm, D_MODEL), BF16)],
        name="inproj%d" % ns,
        compiler_params=_cparams(("parallel", "parallel", "arbitrary")),
    )(*args)


def _conv_epi(ys):
    u, ug, gt = ys
    return [u * _sigmoid(ug), _silu(gt)]


def _na_epi(ys):
    q, k, v, g = ys
    return [q * (HEAD_DIM ** -0.5), k, v, _silu(g)]


def _outproj_kernel(pro, has_bias, nin, *refs):
    in_refs = refs[:nin]
    x_ref, gate_ref, gpost_ref, w_ref = refs[nin:nin + 4]
    b_ref = refs[nin + 4] if has_bias else None
    o_ref = refs[-1]
    tm = o_ref.shape[0]
    rb = min(tm, PIPE_ROWS)

    def project(q):
        return _dot(pro(slice(q * rb, (q + 1) * rb), *in_refs), w_ref[...])

    def finish(q, ox):
        sl = slice(q * rb, (q + 1) * rb)
        if has_bias:
            ox = ox + b_ref[...]
        o_ref[sl, :] = _post(x_ref[sl, :], ox, gate_ref[...], gpost_ref[...])

    _software_pipeline(tm // rb, project, finish)


def _outproj(ins, in_specs_extra, pro, x2, nb, gate, g_post, w_bf, bias, tm):
    rows = x2.shape[0]
    nt = rows // nb // tm
    row_map = lambda b, i: (b * nt + i, 0)
    in_specs = [pl.BlockSpec((tm, D_INNER), row_map) for _ in ins]
    args = list(ins)
    for arr, spec in in_specs_extra:
        in_specs.append(spec)
        args.append(arr)
    nin = len(args)
    in_specs += [pl.BlockSpec((tm, D_MODEL), row_map),
                 pl.BlockSpec((None, 1, D_MODEL), lambda b, i: (b, 0, 0)),
                 pl.BlockSpec((1, D_MODEL), lambda b, i: (0, 0)),
                 pl.BlockSpec((D_INNER, D_MODEL), lambda b, i: (0, 0))]
    args += [x2, gate, g_post, w_bf]
    if bias is not None:
        in_specs.append(pl.BlockSpec((1, D_MODEL), lambda b, i: (0, 0)))
        args.append(bias)
    return pl.pallas_call(
        functools.partial(_outproj_kernel, pro, bias is not None, nin),
        grid=(nb, nt),
        in_specs=in_specs,
        out_specs=pl.BlockSpec((tm, D_MODEL), row_map),
        out_shape=jax.ShapeDtypeStruct((rows, D_MODEL), F32),
        name="outproj_" + pro.__name__.strip("_"),
        compiler_params=_cparams(("parallel", "parallel")),
    )(*args)


def _na_pro(sl, o_ref, sg_ref):
    return o_ref[sl, :] * sg_ref[sl, :]


def _conv_pro(sl, u_ref, sg_ref, lnw_ref, lnb_ref):
    u = u_ref[sl, :].astype(F32)
    mu = jnp.mean(u, axis=-1, keepdims=True)
    d = u - mu
    var = jnp.mean(d * d, axis=-1, keepdims=True)
    y = d * lax.rsqrt(var + LN_EPS) * lnw_ref[...] + lnb_ref[...]
    return (_silu(y) * sg_ref[sl, :].astype(F32)).astype(BF16)


def _dwconv_kernel(tm, rb, nt, u_ref, up_ref, un_ref, dw_ref, db_ref, o_ref, buf_ref):
    i = pl.program_id(1)
    keep_prev = (i > 0).astype(F32)
    keep_next = (i < nt - 1).astype(F32)
    buf_ref[0:CONV_HALO, :] = up_ref[...].astype(F32) * keep_prev
    buf_ref[CONV_HALO:CONV_HALO + tm, :] = u_ref[...].astype(F32)
    buf_ref[CONV_HALO + tm:2 * CONV_HALO + tm, :] = un_ref[...].astype(F32) * keep_next
    dw = dw_ref[...]
    off = CONV_HALO - CONV_WIDTH // 2
    span = rb + 2 * CONV_HALO
    for r0 in range(0, tm, rb):
        acc = jnp.zeros((rb, u_ref.shape[1]), F32) + db_ref[...]
        win = buf_ref[r0:r0 + span, :]
        for s in range(8):
            shifted = win if s == 0 else pltpu.roll(win, span - s, axis=0)
            for a in range(span // 8):
                k = 8 * a + s - off
                if 0 <= k < CONV_WIDTH:
                    acc = acc + shifted[8 * a:8 * a + rb, :] * dw[k:k + 1, :]
        o_ref[r0:r0 + rb, :] = acc.astype(o_ref.dtype)


def _dwconv(u2, nb, dw, dw_b, tm, tc=512, rb=32):
    rows = u2.shape[0]
    t_len = rows // nb
    nt = t_len // tm
    nj = D_INNER // tc
    hb = tm // CONV_HALO
    nhb = rows // CONV_HALO
    prev_map = lambda b, i, j: (jnp.maximum((b * nt + i) * hb - 1, 0), j)
    next_map = lambda b, i, j: (jnp.minimum((b * nt + i + 1) * hb, nhb - 1), j)
    return pl.pallas_call(
        functools.partial(_dwconv_kernel, tm, rb, nt),
        grid=(nb, nt, nj),
        in_specs=[pl.BlockSpec((tm, tc), lambda b, i, j: (b * nt + i, j)),
                  pl.BlockSpec((CONV_HALO, tc), prev_map),
                  pl.BlockSpec((CONV_HALO, tc), next_map),
                  pl.BlockSpec((CONV_WIDTH, tc), lambda b, i, j: (0, j)),
                  pl.BlockSpec((1, tc), lambda b, i, j: (0, j))],
        out_specs=pl.BlockSpec((tm, tc), lambda b, i, j: (b * nt + i, j)),
        out_shape=jax.ShapeDtypeStruct((rows, D_INNER), BF16),
        scratch_shapes=[pltpu.VMEM((tm + 2 * CONV_HALO, tc), F32)],
        name="dwconv",
        compiler_params=_cparams(("parallel", "parallel", "parallel")),
    )(u2, u2, u2, dw, dw_b)


def _stack_heads(q, lane_lo):
    zero = jnp.zeros_like(q)
    return jnp.concatenate([jnp.where(lane_lo, q, zero), jnp.where(lane_lo, zero, q)], axis=0)


def _na_kernel(n_rows, q_ref, k_ref, v_ref, kc_ref, vc_ref, bias_ref, o_ref):
    kr = min(NA_ROWS, n_rows)
    nloc = kr * GRID_W
    lane_lo = lax.broadcasted_iota(jnp.int32, (GRID_W, LANES), 1) < HEAD_DIM
    kc = kc_ref[...]
    vc = vc_ref[...]

    def rows(it, carry):
        rr = [it * NA_UNROLL + u for u in range(NA_UNROLL)]
        start = [jnp.clip(r - kr // 2, 0, n_rows - kr) for r in rr]
        q0 = [pl.multiple_of(r * GRID_W, GRID_W) for r in rr]
        k0 = [pl.multiple_of(s * GRID_W, GRID_W) for s in start]
        qm = [_stack_heads(q_ref[pl.ds(q, GRID_W), :], lane_lo) for q in q0]
        s_loc = [_dot_t(a, k_ref[pl.ds(k, nloc), :]) for a, k in zip(qm, k0)]
        s_ctx_all = _dot_t(jnp.concatenate(qm, axis=0), kc)
        s_ctx = [s_ctx_all[u * LANES:(u + 1) * LANES] for u in range(NA_UNROLL)]
        s_loc = [s + bias_ref[r - st, 0] for s, r, st in zip(s_loc, rr, start)]
        m = [jnp.maximum(jnp.max(a, axis=-1, keepdims=True), jnp.max(b, axis=-1, keepdims=True))
             for a, b in zip(s_loc, s_ctx)]
        p_loc = [jnp.exp(a - mm) for a, mm in zip(s_loc, m)]
        p_ctx = [jnp.exp(a - mm) for a, mm in zip(s_ctx, m)]
        den = [jnp.sum(a, axis=-1, keepdims=True) + jnp.sum(b, axis=-1, keepdims=True)
               for a, b in zip(p_loc, p_ctx)]
        o_ctx = _dot(jnp.concatenate([b.astype(BF16) for b in p_ctx], axis=0), vc)
        o2 = [_dot(a.astype(BF16), v_ref[pl.ds(k, nloc), :]) + o_ctx[u * LANES:(u + 1) * LANES]
              for u, (a, k) in enumerate(zip(p_loc, k0))]
        for o, d, q in zip(o2, den, q0):
            o = o / d
            o_ref[pl.ds(q, GRID_W), :] = jnp.where(lane_lo, o[:GRID_W], o[GRID_W:]).astype(o_ref.dtype)
        return carry

    lax.fori_loop(0, n_rows // NA_UNROLL, rows, 0)


def _na_attention(q, k, v, kc, vc, bias8, nb):
    rows = q.shape[0]
    t_len = rows // nb
    n_rows = t_len // GRID_W
    ctx_len = kc.shape[0] // nb
    nhp = D_INNER // LANES
    kr = min(NA_ROWS, n_rows)
    big = pl.BlockSpec((t_len, LANES), lambda b, h: (b, h))
    cspec = pl.BlockSpec((ctx_len, LANES), lambda b, h: (b, h))
    return pl.pallas_call(
        functools.partial(_na_kernel, n_rows),
        grid=(nb, nhp),
        in_specs=[big, big, big, cspec, cspec,
                  pl.BlockSpec((NA_ROWS, 1, LANES, kr * GRID_W), lambda b, h: (0, h, 0, 0))],
        out_specs=big,
        out_shape=jax.ShapeDtypeStruct((rows, D_INNER), BF16),
        name="na_attn",
        compiler_params=_cparams(("parallel", "parallel")),
    )(q, k, v, kc, vc, bias8)


def _ctx_attn_kernel(q_ref, k_ref, v_ref, o_ref):
    n = q_ref.shape[0]
    lane_lo = lax.broadcasted_iota(jnp.int32, (n, LANES), 1) < HEAD_DIM
    qm = _stack_heads(q_ref[...], lane_lo)
    s = _dot_t(qm, k_ref[...])
    m = jnp.max(s, axis=-1, keepdims=True)
    p = jnp.exp(s - m)
    den = jnp.sum(p, axis=-1, keepdims=True)
    o2 = _dot(p.astype(BF16), v_ref[...]) / den
    o_ref[...] = jnp.where(lane_lo, o2[:n], o2[n:]).astype(o_ref.dtype)


def _ctx_attention(qc, kc, vc, nb):
    ctx_len = qc.shape[0] // nb
    spec = pl.BlockSpec((ctx_len, LANES), lambda b, h: (b, h))
    return pl.pallas_call(
        _ctx_attn_kernel,
        grid=(nb, D_INNER // LANES),
        in_specs=[spec, spec, spec],
        out_specs=spec,
        out_shape=jax.ShapeDtypeStruct(qc.shape, BF16),
        name="ctx_attn",
        compiler_params=_cparams(("parallel", "parallel")),
    )(qc, kc, vc)


def _na_bias_table(rpb, n_rows):
    kr = min(NA_ROWS, n_rows)
    cols = np.arange(GRID_W)
    cstart = np.clip(cols - NA_COLS // 2, 0, GRID_W - NA_COLS)
    jc = np.arange(GRID_W)
    inwin = (jc[None, :] >= cstart[:, None]) & (jc[None, :] < cstart[:, None] + NA_COLS)
    pad = jnp.pad(rpb, ((0, 0), (0, 0), (GRID_W, GRID_W)))
    off = GRID_W + NA_COLS - 1
    toep = jnp.stack([pad[:, :, off - c:off - c + GRID_W] for c in range(GRID_W)], axis=1)
    toep = lax.optimization_barrier(jnp.where(inwin[None, :, None, :], toep, NEG_BIG))
    tab = jnp.stack([toep[:, :, NA_ROWS - 1 - d:NA_ROWS - 1 - d + kr] for d in range(NA_ROWS)])
    return tab.reshape(NA_ROWS, N_HEADS // 2, 2 * GRID_W, kr * GRID_W).astype(F32)


def _rwkv_prep_kernel(tm, nt, x_ref, xp_ref, xn_ref, g_ref, s_ref, sh_ref, mu_ref,
                      wr_ref, wk_ref, wv_ref, wg_ref, w1_ref, w2_ref, w0_ref,
                      a1_ref, a2_ref, a0_ref, kk_ref, ka_ref, rk_ref, bd_ref,
                      r_o, v_o, kkn_o, bon_o, sg_o, lw_o, kd_o, bb_o,
                      xs_ref, hw_ref, ha_ref):
    i = pl.program_id(1)

    @pl.when(pl.program_id(2) == 0)
    def _():
        g, s1p, sh = g_ref[...], s_ref[...], sh_ref[...]
        h = _modnorm(x_ref[...], g, s1p, sh)
        hp = _modnorm(xp_ref[...], g, s1p, sh)[7:8] * (i > 0).astype(F32)
        hn = _modnorm(xn_ref[...], g, s1p, sh)[0:1] * (i < nt - 1).astype(F32)
        rows = lax.broadcasted_iota(jnp.int32, (tm, 1), 0)
        prev = jnp.where(rows == 0, hp, pltpu.roll(h, 1, axis=0))
        nxt = jnp.where(rows == tm - 1, hn, pltpu.roll(h, tm - 1, axis=0))
        ts = 0.5 * (prev + nxt) - h
        for s in range(6):
            xs_ref[s] = (h + ts * mu_ref[s:s + 1, :]).astype(BF16)
        for z in range(2):
            hw_ref[z] = jnp.tanh(_dot(xs_ref[4], w1_ref[z])).astype(BF16)
            ha_ref[z] = _dot(xs_ref[5], a1_ref[z]).astype(BF16)

    bd = bd_ref[...]
    rb = min(tm, PIPE_ROWS)

    def project(q):
        sl = slice(q * rb, (q + 1) * rb)
        main = [_dot(xs_ref[s, sl, :], w[...]) for s, w in enumerate((wr_ref, wk_ref, wv_ref, wg_ref))]
        lora = [(_dot(hw_ref[z, sl, :], w2_ref[z]), _dot(ha_ref[z, sl, :], a2_ref[z])) for z in range(2)]
        return main, lora

    def finish(q, projected):
        sl = slice(q * rb, (q + 1) * rb)
        (r, k, v, g), lora = projected
        kkh = k * kk_ref[...]
        kk = kkh * lax.rsqrt(_head_sum(kkh * kkh, bd) + 1e-12)
        ksum = jnp.zeros_like(k)
        k_scaled = k * ka_ref[...]
        k_rest = k - k_scaled
        for z in range(2):
            w_raw = w0_ref[z] + lora[z][0]
            lw_o[z, sl, :] = -math.exp(-0.5) * _sigmoid(w_raw)
            a = _sigmoid(a0_ref[z] + lora[z][1])
            kd = k_rest + k_scaled * a
            kd_o[z, sl, :] = kd.astype(BF16)
            bb_o[z, sl, :] = (kk * a).astype(BF16)
            ksum = ksum + kd
        r_o[sl, :] = r.astype(BF16)
        v_o[sl, :] = v.astype(BF16)
        kkn_o[sl, :] = kk.astype(BF16)
        bon_o[sl, :] = (_head_sum(r * ksum * rk_ref[...], bd) * v).astype(BF16)
        sg_o[sl, :] = _silu(g).astype(BF16)

    _software_pipeline(tm // rb, project, finish)


def _rwkv_prep(x2, nb, g_pre, s1p, sh, p, tm):
    rows = x2.shape[0]
    nt = rows // nb // tm
    tn = GROUP_W
    hb = tm // 8
    nhb = rows // 8
    c3 = lambda b, i, j: (0, 0)
    vec = lambda b, i, j: (b, 0, 0)
    colv = pl.BlockSpec((1, tn), lambda b, i, j: (0, j))
    in_specs = [
        pl.BlockSpec((tm, D_MODEL), lambda b, i, j: (b * nt + i, 0)),
        pl.BlockSpec((8, D_MODEL), lambda b, i, j: (jnp.maximum((b * nt + i) * hb - 1, 0), 0)),
        pl.BlockSpec((8, D_MODEL), lambda b, i, j: (jnp.minimum((b * nt + i + 1) * hb, nhb - 1), 0)),
        pl.BlockSpec((1, D_MODEL), c3),
        pl.BlockSpec((None, 1, D_MODEL), vec),
        pl.BlockSpec((None, 1, D_MODEL), vec),
        pl.BlockSpec((6, D_MODEL), c3),
    ]
    args = [x2, x2, x2, g_pre, s1p, sh, p['mu']]
    for s in range(4):
        in_specs.append(pl.BlockSpec((None, D_MODEL, tn), functools.partial(lambda s, b, i, j: (s, 0, j), s)))
        args.append(p['w_in'])
    lora_in = pl.BlockSpec((2, D_MODEL, LORA), lambda b, i, j: (0, 0, 0))
    lora_out = pl.BlockSpec((2, LORA, tn), lambda b, i, j: (0, 0, j))
    lora_b = pl.BlockSpec((2, 1, tn), lambda b, i, j: (0, 0, j))
    in_specs += [lora_in, lora_out, lora_b, lora_in, lora_out, lora_b, colv, colv, colv,
                 pl.BlockSpec((GROUP_W, GROUP_W), c3)]
    args += [p['w1'], p['w2'], p['w0'], p['a1'], p['a2'], p['a0'], p['k_k'], p['k_a'], p['r_k'], p['bd']]
    gm = pl.BlockSpec((None, tm, tn), lambda b, i, j: (j, b * nt + i, 0))
    gm2 = pl.BlockSpec((2, None, tm, tn), lambda b, i, j: (0, j, b * nt + i, 0))
    sh1 = jax.ShapeDtypeStruct((N_GROUPS, rows, tn), BF16)
    sh2 = jax.ShapeDtypeStruct((2, N_GROUPS, rows, tn), BF16)
    return pl.pallas_call(
        functools.partial(_rwkv_prep_kernel, tm, nt),
        grid=(nb, nt, N_GROUPS),
        in_specs=in_specs,
        out_specs=[gm, gm, gm, gm, gm, gm2, gm2, gm2],
        out_shape=[sh1, sh1, sh1, sh1, sh1, jax.ShapeDtypeStruct((2, N_GROUPS, rows, tn), F32), sh2, sh2],
        scratch_shapes=[pltpu.VMEM((6, tm, D_MODEL), BF16),
                        pltpu.VMEM((2, tm, LORA), BF16),
                        pltpu.VMEM((2, tm, LORA), BF16)],
        name="rwkv_prep",
        compiler_params=_cparams(("parallel", "parallel", "arbitrary")),
    )(*args)


def _block_diag(x_bf, bdm):
    return jnp.concatenate([x_bf] * HEADS_PER_PAIR, axis=0) * bdm


def _scan_kernel(r_ref, v_ref, kk_ref, lw_ref, kd_ref, bb_ref, sin_ref,
                 tri_ref, mk_ref, eye_ref, bdm_ref, y_ref, s_ref):
    @pl.when(pl.program_id(1) == 0)
    def _():
        s_ref[...] = sin_ref[...]

    nb = s_ref.shape[0]
    pairs = N_GROUPS * GROUP_W // LANES
    tri = tri_ref[...]
    bdm = bdm_ref[...]
    bdm_f = bdm.astype(F32)

    def each(fn, *lists):
        return [fn(*args) for args in zip(*lists)]

    def bdiag(x):
        return _block_diag(x, bdm)

    def mm(a_list, b_list):
        return each(lambda a, b: _dot(a.astype(BF16), bdiag(b.astype(BF16))), a_list, b_list)

    grp = range(nb * pairs)

    def at_chain(g):
        b, p = divmod(g, pairs)
        return (p // 2, b, slice(None), slice((p % 2) * LANES, (p % 2 + 1) * LANES))

    def ld(ref, g):
        return ref[at_chain(g)]

    def st_at(g):
        grp_i, b, rows_i, lanes_i = at_chain(g)
        return (b, grp_i, rows_i, lanes_i)

    ms, mi = mk_ref[0], mk_ref[1]
    lw = [ld(lw_ref, g) for g in grp]
    cs = each(lambda x: _dot(tri, jnp.concatenate(_split_bf16(x), axis=1)), lw)
    cs = each(lambda x: x[:, :LANES] + x[:, LANES:], cs)
    tot = each(lambda x: jnp.sum(x, axis=0, keepdims=True), lw)
    at = [(-ld(kk_ref, g).astype(F32) * jnp.exp(cs[g] - lw[g])).astype(BF16) for g in grp]
    rt = [(ld(r_ref, g).astype(F32) * jnp.exp(cs[g])).astype(BF16) for g in grp]
    lhs = each(lambda a, b: jnp.concatenate([a, b], axis=0), at, rt)
    def bdiag_t(x):
        return (jnp.concatenate([x] * HEADS_PER_PAIR, axis=0) * bdm_f).T.astype(BF16)

    a12 = [_dot(lhs[g], jnp.concatenate([bdiag_t(ld(bb_ref, g).astype(F32) * jnp.exp(-cs[g])),
                                         bdiag_t(ld(kd_ref, g).astype(F32) * jnp.exp(-cs[g]))], axis=1))
           for g in grp]
    a1 = [a[:, :LANES] for a in a12]
    a2 = [a[:, LANES:] for a in a12]
    n = each(lambda a: a[:CHUNK] * ms, a1)
    a_rb = each(lambda a: (a[CHUNK:] * mi).astype(BF16), a1)
    a_k = each(lambda a: jnp.concatenate([a[:CHUNK] * ms, a[CHUNK:] * mi], axis=0).astype(BF16), a2)
    st = [s_ref[st_at(g)] for g in grp]
    from_s = [_dot(lhs[g], bdiag_t(st[g])) for g in grp]
    from_v = [_dot(a_k[g], bdiag(ld(v_ref, g))) for g in grp]

    base = mk_ref[2]
    pw = each(lambda x: x * base, n)
    t_inv = each(lambda x: eye_ref[...] + x, pw)
    pw = mm(pw, pw)
    both = each(lambda t, p: _dot(jnp.concatenate([t, p], axis=0).astype(BF16), bdiag(p.astype(BF16))),
                t_inv, pw)
    t_inv = each(lambda t, b: t + b[:CHUNK], t_inv, both)
    t_inv = each(jnp.add, t_inv, mm(t_inv, each(lambda b: b[CHUNK:], both)))
    def mm_stack(lhs_lists, rhs_list):
        def one(r, *parts):
            return _dot(jnp.concatenate(parts, axis=0).astype(BF16), bdiag(r.astype(BF16)))
        res = each(one, rhs_list, *lhs_lists)
        return [each(lambda x: x[i * CHUNK:(i + 1) * CHUNK], res) for i in range(len(lhs_lists))]

    x8, z16, z32 = mm_stack([each(lambda x: x * mk_ref[3 + lvl], n) for lvl in range(3)], t_inv)
    d_t, d16, d32 = mm_stack([t_inv, z16, z32], x8)
    t_inv = each(jnp.add, t_inv, d_t)
    x16 = each(jnp.add, z16, d16)
    z32 = each(jnp.add, z32, d32)
    d_t, d32 = mm_stack([t_inv, z32], x16)
    t_inv = each(jnp.add, t_inv, d_t)
    x32 = each(jnp.add, z32, d32)
    t_inv = each(jnp.add, t_inv, mm(t_inv, x32))

    rhs_u = [from_s[g][:CHUNK] + from_v[g][:CHUNK] for g in grp]
    u = mm(t_inv, rhs_u)
    y_rb = mm(a_rb, u)
    for g in grp:
        y = from_s[g][CHUNK:] + y_rb[g] + from_v[g][CHUNK:]
        y_ref[at_chain(g)] = y.astype(y_ref.dtype)

    for g in grp:
        e_rem = jnp.exp(tot[g] - cs[g])
        uv_t = jnp.concatenate([u[g], ld(v_ref, g).astype(F32)], axis=0).T.astype(BF16)
        bk = jnp.concatenate([ld(bb_ref, g).astype(F32) * e_rem, ld(kd_ref, g).astype(F32) * e_rem],
                             axis=0).astype(BF16)
        full = _dot(uv_t, bk) * bdm_f
        upd = full[0:CHUNK]
        for hh in range(1, HEADS_PER_PAIR):
            upd = upd + full[hh * CHUNK:(hh + 1) * CHUNK]
        s_ref[st_at(g)] = st[g] * jnp.exp(tot[g]) + upd


def _scan_consts():
    t = np.arange(CHUNK)
    tt, jj = t[:, None], t[None, :]
    lower = jj <= tt
    strict = jj < tt
    base = strict & (tt // 8 == jj // 8)
    pairs = [(tt // s == jj // s + 1) & (tt // (2 * s) == jj // (2 * s)) for s in (8, 16, 32)]
    fwd = np.stack([strict, lower, base] + pairs)
    both = np.stack([fwd, fwd.transpose(0, 2, 1)]).astype(np.float32)
    masks = np.tile(both, (1, 1, 1, HEADS_PER_PAIR))
    tri = np.stack([lower, lower.T]).astype(np.float32)
    eye = np.tile(np.eye(CHUNK, dtype=np.float32), (1, HEADS_PER_PAIR))
    blk = np.arange(LANES) // HEAD_DIM
    bdm = (blk[:, None] == blk[None, :]).astype(np.float32)
    return jnp.asarray(tri, BF16), jnp.asarray(masks, F32), jnp.asarray(eye, F32), jnp.asarray(bdm, BF16)


def _rwkv_scan(r, v, kk, lw, kd, bb, s_in, nb):
    rows = r.shape[1]
    t_len = rows // nb
    nc = t_len // CHUNK
    tri, masks, eye, bdm = _scan_consts()
    split = lambda a: a.reshape(a.shape[:-2] + (nb, t_len, GROUP_W))
    blk = lambda d, c: c + d * (nc - 1 - 2 * c)
    shared = pl.BlockSpec((N_GROUPS, nb, CHUNK, GROUP_W), lambda d, c: (0, 0, blk(d, c), 0))
    perdir = pl.BlockSpec((None, N_GROUPS, nb, CHUNK, GROUP_W), lambda d, c: (d, 0, 0, blk(d, c), 0))
    sspec = pl.BlockSpec((None, nb, N_GROUPS, HEAD_DIM, GROUP_W), lambda d, c: (d, 0, 0, 0, 0))
    y, s_out = pl.pallas_call(
        _scan_kernel,
        grid=(2, nc),
        in_specs=[shared, shared, shared, perdir, perdir, perdir, sspec,
                  pl.BlockSpec((None, CHUNK, CHUNK), lambda d, c: (d, 0, 0)),
                  pl.BlockSpec((None, 6, CHUNK, LANES), lambda d, c: (d, 0, 0, 0)),
                  pl.BlockSpec((CHUNK, LANES), lambda d, c: (0, 0)),
                  pl.BlockSpec((LANES, LANES), lambda d, c: (0, 0))],
        out_specs=[perdir, sspec],
        out_shape=[jax.ShapeDtypeStruct((2, N_GROUPS, nb, t_len, GROUP_W), BF16),
                   jax.ShapeDtypeStruct(s_in.shape, F32)],
        name="rwkv_scan",
        compiler_params=_cparams(("parallel", "arbitrary")),
    )(split(r), split(v), split(kk), split(lw), split(kd), split(bb), s_in, tri, masks, eye, bdm)
    return y.reshape(2, N_GROUPS, rows, GROUP_W), s_out


def _rwkv_out_kernel(y_ref, bon_ref, sg_ref, lnw_ref, lnb_ref, bd_ref,
                     x_ref, gate_ref, gpost_ref, w_ref, o_ref):
    bd = bd_ref[...] * (1.0 / HEAD_DIM)
    grp = range(N_GROUPS)
    yy = [y_ref[0, g].astype(F32) + y_ref[1, g].astype(F32) for g in grp]
    d = [yy[g] - _head_sum(yy[g], bd) for g in grp]
    var = [_head_sum(d[g] * d[g], bd) for g in grp]
    acc = jnp.zeros(o_ref.shape, F32)
    for g in grp:
        yn = d[g] * lax.rsqrt(var[g] + GN_EPS) * lnw_ref[g] + lnb_ref[g]
        o = (yn + bon_ref[g].astype(F32)) * sg_ref[g].astype(F32)
        acc = acc + _dot(o.astype(BF16), w_ref[g])
    o_ref[...] = _post(x_ref[...], acc, gate_ref[...], gpost_ref[...])


def _rwkv_out(y, bon, sg, p, x2, nb, gate, g_post, tm):
    rows = x2.shape[0]
    nt = rows // nb // tm
    row_map = lambda b, i: (b * nt + i, 0)
    gm = pl.BlockSpec((N_GROUPS, tm, GROUP_W), lambda b, i: (0, b * nt + i, 0))
    pv = pl.BlockSpec((N_GROUPS, 1, GROUP_W), lambda b, i: (0, 0, 0))
    return pl.pallas_call(
        _rwkv_out_kernel,
        grid=(nb, nt),
        in_specs=[pl.BlockSpec((2, N_GROUPS, tm, GROUP_W), lambda b, i: (0, 0, b * nt + i, 0)),
                  gm, gm, pv, pv,
                  pl.BlockSpec((GROUP_W, GROUP_W), lambda b, i: (0, 0)),
                  pl.BlockSpec((tm, D_MODEL), row_map),
                  pl.BlockSpec((None, 1, D_MODEL), lambda b, i: (b, 0, 0)),
                  pl.BlockSpec((1, D_MODEL), lambda b, i: (0, 0)),
                  pl.BlockSpec((N_GROUPS, GROUP_W, D_MODEL), lambda b, i: (0, 0, 0))],
        out_specs=pl.BlockSpec((tm, D_MODEL), row_map),
        out_shape=jax.ShapeDtypeStruct((rows, D_MODEL), F32),
        name="rwkv_out",
        compiler_params=_cparams(("parallel", "parallel")),
    )(y, bon, sg, p['ln_w'], p['ln_b'], p['bd'], x2, gate, g_post, p['w_out'])


def _tile(t_len, pref):
    return pref if t_len % pref == 0 else t_len


def _rwkv_layer(xs, cs, nb, mods, g_pre, g_post, p, need_ctx):
    t_x, t_c = xs.shape[0] // nb, cs.shape[0] // nb
    (s1p_x, sh_x, gate_x), (s1p_c, sh_c, gate_c) = mods
    zeros = jnp.zeros((2, nb, N_GROUPS, HEAD_DIM, GROUP_W), F32)
    rc, vc, kkc, bonc, sgc, lwc, kdc, bbc = _rwkv_prep(cs, nb, g_pre, s1p_c, sh_c, p, _tile(t_c, 256))
    yc, state_c = _rwkv_scan(rc, vc, kkc, lwc, kdc, bbc, zeros, nb)
    rx, vx, kkx, bonx, sgx, lwx, kdx, bbx = _rwkv_prep(xs, nb, g_pre, s1p_x, sh_x, p, _tile(t_x, 1024))
    yx, _ = _rwkv_scan(rx, vx, kkx, lwx, kdx, bbx, state_c, nb)
    xs_new = _rwkv_out(yx, bonx, sgx, p, xs, nb, gate_x, g_post, _tile(t_x, 512))
    cs_new = _rwkv_out(yc, bonc, sgc, p, cs, nb, gate_c, g_post, _tile(t_c, 256)) if need_ctx else cs
    return xs_new, cs_new


def _conv_layer(xs, cs, nb, mods, g_pre, g_post, p, need_ctx):
    def run(x2, mod, t_len):
        s1p, sh, gate = mod
        u, sgate = _inproj(x2, nb, g_pre, s1p, sh, p['w_in'], p['b_in'], 3, 2, _conv_epi, _tile(t_len, 1024))
        uc = _dwconv(u, nb, p['dw'], p['dw_b'], _tile(t_len, 256))
        vspec = pl.BlockSpec((1, D_INNER), lambda b, i: (0, 0))
        return _outproj([uc, sgate], [(p['ln_w'], vspec), (p['ln_b'], vspec)], _conv_pro,
                        x2, nb, gate, g_post, p['w_out'], p['b_out'], _tile(t_len, 512))
    xs_new = run(xs, mods[0], xs.shape[0] // nb)
    cs_new = run(cs, mods[1], cs.shape[0] // nb) if need_ctx else cs
    return xs_new, cs_new


def _na_layer(xs, cs, nb, mods, g_pre, g_post, p, need_ctx):
    t_x, t_c = xs.shape[0] // nb, cs.shape[0] // nb
    (s1p_x, sh_x, gate_x), (s1p_c, sh_c, gate_c) = mods
    q, k, v, sg = _inproj(xs, nb, g_pre, s1p_x, sh_x, p['w_in'], None, 4, 4, _na_epi, _tile(t_x, 1024))
    qc, kc, vc, sgc = _inproj(cs, nb, g_pre, s1p_c, sh_c, p['w_in'], None, 4, 4, _na_epi, _tile(t_c, 256))
    o = _na_attention(q, k, v, kc, vc, _na_bias_table(p['rpb'], t_x // GRID_W), nb)
    xs_new = _outproj([o, sg], [], _na_pro, xs, nb, gate_x, g_post, p['w_out'], None, _tile(t_x, 512))
    cs_new = cs
    if need_ctx:
        oc = _ctx_attention(qc, kc, vc, nb)
        cs_new = _outproj([oc, sgc], [], _na_pro, cs, nb, gate_c, g_post, p['w_out'], None, _tile(t_c, 256))
    return xs_new, cs_new


def kernel(x, c, ctx, c_ctx, ada_w, ada_b, norm_pre, norm_post, rw_mu, rw_w_in, rw_w0, rw_w1, rw_w2, rw_a0, rw_a1, rw_a2, rw_k_k, rw_k_a, rw_r_k, rw_ln_w, rw_ln_b, rw_w_out, cf_w_in, cf_b_in, cf_dw, cf_dw_b, cf_ln_w, cf_ln_b, cf_w_out, cf_b_out, na_w_in, na_rpb, na_w_out):
    nb, t_len, _ = x.shape
    ctx_len = ctx.shape[1]
    depth = ada_w.shape[0]
    xs = x.reshape(nb * t_len, D_MODEL)
    cs = ctx.reshape(nb * ctx_len, D_MODEL)

    cc = jnp.zeros((8, D_MODEL), F32).at[:nb].set(c).at[nb].set(c_ctx)
    mod = _modulation(cc, ada_w, ada_b)
    bd = jnp.asarray(np.kron(np.eye(GROUP_W // HEAD_DIM), np.ones((HEAD_DIM, HEAD_DIM))), BF16)

    for i in range(depth):
        kind, j = i % 3, i // 3
        need_ctx = i < depth - 1
        shift, scale, gate = (mod[i, :, s * D_MODEL:(s + 1) * D_MODEL] for s in range(3))
        mods = []
        for sel in (slice(0, nb), slice(nb, nb + 1)):
            mods.append(tuple(jnp.broadcast_to(m[sel], (nb, D_MODEL)).reshape(nb, 1, D_MODEL)
                              for m in (1.0 + scale, shift, gate)))
        g_pre = norm_pre[i].reshape(1, D_MODEL)
        g_post = norm_post[i].reshape(1, D_MODEL)
        if kind == 0:
            p = dict(mu=rw_mu[j], w_in=rw_w_in[j].astype(BF16),
                     w0=rw_w0[j].reshape(2, 1, D_INNER), w1=rw_w1[j].astype(BF16), w2=rw_w2[j].astype(BF16),
                     a0=rw_a0[j].reshape(2, 1, D_INNER), a1=rw_a1[j].astype(BF16), a2=rw_a2[j].astype(BF16),
                     k_k=rw_k_k[j].reshape(1, D_INNER), k_a=rw_k_a[j].reshape(1, D_INNER),
                     r_k=rw_r_k[j].reshape(1, D_INNER),
                     ln_w=rw_ln_w[j].reshape(N_GROUPS, 1, GROUP_W), ln_b=rw_ln_b[j].reshape(N_GROUPS, 1, GROUP_W),
                     w_out=rw_w_out[j].astype(BF16).reshape(N_GROUPS, GROUP_W, D_MODEL), bd=bd)
            xs, cs = _rwkv_layer(xs, cs, nb, mods, g_pre, g_post, p, need_ctx)
        elif kind == 1:
            p = dict(w_in=cf_w_in[j].astype(BF16), b_in=cf_b_in[j].reshape(1, 3 * D_INNER),
                     dw=cf_dw[j], dw_b=cf_dw_b[j].reshape(1, D_INNER),
                     ln_w=cf_ln_w[j].reshape(1, D_INNER), ln_b=cf_ln_b[j].reshape(1, D_INNER),
                     w_out=cf_w_out[j].astype(BF16), b_out=cf_b_out[j].reshape(1, D_MODEL))
            xs, cs = _conv_layer(xs, cs, nb, mods, g_pre, g_post, p, need_ctx)
        else:
            p = dict(w_in=na_w_in[j].astype(BF16), rpb=na_rpb[j], w_out=na_w_out[j].astype(BF16))
            xs, cs = _na_layer(xs, cs, nb, mods, g_pre, g_post, p, need_ctx)
    return xs.reshape(nb, t_len, D_MODEL)
```

```python
import functools
import math

import jax
import jax.numpy as jnp
import numpy as np
from jax import lax
from jax.experimental import pallas as pl
from jax.experimental.pallas import tpu as pltpu

F32 = jnp.float32
BF16 = jnp.bfloat16

D_MODEL = 1024
D_INNER = 2048
N_HEADS = 32
HEAD_DIM = 64
LORA = 64
GRID_W = 64
CONV_WIDTH = 31
NA_ROWS = 8
NA_COLS = 16
RMS_EPS = 1e-6
LN_EPS = 1e-5
GN_EPS = 6.4e-4

LANES = 128
GROUP_W = 256
N_GROUPS = D_INNER // GROUP_W
HEADS_PER_PAIR = LANES // HEAD_DIM
CHUNK = 64
CONV_HALO = 16
NA_UNROLL = 16
PIPE_ROWS = 256
NEG_BIG = -1e30
VMEM_LIMIT = 56 * 1024 * 1024


def _cparams(sem):
    return pltpu.CompilerParams(dimension_semantics=sem, vmem_limit_bytes=VMEM_LIMIT)


def _dot(a, b):
    return jnp.dot(a, b, preferred_element_type=F32)


def _dot_t(a, b):
    return lax.dot_general(a, b, (((1,), (1,)), ((), ())), preferred_element_type=F32)


def _split_bf16(x):
    hi = x.astype(BF16)
    lo = (x - hi.astype(F32)).astype(BF16)
    return hi, lo


def _sigmoid(x):
    return 1.0 / (1.0 + jnp.exp(-x))


def _silu(x):
    return x * _sigmoid(x)


def _modnorm(x, g, s1p, sh):
    ms = jnp.mean(x * x, axis=-1, keepdims=True)
    return (x * lax.rsqrt(ms + RMS_EPS) * g) * s1p + sh


def _post(xres, ox, gate, gpost):
    ms = jnp.mean(ox * ox, axis=-1, keepdims=True)
    return xres + gate * (ox * lax.rsqrt(ms + RMS_EPS) * gpost)


def _software_pipeline(nblk, project, finish):
    pending = project(0)
    for q in range(1, nblk):
        ahead = project(q)
        finish(q - 1, pending)
        pending = ahead
    finish(nblk - 1, pending)


def _head_sum(x, bd):
    return _dot(x.astype(BF16), bd)


def _mod_kernel(c_ref, w_ref, b_ref, o_ref):
    a_hi, a_lo = _split_bf16(_silu(c_ref[...]))
    w_hi, w_lo = _split_bf16(w_ref[...])
    o_ref[...] = _dot(a_hi, w_hi) + _dot(a_hi, w_lo) + _dot(a_lo, w_hi) + b_ref[...]


def _modulation(cc, ada_w, ada_b):
    depth = ada_w.shape[0]
    tn = 1024
    return pl.pallas_call(
        _mod_kernel,
        grid=(depth, 3 * D_MODEL // tn),
        in_specs=[pl.BlockSpec((8, D_MODEL), lambda l, j: (0, 0)),
                  pl.BlockSpec((None, D_MODEL, tn), lambda l, j: (l, 0, j)),
                  pl.BlockSpec((None, 1, tn), lambda l, j: (l, 0, j))],
        out_specs=pl.BlockSpec((None, 8, tn), lambda l, j: (l, 0, j)),
        out_shape=jax.ShapeDtypeStruct((depth, 8, 3 * D_MODEL), F32),
        name="adaln_mod",
        compiler_params=_cparams(("parallel", "parallel")),
    )(cc, ada_w, ada_b.reshape(depth, 1, 3 * D_MODEL))


def _inproj_kernel(ns, has_bias, epi, x_ref, g_ref, s_ref, sh_ref, *rest):
    w_refs = rest[:ns]
    rest = rest[ns:]
    b_refs = rest[:ns] if has_bias else None
    rest = rest[ns:] if has_bias else rest
    out_refs, hn_ref = rest[:-1], rest[-1]

    @pl.when(pl.program_id(2) == 0)
    def _():
        hn_ref[...] = _modnorm(x_ref[...], g_ref[...], s_ref[...], sh_ref[...]).astype(BF16)

    tm = hn_ref.shape[0]
    rb = min(tm, PIPE_ROWS)

    def project(q):
        hn = hn_ref[q * rb:(q + 1) * rb, :]
        return [_dot(hn, w_refs[s][...]) for s in range(ns)]

    def finish(q, ys):
        if has_bias:
            ys = [y + b_refs[s][...] for s, y in enumerate(ys)]
        for o_ref, o in zip(out_refs, epi(ys)):
            o_ref[q * rb:(q + 1) * rb, :] = o.astype(o_ref.dtype)

    _software_pipeline(tm // rb, project, finish)


def _inproj(x2, nb, g_pre, s1p, sh, w_bf, bias, ns, n_out, epi, tm, tn=512):
    rows = x2.shape[0]
    nt = rows // nb // tm
    nj = D_INNER // tn
    row_map = lambda b, i, j: (b * nt + i, 0)
    vec_map = lambda b, i, j: (b, 0, 0)
    in_specs = [pl.BlockSpec((tm, D_MODEL), row_map),
                pl.BlockSpec((1, D_MODEL), lambda b, i, j: (0, 0)),
                pl.BlockSpec((None, 1, D_MODEL), vec_map),
                pl.BlockSpec((None, 1, D_MODEL), vec_map)]
    args = [x2, g_pre, s1p, sh]
    for s in range(ns):
        in_specs.append(pl.BlockSpec((D_MODEL, tn), functools.partial(lambda s, b, i, j: (0, s * nj + j), s)))
        args.append(w_bf)
    if bias is not None:
        for s in range(ns):
            in_specs.append(pl.BlockSpec((1, tn), functools.partial(lambda s, b, i, j: (0, s * nj + j), s)))
            args.append(bias)
    out_spec = pl.BlockSpec((tm, tn), lambda b, i, j: (b * nt + i, j))
    return pl.pallas_call(
        functools.partial(_inproj_kernel, ns, bias is not None, epi),
        grid=(nb, nt, nj),
        in_specs=in_specs,
        out_specs=[out_spec] * n_out,
        out_shape=[jax.ShapeDtypeStruct((rows, D_INNER), BF16)] * n_out,
        scratch_shapes=[pltpu.VMEM((tm, D_MODEL), BF16)],
        name="inproj%d" % ns,
        compiler_params=_cparams(("parallel", "parallel", "arbitrary")),
    )(*args)


def _conv_epi(ys):
    u, ug, gt = ys
    return [u * _sigmoid(ug), _silu(gt)]


def _na_epi(ys):
    q, k, v, g = ys
    return [q * (HEAD_DIM ** -0.5), k, v, _silu(g)]


def _outproj_kernel(pro, has_bias, nin, *refs):
    in_refs = refs[:nin]
    x_ref, gate_ref, gpost_ref, w_ref = refs[nin:nin + 4]
    b_ref = refs[nin + 4] if has_bias else None
    o_ref = refs[-1]
    tm = o_ref.shape[0]
    rb = min(tm, PIPE_ROWS)

    def project(q):
        return _dot(pro(slice(q * rb, (q + 1) * rb), *in_refs), w_ref[...])

    def finish(q, ox):
        sl = slice(q * rb, (q + 1) * rb)
        if has_bias:
            ox = ox + b_ref[...]
        o_ref[sl, :] = _post(x_ref[sl, :], ox, gate_ref[...], gpost_ref[...])

    _software_pipeline(tm // rb, project, finish)


def _outproj(ins, in_specs_extra, pro, x2, nb, gate, g_post, w_bf, bias, tm):
    rows = x2.shape[0]
    nt = rows // nb // tm
    row_map = lambda b, i: (b * nt + i, 0)
    in_specs = [pl.BlockSpec((tm, D_INNER), row_map) for _ in ins]
    args = list(ins)
    for arr, spec in in_specs_extra:
        in_specs.append(spec)
        args.append(arr)
    nin = len(args)
    in_specs += [pl.BlockSpec((tm, D_MODEL), row_map),
                 pl.BlockSpec((None, 1, D_MODEL), lambda b, i: (b, 0, 0)),
                 pl.BlockSpec((1, D_MODEL), lambda b, i: (0, 0)),
                 pl.BlockSpec((D_INNER, D_MODEL), lambda b, i: (0, 0))]
    args += [x2, gate, g_post, w_bf]
    if bias is not None:
        in_specs.append(pl.BlockSpec((1, D_MODEL), lambda b, i: (0, 0)))
        args.append(bias)
    return pl.pallas_call(
        functools.partial(_outproj_kernel, pro, bias is not None, nin),
        grid=(nb, nt),
        in_specs=in_specs,
        out_specs=pl.BlockSpec((tm, D_MODEL), row_map),
        out_shape=jax.ShapeDtypeStruct((rows, D_MODEL), F32),
        name="outproj_" + pro.__name__.strip("_"),
        compiler_params=_cparams(("parallel", "parallel")),
    )(*args)


def _na_pro(sl, o_ref, sg_ref):
    return o_ref[sl, :] * sg_ref[sl, :]


def _conv_pro(sl, u_ref, sg_ref, lnw_ref, lnb_ref):
    u = u_ref[sl, :].astype(F32)
    mu = jnp.mean(u, axis=-1, keepdims=True)
    d = u - mu
    var = jnp.mean(d * d, axis=-1, keepdims=True)
    y = d * lax.rsqrt(var + LN_EPS) * lnw_ref[...] + lnb_ref[...]
    return (_silu(y) * sg_ref[sl, :].astype(F32)).astype(BF16)


def _dwconv_kernel(tm, rb, nt, u_ref, up_ref, un_ref, dw_ref, db_ref, o_ref, buf_ref):
    i = pl.program_id(1)
    keep_prev = (i > 0).astype(F32)
    keep_next = (i < nt - 1).astype(F32)
    buf_ref[0:CONV_HALO, :] = up_ref[...].astype(F32) * keep_prev
    buf_ref[CONV_HALO:CONV_HALO + tm, :] = u_ref[...].astype(F32)
    buf_ref[CONV_HALO + tm:2 * CONV_HALO + tm, :] = un_ref[...].astype(F32) * keep_next
    dw = dw_ref[...]
    off = CONV_HALO - CONV_WIDTH // 2
    span = rb + 2 * CONV_HALO
    for r0 in range(0, tm, rb):
        acc = jnp.zeros((rb, u_ref.shape[1]), F32) + db_ref[...]
        win = buf_ref[r0:r0 + span, :]
        for s in range(8):
            shifted = win if s == 0 else pltpu.roll(win, span - s, axis=0)
            for a in range(span // 8):
                k = 8 * a + s - off
                if 0 <= k < CONV_WIDTH:
                    acc = acc + shifted[8 * a:8 * a + rb, :] * dw[k:k + 1, :]
        o_ref[r0:r0 + rb, :] = acc.astype(o_ref.dtype)


def _dwconv(u2, nb, dw, dw_b, tm, tc=512, rb=32):
    rows = u2.shape[0]
    t_len = rows // nb
    nt = t_len // tm
    nj = D_INNER // tc
    hb = tm // CONV_HALO
    nhb = rows // CONV_HALO
    prev_map = lambda b, i, j: (jnp.maximum((b * nt + i) * hb - 1, 0), j)
    next_map = lambda b, i, j: (jnp.minimum((b * nt + i + 1) * hb, nhb - 1), j)
    return pl.pallas_call(
        functools.partial(_dwconv_kernel, tm, rb, nt),
        grid=(nb, nt, nj),
        in_specs=[pl.BlockSpec((tm, tc), lambda b, i, j: (b * nt + i, j)),
                  pl.BlockSpec((CONV_HALO, tc), prev_map),
                  pl.BlockSpec((CONV_HALO, tc), next_map),
                  pl.BlockSpec((CONV_WIDTH, tc), lambda b, i, j: (0, j)),
                  pl.BlockSpec((1, tc), lambda b, i, j: (0, j))],
        out_specs=pl.BlockSpec((tm, tc), lambda b, i, j: (b * nt + i, j)),
        out_shape=jax.ShapeDtypeStruct((rows, D_INNER), BF16),
        scratch_shapes=[pltpu.VMEM((tm + 2 * CONV_HALO, tc), F32)],
        name="dwconv",
        compiler_params=_cparams(("parallel", "parallel", "parallel")),
    )(u2, u2, u2, dw, dw_b)


def _stack_heads(q, lane_lo):
    zero = jnp.zeros_like(q)
    return jnp.concatenate([jnp.where(lane_lo, q, zero), jnp.where(lane_lo, zero, q)], axis=0)


def _na_kernel(n_rows, q_ref, k_ref, v_ref, kc_ref, vc_ref, bias_ref, o_ref):
    kr = min(NA_ROWS, n_rows)
    nloc = kr * GRID_W
    lane_lo = lax.broadcasted_iota(jnp.int32, (GRID_W, LANES), 1) < HEAD_DIM
    kc = kc_ref[...]
    vc = vc_ref[...]
    unroll = math.gcd(NA_UNROLL, n_rows)

    def rows(it, carry):
        rr = [it * unroll + u for u in range(unroll)]
        start = [jnp.clip(r - kr // 2, 0, n_rows - kr) for r in rr]
        q0 = [pl.multiple_of(r * GRID_W, GRID_W) for r in rr]
        k0 = [pl.multiple_of(s * GRID_W, GRID_W) for s in start]
        qm = [_stack_heads(q_ref[pl.ds(q, GRID_W), :], lane_lo) for q in q0]
        s_loc = [_dot_t(a, k_ref[pl.ds(k, nloc), :]) for a, k in zip(qm, k0)]
        s_ctx_all = _dot_t(jnp.concatenate(qm, axis=0), kc)
        s_ctx = [s_ctx_all[u * LANES:(u + 1) * LANES] for u in range(unroll)]
        s_loc = [s + bias_ref[r - st, 0] for s, r, st in zip(s_loc, rr, start)]
        m = [jnp.maximum(jnp.max(a, axis=-1, keepdims=True), jnp.max(b, axis=-1, keepdims=True))
             for a, b in zip(s_loc, s_ctx)]
        p_loc = [jnp.exp(a - mm) for a, mm in zip(s_loc, m)]
        p_ctx = [jnp.exp(a - mm) for a, mm in zip(s_ctx, m)]
        den = [jnp.sum(a, axis=-1, keepdims=True) + jnp.sum(b, axis=-1, keepdims=True)
               for a, b in zip(p_loc, p_ctx)]
        o_ctx = _dot(jnp.concatenate([b.astype(BF16) for b in p_ctx], axis=0), vc)
        o2 = [_dot(a.astype(BF16), v_ref[pl.ds(k, nloc), :]) + o_ctx[u * LANES:(u + 1) * LANES]
              for u, (a, k) in enumerate(zip(p_loc, k0))]
        for o, d, q in zip(o2, den, q0):
            o = o / d
            o_ref[pl.ds(q, GRID_W), :] = jnp.where(lane_lo, o[:GRID_W], o[GRID_W:]).astype(o_ref.dtype)
        return carry

    lax.fori_loop(0, n_rows // unroll, rows, 0)


def _na_attention(q, k, v, kc, vc, bias8, nb):
    rows = q.shape[0]
    t_len = rows // nb
    n_rows = t_len // GRID_W
    ctx_len = kc.shape[0] // nb
    nhp = D_INNER // LANES
    kr = min(NA_ROWS, n_rows)
    big = pl.BlockSpec((t_len, LANES), lambda b, h: (b, h))
    cspec = pl.BlockSpec((ctx_len, LANES), lambda b, h: (b, h))
    return pl.pallas_call(
        functools.partial(_na_kernel, n_rows),
        grid=(nb, nhp),
        in_specs=[big, big, big, cspec, cspec,
                  pl.BlockSpec((NA_ROWS, 1, LANES, kr * GRID_W), lambda b, h: (0, h, 0, 0))],
        out_specs=big,
        out_shape=jax.ShapeDtypeStruct((rows, D_INNER), BF16),
        name="na_attn",
        compiler_params=_cparams(("parallel", "parallel")),
    )(q, k, v, kc, vc, bias8)


def _ctx_attn_kernel(q_ref, k_ref, v_ref, o_ref):
    n = q_ref.shape[0]
    lane_lo = lax.broadcasted_iota(jnp.int32, (n, LANES), 1) < HEAD_DIM
    qm = _stack_heads(q_ref[...], lane_lo)
    s = _dot_t(qm, k_ref[...])
    m = jnp.max(s, axis=-1, keepdims=True)
    p = jnp.exp(s - m)
    den = jnp.sum(p, axis=-1, keepdims=True)
    o2 = _dot(p.astype(BF16), v_ref[...]) / den
    o_ref[...] = jnp.where(lane_lo, o2[:n], o2[n:]).astype(o_ref.dtype)


def _ctx_attention(qc, kc, vc, nb):
    ctx_len = qc.shape[0] // nb
    spec = pl.BlockSpec((ctx_len, LANES), lambda b, h: (b, h))
    return pl.pallas_call(
        _ctx_attn_kernel,
        grid=(nb, D_INNER // LANES),
        in_specs=[spec, spec, spec],
        out_specs=spec,
        out_shape=jax.ShapeDtypeStruct(qc.shape, BF16),
        name="ctx_attn",
        compiler_params=_cparams(("parallel", "parallel")),
    )(qc, kc, vc)


def _na_bias_table(rpb, n_rows):
    kr = min(NA_ROWS, n_rows)
    cols = np.arange(GRID_W)
    cstart = np.clip(cols - NA_COLS // 2, 0, GRID_W - NA_COLS)
    jc = np.arange(GRID_W)
    inwin = (jc[None, :] >= cstart[:, None]) & (jc[None, :] < cstart[:, None] + NA_COLS)
    pad = jnp.pad(rpb, ((0, 0), (0, 0), (GRID_W, GRID_W)))
    off = GRID_W + NA_COLS - 1
    toep = jnp.stack([pad[:, :, off - c:off - c + GRID_W] for c in range(GRID_W)], axis=1)
    toep = jnp.where(inwin[None, :, None, :], toep, NEG_BIG)
    flat = toep.reshape(N_HEADS, GRID_W, (2 * NA_ROWS - 1) * GRID_W)
    tab = jnp.stack([flat[:, :, (NA_ROWS - 1 - d) * GRID_W:(NA_ROWS - 1 - d + kr) * GRID_W]
                     for d in range(NA_ROWS)])
    return tab.reshape(NA_ROWS, N_HEADS // 2, 2 * GRID_W, kr * GRID_W).astype(F32)


def _rwkv_prep_kernel(tm, nt, x_ref, xp_ref, xn_ref, g_ref, s_ref, sh_ref, mu_ref,
                      wr_ref, wk_ref, wv_ref, wg_ref, w1_ref, w2_ref, w0_ref,
                      a1_ref, a2_ref, a0_ref, kk_ref, ka_ref, rk_ref, bd_ref,
                      r_o, v_o, kkn_o, bon_o, sg_o, lw_o, kd_o, bb_o,
                      xs_ref, hw_ref, ha_ref):
    i = pl.program_id(1)

    @pl.when(pl.program_id(2) == 0)
    def _():
        g, s1p, sh = g_ref[...], s_ref[...], sh_ref[...]
        h = _modnorm(x_ref[...], g, s1p, sh)
        hp = _modnorm(xp_ref[...], g, s1p, sh)[7:8] * (i > 0).astype(F32)
        hn = _modnorm(xn_ref[...], g, s1p, sh)[0:1] * (i < nt - 1).astype(F32)
        rows = lax.broadcasted_iota(jnp.int32, (tm, 1), 0)
        prev = jnp.where(rows == 0, hp, pltpu.roll(h, 1, axis=0))
        nxt = jnp.where(rows == tm - 1, hn, pltpu.roll(h, tm - 1, axis=0))
        ts = 0.5 * (prev + nxt) - h
        for s in range(6):
            xs_ref[s] = (h + ts * mu_ref[s:s + 1, :]).astype(BF16)
        for z in range(2):
            hw_ref[z] = jnp.tanh(_dot(xs_ref[4], w1_ref[z])).astype(BF16)
            ha_ref[z] = _dot(xs_ref[5], a1_ref[z]).astype(BF16)

    bd = bd_ref[...]
    rb = min(tm, PIPE_ROWS)

    def project(q):
        sl = slice(q * rb, (q + 1) * rb)
        main = [_dot(xs_ref[s, sl, :], w[...]) for s, w in enumerate((wr_ref, wk_ref, wv_ref, wg_ref))]
        lora = [(_dot(hw_ref[z, sl, :], w2_ref[z]), _dot(ha_ref[z, sl, :], a2_ref[z])) for z in range(2)]
        return main, lora

    def finish(q, projected):
        sl = slice(q * rb, (q + 1) * rb)
        (r, k, v, g), lora = projected
        kkh = k * kk_ref[...]
        kk = kkh * lax.rsqrt(_head_sum(kkh * kkh, bd) + 1e-12)
        ksum = jnp.zeros_like(k)
        k_scaled = k * ka_ref[...]
        k_rest = k - k_scaled
        for z in range(2):
            w_raw = w0_ref[z] + lora[z][0]
            lw_o[z, sl, :] = -math.exp(-0.5) * _sigmoid(w_raw)
            a = _sigmoid(a0_ref[z] + lora[z][1])
            kd = k_rest + k_scaled * a
            kd_o[z, sl, :] = kd.astype(BF16)
            bb_o[z, sl, :] = (kk * a).astype(BF16)
            ksum = ksum + kd
        r_o[sl, :] = r.astype(BF16)
        v_o[sl, :] = v.astype(BF16)
        kkn_o[sl, :] = kk.astype(BF16)
        bon_o[sl, :] = (_head_sum(r * ksum * rk_ref[...], bd) * v).astype(BF16)
        sg_o[sl, :] = _silu(g).astype(BF16)

    _software_pipeline(tm // rb, project, finish)


def _rwkv_prep(x2, nb, g_pre, s1p, sh, p, tm):
    rows = x2.shape[0]
    nt = rows // nb // tm
    tn = GROUP_W
    hb = tm // 8
    nhb = rows // 8
    c3 = lambda b, i, j: (0, 0)
    vec = lambda b, i, j: (b, 0, 0)
    colv = pl.BlockSpec((1, tn), lambda b, i, j: (0, j))
    in_specs = [
        pl.BlockSpec((tm, D_MODEL), lambda b, i, j: (b * nt + i, 0)),
        pl.BlockSpec((8, D_MODEL), lambda b, i, j: (jnp.maximum((b * nt + i) * hb - 1, 0), 0)),
        pl.BlockSpec((8, D_MODEL), lambda b, i, j: (jnp.minimum((b * nt + i + 1) * hb, nhb - 1), 0)),
        pl.BlockSpec((1, D_MODEL), c3),
        pl.BlockSpec((None, 1, D_MODEL), vec),
        pl.BlockSpec((None, 1, D_MODEL), vec),
        pl.BlockSpec((6, D_MODEL), c3),
    ]
    args = [x2, x2, x2, g_pre, s1p, sh, p['mu']]
    for s in range(4):
        in_specs.append(pl.BlockSpec((None, D_MODEL, tn), functools.partial(lambda s, b, i, j: (s, 0, j), s)))
        args.append(p['w_in'])
    lora_in = pl.BlockSpec((2, D_MODEL, LORA), lambda b, i, j: (0, 0, 0))
    lora_out = pl.BlockSpec((2, LORA, tn), lambda b, i, j: (0, 0, j))
    lora_b = pl.BlockSpec((2, 1, tn), lambda b, i, j: (0, 0, j))
    in_specs += [lora_in, lora_out, lora_b, lora_in, lora_out, lora_b, colv, colv, colv,
                 pl.BlockSpec((GROUP_W, GROUP_W), c3)]
    args += [p['w1'], p['w2'], p['w0'], p['a1'], p['a2'], p['a0'], p['k_k'], p['k_a'], p['r_k'], p['bd']]
    gm = pl.BlockSpec((None, tm, tn), lambda b, i, j: (j, b * nt + i, 0))
    gm2 = pl.BlockSpec((2, None, tm, tn), lambda b, i, j: (0, j, b * nt + i, 0))
    sh1 = jax.ShapeDtypeStruct((N_GROUPS, rows, tn), BF16)
    sh2 = jax.ShapeDtypeStruct((2, N_GROUPS, rows, tn), BF16)
    return pl.pallas_call(
        functools.partial(_rwkv_prep_kernel, tm, nt),
        grid=(nb, nt, N_GROUPS),
        in_specs=in_specs,
        out_specs=[gm, gm, gm, gm, gm, gm2, gm2, gm2],
        out_shape=[sh1, sh1, sh1, sh1, sh1, jax.ShapeDtypeStruct((2, N_GROUPS, rows, tn), F32), sh2, sh2],
        scratch_shapes=[pltpu.VMEM((6, tm, D_MODEL), BF16),
                        pltpu.VMEM((2, tm, LORA), BF16),
                        pltpu.VMEM((2, tm, LORA), BF16)],
        name="rwkv_prep",
        compiler_params=_cparams(("parallel", "parallel", "arbitrary")),
    )(*args)


def _block_diag(x_bf, bdm):
    return jnp.concatenate([x_bf] * HEADS_PER_PAIR, axis=0) * bdm


def _scan_kernel(r_ref, v_ref, kk_ref, lw_ref, kd_ref, bb_ref, sin_ref,
                 tri_ref, mk_ref, eye_ref, bdm_ref, y_ref, s_ref):
    @pl.when(pl.program_id(1) == 0)
    def _():
        s_ref[...] = sin_ref[...]

    nb = s_ref.shape[0]
    pairs = N_GROUPS * GROUP_W // LANES
    tri = tri_ref[...]
    bdm = bdm_ref[...]
    bdm_f = bdm.astype(F32)

    def each(fn, *lists):
        return [fn(*args) for args in zip(*lists)]

    def bdiag(x):
        return _block_diag(x, bdm)

    def mm(a_list, b_list):
        return each(lambda a, b: _dot(a.astype(BF16), bdiag(b.astype(BF16))), a_list, b_list)

    grp = range(nb * pairs)

    def at_chain(g):
        b, p = divmod(g, pairs)
        return (p // 2, b, slice(None), slice((p % 2) * LANES, (p % 2 + 1) * LANES))

    def ld(ref, g):
        return ref[at_chain(g)]

    def st_at(g):
        grp_i, b, rows_i, lanes_i = at_chain(g)
        return (b, grp_i, rows_i, lanes_i)

    ms, mi = mk_ref[0], mk_ref[1]
    lw = [ld(lw_ref, g) for g in grp]
    cs = each(lambda x: _dot(tri, jnp.concatenate(_split_bf16(x), axis=1)), lw)
    cs = each(lambda x: x[:, :LANES] + x[:, LANES:], cs)
    tot = each(lambda x: jnp.sum(x, axis=0, keepdims=True), lw)
    at = [(-ld(kk_ref, g).astype(F32) * jnp.exp(cs[g] - lw[g])).astype(BF16) for g in grp]
    rt = [(ld(r_ref, g).astype(F32) * jnp.exp(cs[g])).astype(BF16) for g in grp]
    lhs = each(lambda a, b: jnp.concatenate([a, b], axis=0), at, rt)
    def bdiag_t(x):
        return (jnp.concatenate([x] * HEADS_PER_PAIR, axis=0) * bdm_f).T.astype(BF16)

    a12 = [_dot(lhs[g], jnp.concatenate([bdiag_t(ld(bb_ref, g).astype(F32) * jnp.exp(-cs[g])),
                                         bdiag_t(ld(kd_ref, g).astype(F32) * jnp.exp(-cs[g]))], axis=1))
           for g in grp]
    a1 = [a[:, :LANES] for a in a12]
    a2 = [a[:, LANES:] for a in a12]
    n = each(lambda a: a[:CHUNK] * ms, a1)
    a_rb = each(lambda a: (a[CHUNK:] * mi).astype(BF16), a1)
    a_k = each(lambda a: jnp.concatenate([a[:CHUNK] * ms, a[CHUNK:] * mi], axis=0).astype(BF16), a2)
    st = [s_ref[st_at(g)] for g in grp]
    from_s = [_dot(lhs[g], bdiag_t(st[g])) for g in grp]
    from_v = [_dot(a_k[g], bdiag(ld(v_ref, g))) for g in grp]

    base = mk_ref[2]
    pw = each(lambda x: x * base, n)
    t_inv = each(lambda x: eye_ref[...] + x, pw)
    pw = mm(pw, pw)
    both = each(lambda t, p: _dot(jnp.concatenate([t, p], axis=0).astype(BF16), bdiag(p.astype(BF16))),
                t_inv, pw)
    t_inv = each(lambda t, b: t + b[:CHUNK], t_inv, both)
    t_inv = each(jnp.add, t_inv, mm(t_inv, each(lambda b: b[CHUNK:], both)))
    def mm_stack(lhs_lists, rhs_list):
        def one(r, *parts):
            return _dot(jnp.concatenate(parts, axis=0).astype(BF16), bdiag(r.astype(BF16)))
        res = each(one, rhs_list, *lhs_lists)
        return [each(lambda x: x[i * CHUNK:(i + 1) * CHUNK], res) for i in range(len(lhs_lists))]

    x8, z16, z32 = mm_stack([each(lambda x: x * mk_ref[3 + lvl], n) for lvl in range(3)], t_inv)
    d_t, d16, d32 = mm_stack([t_inv, z16, z32], x8)
    t_inv = each(jnp.add, t_inv, d_t)
    x16 = each(jnp.add, z16, d16)
    z32 = each(jnp.add, z32, d32)
    d_t, d32 = mm_stack([t_inv, z32], x16)
    t_inv = each(jnp.add, t_inv, d_t)
    x32 = each(jnp.add, z32, d32)
    t_inv = each(jnp.add, t_inv, mm(t_inv, x32))

    rhs_u = [from_s[g][:CHUNK] + from_v[g][:CHUNK] for g in grp]
    u = mm(t_inv, rhs_u)
    y_rb = mm(a_rb, u)
    for g in grp:
        y = from_s[g][CHUNK:] + y_rb[g] + from_v[g][CHUNK:]
        y_ref[at_chain(g)] = y.astype(y_ref.dtype)

    for g in grp:
        e_rem = jnp.exp(tot[g] - cs[g])
        uv_t = jnp.concatenate([u[g], ld(v_ref, g).astype(F32)], axis=0).T.astype(BF16)
        bk = jnp.concatenate([ld(bb_ref, g).astype(F32) * e_rem, ld(kd_ref, g).astype(F32) * e_rem],
                             axis=0).astype(BF16)
        full = _dot(uv_t, bk) * bdm_f
        upd = full[0:CHUNK]
        for hh in range(1, HEADS_PER_PAIR):
            upd = upd + full[hh * CHUNK:(hh + 1) * CHUNK]
        s_ref[st_at(g)] = st[g] * jnp.exp(tot[g]) + upd


def _scan_consts():
    t = np.arange(CHUNK)
    tt, jj = t[:, None], t[None, :]
    lower = jj <= tt
    strict = jj < tt
    base = strict & (tt // 8 == jj // 8)
    pairs = [(tt // s == jj // s + 1) & (tt // (2 * s) == jj // (2 * s)) for s in (8, 16, 32)]
    fwd = np.stack([strict, lower, base] + pairs)
    both = np.stack([fwd, fwd.transpose(0, 2, 1)]).astype(np.float32)
    masks = np.tile(both, (1, 1, 1, HEADS_PER_PAIR))
    tri = np.stack([lower, lower.T]).astype(np.float32)
    eye = np.tile(np.eye(CHUNK, dtype=np.float32), (1, HEADS_PER_PAIR))
    blk = np.arange(LANES) // HEAD_DIM
    bdm = (blk[:, None] == blk[None, :]).astype(np.float32)
    return jnp.asarray(tri, BF16), jnp.asarray(masks, F32), jnp.asarray(eye, F32), jnp.asarray(bdm, BF16)


def _rwkv_scan(r, v, kk, lw, kd, bb, s_in, nb):
    rows = r.shape[1]
    t_len = rows // nb
    nc = t_len // CHUNK
    tri, masks, eye, bdm = _scan_consts()
    split = lambda a: a.reshape(a.shape[:-2] + (nb, t_len, GROUP_W))
    blk = lambda d, c: c + d * (nc - 1 - 2 * c)
    shared = pl.BlockSpec((N_GROUPS, nb, CHUNK, GROUP_W), lambda d, c: (0, 0, blk(d, c), 0))
    perdir = pl.BlockSpec((None, N_GROUPS, nb, CHUNK, GROUP_W), lambda d, c: (d, 0, 0, blk(d, c), 0))
    sspec = pl.BlockSpec((None, nb, N_GROUPS, HEAD_DIM, GROUP_W), lambda d, c: (d, 0, 0, 0, 0))
    y, s_out = pl.pallas_call(
        _scan_kernel,
        grid=(2, nc),
        in_specs=[shared, shared, shared, perdir, perdir, perdir, sspec,
                  pl.BlockSpec((None, CHUNK, CHUNK), lambda d, c: (d, 0, 0)),
                  pl.BlockSpec((None, 6, CHUNK, LANES), lambda d, c: (d, 0, 0, 0)),
                  pl.BlockSpec((CHUNK, LANES), lambda d, c: (0, 0)),
                  pl.BlockSpec((LANES, LANES), lambda d, c: (0, 0))],
        out_specs=[perdir, sspec],
        out_shape=[jax.ShapeDtypeStruct((2, N_GROUPS, nb, t_len, GROUP_W), BF16),
                   jax.ShapeDtypeStruct(s_in.shape, F32)],
        name="rwkv_scan",
        compiler_params=_cparams(("parallel", "arbitrary")),
    )(split(r), split(v), split(kk), split(lw), split(kd), split(bb), s_in, tri, masks, eye, bdm)
    return y.reshape(2, N_GROUPS, rows, GROUP_W), s_out


def _rwkv_out_kernel(y_ref, bon_ref, sg_ref, lnw_ref, lnb_ref, bd_ref,
                     x_ref, gate_ref, gpost_ref, w_ref, o_ref):
    bd = bd_ref[...] * (1.0 / HEAD_DIM)
    grp = range(N_GROUPS)
    yy = [y_ref[0, g].astype(F32) + y_ref[1, g].astype(F32) for g in grp]
    d = [yy[g] - _head_sum(yy[g], bd) for g in grp]
    var = [_head_sum(d[g] * d[g], bd) for g in grp]
    acc = jnp.zeros(o_ref.shape, F32)
    for g in grp:
        yn = d[g] * lax.rsqrt(var[g] + GN_EPS) * lnw_ref[g] + lnb_ref[g]
        o = (yn + bon_ref[g].astype(F32)) * sg_ref[g].astype(F32)
        acc = acc + _dot(o.astype(BF16), w_ref[g])
    o_ref[...] = _post(x_ref[...], acc, gate_ref[...], gpost_ref[...])


def _rwkv_out(y, bon, sg, p, x2, nb, gate, g_post, tm):
    rows = x2.shape[0]
    nt = rows // nb // tm
    row_map = lambda b, i: (b * nt + i, 0)
    gm = pl.BlockSpec((N_GROUPS, tm, GROUP_W), lambda b, i: (0, b * nt + i, 0))
    pv = pl.BlockSpec((N_GROUPS, 1, GROUP_W), lambda b, i: (0, 0, 0))
    return pl.pallas_call(
        _rwkv_out_kernel,
        grid=(nb, nt),
        in_specs=[pl.BlockSpec((2, N_GROUPS, tm, GROUP_W), lambda b, i: (0, 0, b * nt + i, 0)),
                  gm, gm, pv, pv,
                  pl.BlockSpec((GROUP_W, GROUP_W), lambda b, i: (0, 0)),
                  pl.BlockSpec((tm, D_MODEL), row_map),
                  pl.BlockSpec((None, 1, D_MODEL), lambda b, i: (b, 0, 0)),
                  pl.BlockSpec((1, D_MODEL), lambda b, i: (0, 0)),
                  pl.BlockSpec((N_GROUPS, GROUP_W, D_MODEL), lambda b, i: (0, 0, 0))],
        out_specs=pl.BlockSpec((tm, D_MODEL), row_map),
        out_shape=jax.ShapeDtypeStruct((rows, D_MODEL), F32),
        name="rwkv_out",
        compiler_params=_cparams(("parallel", "parallel")),
    )(y, bon, sg, p['ln_w'], p['ln_b'], p['bd'], x2, gate, g_post, p['w_out'])


def _tile(t_len, pref):
    return pref if t_len % pref == 0 else t_len


def _rwkv_layer(xs, cs, nb, mods, g_pre, g_post, p, need_ctx):
    t_x, t_c = xs.shape[0] // nb, cs.shape[0] // nb
    (s1p_x, sh_x, gate_x), (s1p_c, sh_c, gate_c) = mods
    zeros = jnp.zeros((2, nb, N_GROUPS, HEAD_DIM, GROUP_W), F32)
    rc, vc, kkc, bonc, sgc, lwc, kdc, bbc = _rwkv_prep(cs, nb, g_pre, s1p_c, sh_c, p, _tile(t_c, 256))
    yc, state_c = _rwkv_scan(rc, vc, kkc, lwc, kdc, bbc, zeros, nb)
    rx, vx, kkx, bonx, sgx, lwx, kdx, bbx = _rwkv_prep(xs, nb, g_pre, s1p_x, sh_x, p, _tile(t_x, 1024))
    yx, _ = _rwkv_scan(rx, vx, kkx, lwx, kdx, bbx, state_c, nb)
    xs_new = _rwkv_out(yx, bonx, sgx, p, xs, nb, gate_x, g_post, _tile(t_x, 512))
    cs_new = _rwkv_out(yc, bonc, sgc, p, cs, nb, gate_c, g_post, _tile(t_c, 256)) if need_ctx else cs
    return xs_new, cs_new


def _conv_layer(xs, cs, nb, mods, g_pre, g_post, p, need_ctx):
    def run(x2, mod, t_len):
        s1p, sh, gate = mod
        u, sgate = _inproj(x2, nb, g_pre, s1p, sh, p['w_in'], p['b_in'], 3, 2, _conv_epi, _tile(t_len, 1024))
        uc = _dwconv(u, nb, p['dw'], p['dw_b'], _tile(t_len, 256))
        vspec = pl.BlockSpec((1, D_INNER), lambda b, i: (0, 0))
        return _outproj([uc, sgate], [(p['ln_w'], vspec), (p['ln_b'], vspec)], _conv_pro,
                        x2, nb, gate, g_post, p['w_out'], p['b_out'], _tile(t_len, 512))
    xs_new = run(xs, mods[0], xs.shape[0] // nb)
    cs_new = run(cs, mods[1], cs.shape[0] // nb) if need_ctx else cs
    return xs_new, cs_new


def _na_layer(xs, cs, nb, mods, g_pre, g_post, p, need_ctx):
    t_x, t_c = xs.shape[0] // nb, cs.shape[0] // nb
    (s1p_x, sh_x, gate_x), (s1p_c, sh_c, gate_c) = mods
    q, k, v, sg = _inproj(xs, nb, g_pre, s1p_x, sh_x, p['w_in'], None, 4, 4, _na_epi, _tile(t_x, 1024))
    qc, kc, vc, sgc = _inproj(cs, nb, g_pre, s1p_c, sh_c, p['w_in'], None, 4, 4, _na_epi, _tile(t_c, 256))
    o = _na_attention(q, k, v, kc, vc, _na_bias_table(p['rpb'], t_x // GRID_W), nb)
    xs_new = _outproj([o, sg], [], _na_pro, xs, nb, gate_x, g_post, p['w_out'], None, _tile(t_x, 512))
    cs_new = cs
    if need_ctx:
        oc = _ctx_attention(qc, kc, vc, nb)
        cs_new = _outproj([oc, sgc], [], _na_pro, cs, nb, gate_c, g_post, p['w_out'], None, _tile(t_c, 256))
    return xs_new, cs_new


def kernel(x, c, ctx, c_ctx, ada_w, ada_b, norm_pre, norm_post, rw_mu, rw_w_in, rw_w0, rw_w1, rw_w2, rw_a0, rw_a1, rw_a2, rw_k_k, rw_k_a, rw_r_k, rw_ln_w, rw_ln_b, rw_w_out, cf_w_in, cf_b_in, cf_dw, cf_dw_b, cf_ln_w, cf_ln_b, cf_w_out, cf_b_out, na_w_in, na_rpb, na_w_out):
    nb, t_len, _ = x.shape
    ctx_len = ctx.shape[1]
    depth = ada_w.shape[0]
    xs = x.reshape(nb * t_len, D_MODEL)
    cs = ctx.reshape(nb * ctx_len, D_MODEL)

    cc = jnp.zeros((8, D_MODEL), F32).at[:nb].set(c).at[nb].set(c_ctx)
    mod = _modulation(cc, ada_w, ada_b)
    bd = jnp.asarray(np.kron(np.eye(GROUP_W // HEAD_DIM), np.ones((HEAD_DIM, HEAD_DIM))), BF16)

    for i in range(depth):
        kind, j = i % 3, i // 3
        need_ctx = i < depth - 1
        shift, scale, gate = (mod[i, :, s * D_MODEL:(s + 1) * D_MODEL] for s in range(3))
        mods = []
        for sel in (slice(0, nb), slice(nb, nb + 1)):
            mods.append(tuple(jnp.broadcast_to(m[sel], (nb, D_MODEL)).reshape(nb, 1, D_MODEL)
                              for m in (1.0 + scale, shift, gate)))
        g_pre = norm_pre[i].reshape(1, D_MODEL)
        g_post = norm_post[i].reshape(1, D_MODEL)
        if kind == 0:
            p = dict(mu=rw_mu[j], w_in=rw_w_in[j].astype(BF16),
                     w0=rw_w0[j].reshape(2, 1, D_INNER), w1=rw_w1[j].astype(BF16), w2=rw_w2[j].astype(BF16),
                     a0=rw_a0[j].reshape(2, 1, D_INNER), a1=rw_a1[j].astype(BF16), a2=rw_a2[j].astype(BF16),
                     k_k=rw_k_k[j].reshape(1, D_INNER), k_a=rw_k_a[j].reshape(1, D_INNER),
                     r_k=rw_r_k[j].reshape(1, D_INNER),
                     ln_w=rw_ln_w[j].reshape(N_GROUPS, 1, GROUP_W), ln_b=rw_ln_b[j].reshape(N_GROUPS, 1, GROUP_W),
                     w_out=rw_w_out[j].astype(BF16).reshape(N_GROUPS, GROUP_W, D_MODEL), bd=bd)
            xs, cs = _rwkv_layer(xs, cs, nb, mods, g_pre, g_post, p, need_ctx)
        elif kind == 1:
            p = dict(w_in=cf_w_in[j].astype(BF16), b_in=cf_b_in[j].reshape(1, 3 * D_INNER),
                     dw=cf_dw[j], dw_b=cf_dw_b[j].reshape(1, D_INNER),
                     ln_w=cf_ln_w[j].reshape(1, D_INNER), ln_b=cf_ln_b[j].reshape(1, D_INNER),
                     w_out=cf_w_out[j].astype(BF16), b_out=cf_b_out[j].reshape(1, D_MODEL))
            xs, cs = _conv_layer(xs, cs, nb, mods, g_pre, g_post, p, need_ctx)
        else:
            p = dict(w_in=na_w_in[j].astype(BF16), rpb=na_rpb[j], w_out=na_w_out[j].astype(BF16))
            xs, cs = _na_layer(xs, cs, nb, mods, g_pre, g_post, p, need_ctx)
    return xs.reshape(nb, t_len, D_MODEL)
```

```python
import functools
import math

import jax
import jax.numpy as jnp
import numpy as np
from jax import lax
from jax.experimental import pallas as pl
from jax.experimental.pallas import tpu as pltpu

F32 = jnp.float32
BF16 = jnp.bfloat16

D_MODEL = 1024
D_INNER = 2048
N_HEADS = 32
HEAD_DIM = 64
LORA = 64
GRID_W = 64
CONV_WIDTH = 31
NA_ROWS = 8
NA_COLS = 16
RMS_EPS = 1e-6
LN_EPS = 1e-5
GN_EPS = 6.4e-4

LANES = 128
GROUP_W = 256
N_GROUPS = D_INNER // GROUP_W
HEADS_PER_PAIR = LANES // HEAD_DIM
CHUNK = 64
CONV_HALO = 16
NA_UNROLL = 16
PIPE_ROWS = 256
NEG_BIG = -1e30
VMEM_LIMIT = 56 * 1024 * 1024


def _cparams(sem):
    return pltpu.CompilerParams(dimension_semantics=sem, vmem_limit_bytes=VMEM_LIMIT)


def _dot(a, b):
    return jnp.dot(a, b, preferred_element_type=F32)


def _dot_t(a, b):
    return lax.dot_general(a, b, (((1,), (1,)), ((), ())), preferred_element_type=F32)


def _split_bf16(x):
    hi = x.astype(BF16)
    lo = (x - hi.astype(F32)).astype(BF16)
    return hi, lo


def _sigmoid(x):
    return 1.0 / (1.0 + jnp.exp(-x))


def _silu(x):
    return x * _sigmoid(x)


def _modnorm(x, g, s1p, sh):
    ms = jnp.mean(x * x, axis=-1, keepdims=True)
    return (x * lax.rsqrt(ms + RMS_EPS) * g) * s1p + sh


def _post(xres, ox, gate, gpost):
    ms = jnp.mean(ox * ox, axis=-1, keepdims=True)
    return xres + gate * (ox * lax.rsqrt(ms + RMS_EPS) * gpost)


def _software_pipeline(nblk, project, finish):
    pending = project(0)
    for q in range(1, nblk):
        ahead = project(q)
        finish(q - 1, pending)
        pending = ahead
    finish(nblk - 1, pending)


def _head_sum(x, bd):
    return _dot(x.astype(BF16), bd)


def _mod_kernel(c_ref, w_ref, b_ref, o_ref):
    a_hi, a_lo = _split_bf16(_silu(c_ref[...]))
    w_hi, w_lo = _split_bf16(w_ref[...])
    o_ref[...] = _dot(a_hi, w_hi) + _dot(a_hi, w_lo) + _dot(a_lo, w_hi) + b_ref[...]


def _modulation(cc, ada_w, ada_b):
    depth = ada_w.shape[0]
    tn = 1024
    return pl.pallas_call(
        _mod_kernel,
        grid=(depth, 3 * D_MODEL // tn),
        in_specs=[pl.BlockSpec((8, D_MODEL), lambda l, j: (0, 0)),
                  pl.BlockSpec((None, D_MODEL, tn), lambda l, j: (l, 0, j)),
                  pl.BlockSpec((None, 1, tn), lambda l, j: (l, 0, j))],
        out_specs=pl.BlockSpec((None, 8, tn), lambda l, j: (l, 0, j)),
        out_shape=jax.ShapeDtypeStruct((depth, 8, 3 * D_MODEL), F32),
        name="adaln_mod",
        compiler_params=_cparams(("parallel", "parallel")),
    )(cc, ada_w, ada_b.reshape(depth, 1, 3 * D_MODEL))


def _inproj_kernel(ns, has_bias, epi, x_ref, g_ref, s_ref, sh_ref, *rest):
    w_refs = rest[:ns]
    rest = rest[ns:]
    b_refs = rest[:ns] if has_bias else None
    rest = rest[ns:] if has_bias else rest
    out_refs, hn_ref = rest[:-1], rest[-1]

    @pl.when(pl.program_id(2) == 0)
    def _():
        hn_ref[...] = _modnorm(x_ref[...], g_ref[...], s_ref[...], sh_ref[...]).astype(BF16)

    tm = hn_ref.shape[0]
    rb = min(tm, PIPE_ROWS)

    def project(q):
        hn = hn_ref[q * rb:(q + 1) * rb, :]
        return [_dot(hn, w_refs[s][...]) for s in range(ns)]

    def finish(q, ys):
        if has_bias:
            ys = [y + b_refs[s][...] for s, y in enumerate(ys)]
        for o_ref, o in zip(out_refs, epi(ys)):
            o_ref[q * rb:(q + 1) * rb, :] = o.astype(o_ref.dtype)

    _software_pipeline(tm // rb, project, finish)


def _inproj(x2, nb, g_pre, s1p, sh, w_bf, bias, ns, n_out, epi, tm, tn=512):
    rows = x2.shape[0]
    nt = rows // nb // tm
    nj = D_INNER // tn
    row_map = lambda b, i, j: (b * nt + i, 0)
    vec_map = lambda b, i, j: (b, 0, 0)
    in_specs = [pl.BlockSpec((tm, D_MODEL), row_map),
                pl.BlockSpec((1, D_MODEL), lambda b, i, j: (0, 0)),
                pl.BlockSpec((None, 1, D_MODEL), vec_map),
                pl.BlockSpec((None, 1, D_MODEL), vec_map)]
    args = [x2, g_pre, s1p, sh]
    for s in range(ns):
        in_specs.append(pl.BlockSpec((D_MODEL, tn), functools.partial(lambda s, b, i, j: (0, s * nj + j), s)))
        args.append(w_bf)
    if bias is not None:
        for s in range(ns):
            in_specs.append(pl.BlockSpec((1, tn), functools.partial(lambda s, b, i, j: (0, s * nj + j), s)))
            args.append(bias)
    out_spec = pl.BlockSpec((tm, tn), lambda b, i, j: (b * nt + i, j))
    return pl.pallas_call(
        functools.partial(_inproj_kernel, ns, bias is not None, epi),
        grid=(nb, nt, nj),
        in_specs=in_specs,
        out_specs=[out_spec] * n_out,
        out_shape=[jax.ShapeDtypeStruct((rows, D_INNER), BF16)] * n_out,
        scratch_shapes=[pltpu.VMEM((tm, D_MODEL), BF16)],
        name="inproj%d" % ns,
        compiler_params=_cparams(("parallel", "parallel", "arbitrary")),
    )(*args)


def _conv_epi(ys):
    u, ug, gt = ys
    return [u * _sigmoid(ug), _silu(gt)]


def _na_epi(ys):
    q, k, v, g = ys
    return [q * (HEAD_DIM ** -0.5), k, v, _silu(g)]


def _outproj_kernel(pro, has_bias, nin, *refs):
    in_refs = refs[:nin]
    x_ref, gate_ref, gpost_ref, w_ref = refs[nin:nin + 4]
    b_ref = refs[nin + 4] if has_bias else None
    o_ref = refs[-1]
    tm = o_ref.shape[0]
    rb = min(tm, PIPE_ROWS)

    def project(q):
        return _dot(pro(slice(q * rb, (q + 1) * rb), *in_refs), w_ref[...])

    def finish(q, ox):
        sl = slice(q * rb, (q + 1) * rb)
        if has_bias:
            ox = ox + b_ref[...]
        o_ref[sl, :] = _post(x_ref[sl, :], ox, gate_ref[...], gpost_ref[...])

    _software_pipeline(tm // rb, project, finish)


def _outproj(ins, in_specs_extra, pro, x2, nb, gate, g_post, w_bf, bias, tm):
    rows = x2.shape[0]
    nt = rows // nb // tm
    row_map = lambda b, i: (b * nt + i, 0)
    in_specs = [pl.BlockSpec((tm, D_INNER), row_map) for _ in ins]
    args = list(ins)
    for arr, spec in in_specs_extra:
        in_specs.append(spec)
        args.append(arr)
    nin = len(args)
    in_specs += [pl.BlockSpec((tm, D_MODEL), row_map),
                 pl.BlockSpec((None, 1, D_MODEL), lambda b, i: (b, 0, 0)),
                 pl.BlockSpec((1, D_MODEL), lambda b, i: (0, 0)),
                 pl.BlockSpec((D_INNER, D_MODEL), lambda b, i: (0, 0))]
    args += [x2, gate, g_post, w_bf]
    if bias is not None:
        in_specs.append(pl.BlockSpec((1, D_MODEL), lambda b, i: (0, 0)))
        args.append(bias)
    return pl.pallas_call(
        functools.partial(_outproj_kernel, pro, bias is not None, nin),
        grid=(nb, nt),
        in_specs=in_specs,
        out_specs=pl.BlockSpec((tm, D_MODEL), row_map),
        out_shape=jax.ShapeDtypeStruct((rows, D_MODEL), F32),
        name="outproj_" + pro.__name__.strip("_"),
        compiler_params=_cparams(("parallel", "parallel")),
    )(*args)


def _na_pro(sl, o_ref, sg_ref):
    return o_ref[sl, :] * sg_ref[sl, :]


def _conv_pro(sl, u_ref, sg_ref, lnw_ref, lnb_ref):
    u = u_ref[sl, :].astype(F32)
    mu = jnp.mean(u, axis=-1, keepdims=True)
    d = u - mu
    var = jnp.mean(d * d, axis=-1, keepdims=True)
    y = d * lax.rsqrt(var + LN_EPS) * lnw_ref[...] + lnb_ref[...]
    return (_silu(y) * sg_ref[sl, :].astype(F32)).astype(BF16)


def _dwconv_kernel(tm, rb, nt, u_ref, up_ref, un_ref, dw_ref, db_ref, sh_ref, o_ref, buf_ref):
    i = pl.program_id(1)
    buf_ref[0:CONV_HALO, :] = up_ref[...] * (i > 0).astype(BF16)
    buf_ref[CONV_HALO:CONV_HALO + tm, :] = u_ref[...]
    buf_ref[CONV_HALO + tm:2 * CONV_HALO + tm, :] = un_ref[...] * (i < nt - 1).astype(BF16)
    dw = dw_ref[...]
    sh = sh_ref[...]
    off = CONV_HALO - CONV_WIDTH // 2
    span = rb + 2 * CONV_HALO
    keep = span - 8
    for r0 in range(0, tm, rb):
        acc = jnp.zeros((rb, u_ref.shape[1]), F32) + db_ref[...]
        shifted = _dot(sh, buf_ref[r0:r0 + span, :])
        for s in range(8):
            for a in range(span // 8):
                k = 8 * a + s - off
                if 0 <= k < CONV_WIDTH:
                    acc = acc + shifted[s * keep + 8 * a:s * keep + 8 * a + rb, :] * dw[k:k + 1, :]
        o_ref[r0:r0 + rb, :] = acc.astype(o_ref.dtype)


def _dwconv(u2, nb, dw, dw_b, tm, tc=512, rb=32):
    rows = u2.shape[0]
    t_len = rows // nb
    nt = t_len // tm
    nj = D_INNER // tc
    hb = tm // CONV_HALO
    nhb = rows // CONV_HALO
    prev_map = lambda b, i, j: (jnp.maximum((b * nt + i) * hb - 1, 0), j)
    next_map = lambda b, i, j: (jnp.minimum((b * nt + i + 1) * hb, nhb - 1), j)
    span = rb + 2 * CONV_HALO
    keep = span - 8
    shift = np.zeros((8, keep, span), np.float32)
    for s in range(8):
        shift[s, np.arange(keep), np.arange(keep) + s] = 1.0
    shift = jnp.asarray(shift.reshape(8 * keep, span), BF16)
    return pl.pallas_call(
        functools.partial(_dwconv_kernel, tm, rb, nt),
        grid=(nb, nt, nj),
        in_specs=[pl.BlockSpec((tm, tc), lambda b, i, j: (b * nt + i, j)),
                  pl.BlockSpec((CONV_HALO, tc), prev_map),
                  pl.BlockSpec((CONV_HALO, tc), next_map),
                  pl.BlockSpec((CONV_WIDTH, tc), lambda b, i, j: (0, j)),
                  pl.BlockSpec((1, tc), lambda b, i, j: (0, j)),
                  pl.BlockSpec((8 * keep, span), lambda b, i, j: (0, 0))],
        out_specs=pl.BlockSpec((tm, tc), lambda b, i, j: (b * nt + i, j)),
        out_shape=jax.ShapeDtypeStruct((rows, D_INNER), BF16),
        scratch_shapes=[pltpu.VMEM((tm + 2 * CONV_HALO, tc), BF16)],
        name="dwconv",
        compiler_params=_cparams(("parallel", "parallel", "parallel")),
    )(u2, u2, u2, dw, dw_b, shift)


def _stack_heads(q, lane_lo):
    zero = jnp.zeros_like(q)
    return jnp.concatenate([jnp.where(lane_lo, q, zero), jnp.where(lane_lo, zero, q)], axis=0)


def _na_kernel(n_rows, q_ref, k_ref, v_ref, kc_ref, vc_ref, bias_ref, o_ref):
    kr = min(NA_ROWS, n_rows)
    nloc = kr * GRID_W
    lane_lo = lax.broadcasted_iota(jnp.int32, (GRID_W, LANES), 1) < HEAD_DIM
    kc = kc_ref[...]
    vc = vc_ref[...]
    unroll = math.gcd(NA_UNROLL, n_rows)

    def rows(it, carry):
        rr = [it * unroll + u for u in range(unroll)]
        start = [jnp.clip(r - kr // 2, 0, n_rows - kr) for r in rr]
        q0 = [pl.multiple_of(r * GRID_W, GRID_W) for r in rr]
        k0 = [pl.multiple_of(s * GRID_W, GRID_W) for s in start]
        qm = [_stack_heads(q_ref[pl.ds(q, GRID_W), :], lane_lo) for q in q0]
        s_loc = [_dot_t(a, k_ref[pl.ds(k, nloc), :]) for a, k in zip(qm, k0)]
        s_ctx_all = _dot_t(jnp.concatenate(qm, axis=0), kc)
        s_ctx = [s_ctx_all[u * LANES:(u + 1) * LANES] for u in range(unroll)]
        s_loc = [s + bias_ref[r - st, 0] for s, r, st in zip(s_loc, rr, start)]
        m = [jnp.maximum(jnp.max(a, axis=-1, keepdims=True), jnp.max(b, axis=-1, keepdims=True))
             for a, b in zip(s_loc, s_ctx)]
        p_loc = [jnp.exp(a - mm) for a, mm in zip(s_loc, m)]
        p_ctx = [jnp.exp(a - mm) for a, mm in zip(s_ctx, m)]
        den = [jnp.sum(a, axis=-1, keepdims=True) + jnp.sum(b, axis=-1, keepdims=True)
               for a, b in zip(p_loc, p_ctx)]
        o_ctx = _dot(jnp.concatenate([b.astype(BF16) for b in p_ctx], axis=0), vc)
        o2 = [_dot(a.astype(BF16), v_ref[pl.ds(k, nloc), :]) + o_ctx[u * LANES:(u + 1) * LANES]
              for u, (a, k) in enumerate(zip(p_loc, k0))]
        for o, d, q in zip(o2, den, q0):
            o = o / d
            o_ref[pl.ds(q, GRID_W), :] = jnp.where(lane_lo, o[:GRID_W], o[GRID_W:]).astype(o_ref.dtype)
        return carry

    lax.fori_loop(0, n_rows // unroll, rows, 0)


def _na_attention(q, k, v, kc, vc, bias8, nb):
    rows = q.shape[0]
    t_len = rows // nb
    n_rows = t_len // GRID_W
    ctx_len = kc.shape[0] // nb
    nhp = D_INNER // LANES
    kr = min(NA_ROWS, n_rows)
    big = pl.BlockSpec((t_len, LANES), lambda b, h: (b, h))
    cspec = pl.BlockSpec((ctx_len, LANES), lambda b, h: (b, h))
    return pl.pallas_call(
        functools.partial(_na_kernel, n_rows),
        grid=(nb, nhp),
        in_specs=[big, big, big, cspec, cspec,
                  pl.BlockSpec((NA_ROWS, 1, LANES, kr * GRID_W), lambda b, h: (0, h, 0, 0))],
        out_specs=big,
        out_shape=jax.ShapeDtypeStruct((rows, D_INNER), BF16),
        name="na_attn",
        compiler_params=_cparams(("parallel", "parallel")),
    )(q, k, v, kc, vc, bias8)


def _ctx_attn_kernel(q_ref, k_ref, v_ref, o_ref):
    n = q_ref.shape[0]
    lane_lo = lax.broadcasted_iota(jnp.int32, (n, LANES), 1) < HEAD_DIM
    qm = _stack_heads(q_ref[...], lane_lo)
    s = _dot_t(qm, k_ref[...])
    m = jnp.max(s, axis=-1, keepdims=True)
    p = jnp.exp(s - m)
    den = jnp.sum(p, axis=-1, keepdims=True)
    o2 = _dot(p.astype(BF16), v_ref[...]) / den
    o_ref[...] = jnp.where(lane_lo, o2[:n], o2[n:]).astype(o_ref.dtype)


def _ctx_attention(qc, kc, vc, nb):
    ctx_len = qc.shape[0] // nb
    spec = pl.BlockSpec((ctx_len, LANES), lambda b, h: (b, h))
    return pl.pallas_call(
        _ctx_attn_kernel,
        grid=(nb, D_INNER // LANES),
        in_specs=[spec, spec, spec],
        out_specs=spec,
        out_shape=jax.ShapeDtypeStruct(qc.shape, BF16),
        name="ctx_attn",
        compiler_params=_cparams(("parallel", "parallel")),
    )(qc, kc, vc)


def _na_bias_table(rpb, n_rows):
    kr = min(NA_ROWS, n_rows)
    cols = np.arange(GRID_W)
    cstart = np.clip(cols - NA_COLS // 2, 0, GRID_W - NA_COLS)
    jc = np.arange(GRID_W)
    inwin = (jc[None, :] >= cstart[:, None]) & (jc[None, :] < cstart[:, None] + NA_COLS)
    pad = jnp.pad(rpb, ((0, 0), (0, 0), (GRID_W, GRID_W)))
    off = GRID_W + NA_COLS - 1
    toep = jnp.stack([pad[:, :, off - c:off - c + GRID_W] for c in range(GRID_W)], axis=1)
    toep = jnp.where(inwin[None, :, None, :], toep, NEG_BIG)
    flat = toep.reshape(N_HEADS, GRID_W, (2 * NA_ROWS - 1) * GRID_W)
    tab = jnp.stack([flat[:, :, (NA_ROWS - 1 - d) * GRID_W:(NA_ROWS - 1 - d + kr) * GRID_W]
                     for d in range(NA_ROWS)])
    return tab.reshape(NA_ROWS, N_HEADS // 2, 2 * GRID_W, kr * GRID_W).astype(F32)


def _rwkv_prep_kernel(tm, nt, x_ref, xp_ref, xn_ref, g_ref, s_ref, sh_ref, mu_ref,
                      wr_ref, wk_ref, wv_ref, wg_ref, w1_ref, w2_ref, w0_ref,
                      a1_ref, a2_ref, a0_ref, kk_ref, ka_ref, rk_ref, bd_ref,
                      r_o, v_o, kkn_o, bon_o, sg_o, lw_o, kd_o, bb_o,
                      xs_ref, hw_ref, ha_ref):
    i = pl.program_id(1)

    @pl.when(pl.program_id(2) == 0)
    def _():
        g, s1p, sh = g_ref[...], s_ref[...], sh_ref[...]
        h = _modnorm(x_ref[...], g, s1p, sh)
        hp = _modnorm(xp_ref[...], g, s1p, sh)[7:8] * (i > 0).astype(F32)
        hn = _modnorm(xn_ref[...], g, s1p, sh)[0:1] * (i < nt - 1).astype(F32)
        rows = lax.broadcasted_iota(jnp.int32, (tm, 1), 0)
        prev = jnp.where(rows == 0, hp, pltpu.roll(h, 1, axis=0))
        nxt = jnp.where(rows == tm - 1, hn, pltpu.roll(h, tm - 1, axis=0))
        ts = 0.5 * (prev + nxt) - h
        for s in range(6):
            xs_ref[s] = (h + ts * mu_ref[s:s + 1, :]).astype(BF16)
        for z in range(2):
            hw_ref[z] = jnp.tanh(_dot(xs_ref[4], w1_ref[z])).astype(BF16)
            ha_ref[z] = _dot(xs_ref[5], a1_ref[z]).astype(BF16)

    bd = bd_ref[...]
    rb = min(tm, PIPE_ROWS)

    def project(q):
        sl = slice(q * rb, (q + 1) * rb)
        main = [_dot(xs_ref[s, sl, :], w[...]) for s, w in enumerate((wr_ref, wk_ref, wv_ref, wg_ref))]
        lora = [(_dot(hw_ref[z, sl, :], w2_ref[z]), _dot(ha_ref[z, sl, :], a2_ref[z])) for z in range(2)]
        return main, lora

    def finish(q, projected):
        sl = slice(q * rb, (q + 1) * rb)
        (r, k, v, g), lora = projected
        kkh = k * kk_ref[...]
        kk = kkh * lax.rsqrt(_head_sum(kkh * kkh, bd) + 1e-12)
        ksum = jnp.zeros_like(k)
        k_scaled = k * ka_ref[...]
        k_rest = k - k_scaled
        for z in range(2):
            w_raw = w0_ref[z] + lora[z][0]
            lw_o[z, sl, :] = -math.exp(-0.5) * _sigmoid(w_raw)
            a = _sigmoid(a0_ref[z] + lora[z][1])
            kd = k_rest + k_scaled * a
            kd_o[z, sl, :] = kd.astype(BF16)
            bb_o[z, sl, :] = (kk * a).astype(BF16)
            ksum = ksum + kd
        r_o[sl, :] = r.astype(BF16)
        v_o[sl, :] = v.astype(BF16)
        kkn_o[sl, :] = kk.astype(BF16)
        bon_o[sl, :] = (_head_sum(r * ksum * rk_ref[...], bd) * v).astype(BF16)
        sg_o[sl, :] = _silu(g).astype(BF16)

    _software_pipeline(tm // rb, project, finish)


def _rwkv_prep(x2, nb, g_pre, s1p, sh, p, tm):
    rows = x2.shape[0]
    nt = rows // nb // tm
    tn = GROUP_W
    hb = tm // 8
    nhb = rows // 8
    c3 = lambda b, i, j: (0, 0)
    vec = lambda b, i, j: (b, 0, 0)
    colv = pl.BlockSpec((1, tn), lambda b, i, j: (0, j))
    in_specs = [
        pl.BlockSpec((tm, D_MODEL), lambda b, i, j: (b * nt + i, 0)),
        pl.BlockSpec((8, D_MODEL), lambda b, i, j: (jnp.maximum((b * nt + i) * hb - 1, 0), 0)),
        pl.BlockSpec((8, D_MODEL), lambda b, i, j: (jnp.minimum((b * nt + i + 1) * hb, nhb - 1), 0)),
        pl.BlockSpec((1, D_MODEL), c3),
        pl.BlockSpec((None, 1, D_MODEL), vec),
        pl.BlockSpec((None, 1, D_MODEL), vec),
        pl.BlockSpec((6, D_MODEL), c3),
    ]
    args = [x2, x2, x2, g_pre, s1p, sh, p['mu']]
    for s in range(4):
        in_specs.append(pl.BlockSpec((None, D_MODEL, tn), functools.partial(lambda s, b, i, j: (s, 0, j), s)))
        args.append(p['w_in'])
    lora_in = pl.BlockSpec((2, D_MODEL, LORA), lambda b, i, j: (0, 0, 0))
    lora_out = pl.BlockSpec((2, LORA, tn), lambda b, i, j: (0, 0, j))
    lora_b = pl.BlockSpec((2, 1, tn), lambda b, i, j: (0, 0, j))
    in_specs += [lora_in, lora_out, lora_b, lora_in, lora_out, lora_b, colv, colv, colv,
                 pl.BlockSpec((GROUP_W, GROUP_W), c3)]
    args += [p['w1'], p['w2'], p['w0'], p['a1'], p['a2'], p['a0'], p['k_k'], p['k_a'], p['r_k'], p['bd']]
    gm = pl.BlockSpec((None, tm, tn), lambda b, i, j: (j, b * nt + i, 0))
    gm2 = pl.BlockSpec((2, None, tm, tn), lambda b, i, j: (0, j, b * nt + i, 0))
    sh1 = jax.ShapeDtypeStruct((N_GROUPS, rows, tn), BF16)
    sh2 = jax.ShapeDtypeStruct((2, N_GROUPS, rows, tn), BF16)
    return pl.pallas_call(
        functools.partial(_rwkv_prep_kernel, tm, nt),
        grid=(nb, nt, N_GROUPS),
        in_specs=in_specs,
        out_specs=[gm, gm, gm, gm, gm, gm2, gm2, gm2],
        out_shape=[sh1, sh1, sh1, sh1, sh1, jax.ShapeDtypeStruct((2, N_GROUPS, rows, tn), F32), sh2, sh2],
        scratch_shapes=[pltpu.VMEM((6, tm, D_MODEL), BF16),
                        pltpu.VMEM((2, tm, LORA), BF16),
                        pltpu.VMEM((2, tm, LORA), BF16)],
        name="rwkv_prep",
        compiler_params=_cparams(("parallel", "parallel", "arbitrary")),
    )(*args)


def _block_diag(x_bf, bdm):
    return jnp.concatenate([x_bf] * HEADS_PER_PAIR, axis=0) * bdm


def _scan_kernel(r_ref, v_ref, kk_ref, lw_ref, kd_ref, bb_ref, sin_ref,
                 tri_ref, mk_ref, eye_ref, bdm_ref, y_ref, s_ref):
    @pl.when(pl.program_id(1) == 0)
    def _():
        s_ref[...] = sin_ref[...]

    nb = s_ref.shape[0]
    pairs = N_GROUPS * GROUP_W // LANES
    tri = tri_ref[...]
    bdm = bdm_ref[...]
    bdm_f = bdm.astype(F32)

    def each(fn, *lists):
        return [fn(*args) for args in zip(*lists)]

    def bdiag(x):
        return _block_diag(x, bdm)

    def mm(a_list, b_list):
        return each(lambda a, b: _dot(a.astype(BF16), bdiag(b.astype(BF16))), a_list, b_list)

    grp = range(nb * pairs)

    def at_chain(g):
        b, p = divmod(g, pairs)
        return (p // 2, b, slice(None), slice((p % 2) * LANES, (p % 2 + 1) * LANES))

    def ld(ref, g):
        return ref[at_chain(g)]

    def st_at(g):
        grp_i, b, rows_i, lanes_i = at_chain(g)
        return (b, grp_i, rows_i, lanes_i)

    ms, mi = mk_ref[0], mk_ref[1]
    lw = [ld(lw_ref, g) for g in grp]
    cs = each(lambda x: _dot(tri, jnp.concatenate(_split_bf16(x), axis=1)), lw)
    cs = each(lambda x: x[:, :LANES] + x[:, LANES:], cs)
    tot = each(lambda x: jnp.sum(x, axis=0, keepdims=True), lw)
    at = [(-ld(kk_ref, g).astype(F32) * jnp.exp(cs[g] - lw[g])).astype(BF16) for g in grp]
    rt = [(ld(r_ref, g).astype(F32) * jnp.exp(cs[g])).astype(BF16) for g in grp]
    lhs = each(lambda a, b: jnp.concatenate([a, b], axis=0), at, rt)
    def bdiag_t(x):
        return (jnp.concatenate([x] * HEADS_PER_PAIR, axis=0) * bdm_f).T.astype(BF16)

    a12 = [_dot(lhs[g], jnp.concatenate([bdiag_t(ld(bb_ref, g).astype(F32) * jnp.exp(-cs[g])),
                                         bdiag_t(ld(kd_ref, g).astype(F32) * jnp.exp(-cs[g]))], axis=1))
           for g in grp]
    a1 = [a[:, :LANES] for a in a12]
    a2 = [a[:, LANES:] for a in a12]
    n = each(lambda a: a[:CHUNK] * ms, a1)
    a_rb = each(lambda a: (a[CHUNK:] * mi).astype(BF16), a1)
    a_k = each(lambda a: jnp.concatenate([a[:CHUNK] * ms, a[CHUNK:] * mi], axis=0).astype(BF16), a2)
    st = [s_ref[st_at(g)] for g in grp]
    from_s = [_dot(lhs[g], bdiag_t(st[g])) for g in grp]
    from_v = [_dot(a_k[g], bdiag(ld(v_ref, g))) for g in grp]

    base = mk_ref[2]
    pw = each(lambda x: x * base, n)
    t_inv = each(lambda x: eye_ref[...] + x, pw)
    pw = mm(pw, pw)
    both = each(lambda t, p: _dot(jnp.concatenate([t, p], axis=0).astype(BF16), bdiag(p.astype(BF16))),
                t_inv, pw)
    t_inv = each(lambda t, b: t + b[:CHUNK], t_inv, both)
    t_inv = each(jnp.add, t_inv, mm(t_inv, each(lambda b: b[CHUNK:], both)))
    def mm_stack(lhs_lists, rhs_list):
        def one(r, *parts):
            return _dot(jnp.concatenate(parts, axis=0).astype(BF16), bdiag(r.astype(BF16)))
        res = each(one, rhs_list, *lhs_lists)
        return [each(lambda x: x[i * CHUNK:(i + 1) * CHUNK], res) for i in range(len(lhs_lists))]

    x8, z16, z32 = mm_stack([each(lambda x: x * mk_ref[3 + lvl], n) for lvl in range(3)], t_inv)
    d_t, d16, d32 = mm_stack([t_inv, z16, z32], x8)
    t_inv = each(jnp.add, t_inv, d_t)
    x16 = each(jnp.add, z16, d16)
    z32 = each(jnp.add, z32, d32)
    d_t, d32 = mm_stack([t_inv, z32], x16)
    t_inv = each(jnp.add, t_inv, d_t)
    x32 = each(jnp.add, z32, d32)
    t_inv = each(jnp.add, t_inv, mm(t_inv, x32))

    rhs_u = [from_s[g][:CHUNK] + from_v[g][:CHUNK] for g in grp]
    u = mm(t_inv, rhs_u)
    y_rb = mm(a_rb, u)
    for g in grp:
        y = from_s[g][CHUNK:] + y_rb[g] + from_v[g][CHUNK:]
        y_ref[at_chain(g)] = y.astype(y_ref.dtype)

    for g in grp:
        e_rem = jnp.exp(tot[g] - cs[g])
        uv_t = jnp.concatenate([u[g], ld(v_ref, g).astype(F32)], axis=0).T.astype(BF16)
        bk = jnp.concatenate([ld(bb_ref, g).astype(F32) * e_rem, ld(kd_ref, g).astype(F32) * e_rem],
                             axis=0).astype(BF16)
        full = _dot(uv_t, bk) * bdm_f
        upd = full[0:CHUNK]
        for hh in range(1, HEADS_PER_PAIR):
            upd = upd + full[hh * CHUNK:(hh + 1) * CHUNK]
        s_ref[st_at(g)] = st[g] * jnp.exp(tot[g]) + upd


def _scan_consts():
    t = np.arange(CHUNK)
    tt, jj = t[:, None], t[None, :]
    lower = jj <= tt
    strict = jj < tt
    base = strict & (tt // 8 == jj // 8)
    pairs = [(tt // s == jj // s + 1) & (tt // (2 * s) == jj // (2 * s)) for s in (8, 16, 32)]
    fwd = np.stack([strict, lower, base] + pairs)
    both = np.stack([fwd, fwd.transpose(0, 2, 1)]).astype(np.float32)
    masks = np.tile(both, (1, 1, 1, HEADS_PER_PAIR))
    tri = np.stack([lower, lower.T]).astype(np.float32)
    eye = np.tile(np.eye(CHUNK, dtype=np.float32), (1, HEADS_PER_PAIR))
    blk = np.arange(LANES) // HEAD_DIM
    bdm = (blk[:, None] == blk[None, :]).astype(np.float32)
    return jnp.asarray(tri, BF16), jnp.asarray(masks, F32), jnp.asarray(eye, F32), jnp.asarray(bdm, BF16)


def _rwkv_scan(r, v, kk, lw, kd, bb, s_in, nb):
    rows = r.shape[1]
    t_len = rows // nb
    nc = t_len // CHUNK
    tri, masks, eye, bdm = _scan_consts()
    split = lambda a: a.reshape(a.shape[:-2] + (nb, t_len, GROUP_W))
    blk = lambda d, c: c + d * (nc - 1 - 2 * c)
    shared = pl.BlockSpec((N_GROUPS, nb, CHUNK, GROUP_W), lambda d, c: (0, 0, blk(d, c), 0))
    perdir = pl.BlockSpec((None, N_GROUPS, nb, CHUNK, GROUP_W), lambda d, c: (d, 0, 0, blk(d, c), 0))
    sspec = pl.BlockSpec((None, nb, N_GROUPS, HEAD_DIM, GROUP_W), lambda d, c: (d, 0, 0, 0, 0))
    y, s_out = pl.pallas_call(
        _scan_kernel,
        grid=(2, nc),
        in_specs=[shared, shared, shared, perdir, perdir, perdir, sspec,
                  pl.BlockSpec((None, CHUNK, CHUNK), lambda d, c: (d, 0, 0)),
                  pl.BlockSpec((None, 6, CHUNK, LANES), lambda d, c: (d, 0, 0, 0)),
                  pl.BlockSpec((CHUNK, LANES), lambda d, c: (0, 0)),
                  pl.BlockSpec((LANES, LANES), lambda d, c: (0, 0))],
        out_specs=[perdir, sspec],
        out_shape=[jax.ShapeDtypeStruct((2, N_GROUPS, nb, t_len, GROUP_W), BF16),
                   jax.ShapeDtypeStruct(s_in.shape, F32)],
        name="rwkv_scan",
        compiler_params=_cparams(("parallel", "arbitrary")),
    )(split(r), split(v), split(kk), split(lw), split(kd), split(bb), s_in, tri, masks, eye, bdm)
    return y.reshape(2, N_GROUPS, rows, GROUP_W), s_out


def _rwkv_out_kernel(y_ref, bon_ref, sg_ref, lnw_ref, lnb_ref, bd_ref,
                     x_ref, gate_ref, gpost_ref, w_ref, o_ref):
    bd = bd_ref[...] * (1.0 / HEAD_DIM)
    grp = range(N_GROUPS)
    yy = [y_ref[0, g].astype(F32) + y_ref[1, g].astype(F32) for g in grp]
    d = [yy[g] - _head_sum(yy[g], bd) for g in grp]
    var = [_head_sum(d[g] * d[g], bd) for g in grp]
    acc = jnp.zeros(o_ref.shape, F32)
    for g in grp:
        yn = d[g] * lax.rsqrt(var[g] + GN_EPS) * lnw_ref[g] + lnb_ref[g]
        o = (yn + bon_ref[g].astype(F32)) * sg_ref[g].astype(F32)
        acc = acc + _dot(o.astype(BF16), w_ref[g])
    o_ref[...] = _post(x_ref[...], acc, gate_ref[...], gpost_ref[...])


def _rwkv_out(y, bon, sg, p, x2, nb, gate, g_post, tm):
    rows = x2.shape[0]
    nt = rows // nb // tm
    row_map = lambda b, i: (b * nt + i, 0)
    gm = pl.BlockSpec((N_GROUPS, tm, GROUP_W), lambda b, i: (0, b * nt + i, 0))
    pv = pl.BlockSpec((N_GROUPS, 1, GROUP_W), lambda b, i: (0, 0, 0))
    return pl.pallas_call(
        _rwkv_out_kernel,
        grid=(nb, nt),
        in_specs=[pl.BlockSpec((2, N_GROUPS, tm, GROUP_W), lambda b, i: (0, 0, b * nt + i, 0)),
                  gm, gm, pv, pv,
                  pl.BlockSpec((GROUP_W, GROUP_W), lambda b, i: (0, 0)),
                  pl.BlockSpec((tm, D_MODEL), row_map),
                  pl.BlockSpec((None, 1, D_MODEL), lambda b, i: (b, 0, 0)),
                  pl.BlockSpec((1, D_MODEL), lambda b, i: (0, 0)),
                  pl.BlockSpec((N_GROUPS, GROUP_W, D_MODEL), lambda b, i: (0, 0, 0))],
        out_specs=pl.BlockSpec((tm, D_MODEL), row_map),
        out_shape=jax.ShapeDtypeStruct((rows, D_MODEL), F32),
        name="rwkv_out",
        compiler_params=_cparams(("parallel", "parallel")),
    )(y, bon, sg, p['ln_w'], p['ln_b'], p['bd'], x2, gate, g_post, p['w_out'])


def _tile(t_len, pref):
    return pref if t_len % pref == 0 else t_len


def _rwkv_layer(xs, cs, nb, mods, g_pre, g_post, p, need_ctx):
    t_x, t_c = xs.shape[0] // nb, cs.shape[0] // nb
    (s1p_x, sh_x, gate_x), (s1p_c, sh_c, gate_c) = mods
    zeros = jnp.zeros((2, nb, N_GROUPS, HEAD_DIM, GROUP_W), F32)
    rc, vc, kkc, bonc, sgc, lwc, kdc, bbc = _rwkv_prep(cs, nb, g_pre, s1p_c, sh_c, p, _tile(t_c, 256))
    yc, state_c = _rwkv_scan(rc, vc, kkc, lwc, kdc, bbc, zeros, nb)
    rx, vx, kkx, bonx, sgx, lwx, kdx, bbx = _rwkv_prep(xs, nb, g_pre, s1p_x, sh_x, p, _tile(t_x, 1024))
    yx, _ = _rwkv_scan(rx, vx, kkx, lwx, kdx, bbx, state_c, nb)
    xs_new = _rwkv_out(yx, bonx, sgx, p, xs, nb, gate_x, g_post, _tile(t_x, 512))
    cs_new = _rwkv_out(yc, bonc, sgc, p, cs, nb, gate_c, g_post, _tile(t_c, 256)) if need_ctx else cs
    return xs_new, cs_new


def _conv_layer(xs, cs, nb, mods, g_pre, g_post, p, need_ctx):
    def run(x2, mod, t_len):
        s1p, sh, gate = mod
        u, sgate = _inproj(x2, nb, g_pre, s1p, sh, p['w_in'], p['b_in'], 3, 2, _conv_epi, _tile(t_len, 1024))
        uc = _dwconv(u, nb, p['dw'], p['dw_b'], _tile(t_len, 512))
        vspec = pl.BlockSpec((1, D_INNER), lambda b, i: (0, 0))
        return _outproj([uc, sgate], [(p['ln_w'], vspec), (p['ln_b'], vspec)], _conv_pro,
                        x2, nb, gate, g_post, p['w_out'], p['b_out'], _tile(t_len, 512))
    xs_new = run(xs, mods[0], xs.shape[0] // nb)
    cs_new = run(cs, mods[1], cs.shape[0] // nb) if need_ctx else cs
    return xs_new, cs_new


def _na_layer(xs, cs, nb, mods, g_pre, g_post, p, need_ctx):
    t_x, t_c = xs.shape[0] // nb, cs.shape[0] // nb
    (s1p_x, sh_x, gate_x), (s1p_c, sh_c, gate_c) = mods
    q, k, v, sg = _inproj(xs, nb, g_pre, s1p_x, sh_x, p['w_in'], None, 4, 4, _na_epi, _tile(t_x, 1024))
    qc, kc, vc, sgc = _inproj(cs, nb, g_pre, s1p_c, sh_c, p['w_in'], None, 4, 4, _na_epi, _tile(t_c, 256))
    o = _na_attention(q, k, v, kc, vc, _na_bias_table(p['rpb'], t_x // GRID_W), nb)
    xs_new = _outproj([o, sg], [], _na_pro, xs, nb, gate_x, g_post, p['w_out'], None, _tile(t_x, 512))
    cs_new = cs
    if need_ctx:
        oc = _ctx_attention(qc, kc, vc, nb)
        cs_new = _outproj([oc, sgc], [], _na_pro, cs, nb, gate_c, g_post, p['w_out'], None, _tile(t_c, 256))
    return xs_new, cs_new


def kernel(x, c, ctx, c_ctx, ada_w, ada_b, norm_pre, norm_post, rw_mu, rw_w_in, rw_w0, rw_w1, rw_w2, rw_a0, rw_a1, rw_a2, rw_k_k, rw_k_a, rw_r_k, rw_ln_w, rw_ln_b, rw_w_out, cf_w_in, cf_b_in, cf_dw, cf_dw_b, cf_ln_w, cf_ln_b, cf_w_out, cf_b_out, na_w_in, na_rpb, na_w_out):
    nb, t_len, _ = x.shape
    ctx_len = ctx.shape[1]
    depth = ada_w.shape[0]
    xs = x.reshape(nb * t_len, D_MODEL)
    cs = ctx.reshape(nb * ctx_len, D_MODEL)

    cc = jnp.zeros((8, D_MODEL), F32).at[:nb].set(c).at[nb].set(c_ctx)
    mod = _modulation(cc, ada_w, ada_b)
    bd = jnp.asarray(np.kron(np.eye(GROUP_W // HEAD_DIM), np.ones((HEAD_DIM, HEAD_DIM))), BF16)

    for i in range(depth):
        kind, j = i % 3, i // 3
        need_ctx = i < depth - 1
        shift, scale, gate = (mod[i, :, s * D_MODEL:(s + 1) * D_MODEL] for s in range(3))
        mods = []
        for sel in (slice(0, nb), slice(nb, nb + 1)):
            mods.append(tuple(jnp.broadcast_to(m[sel], (nb, D_MODEL)).reshape(nb, 1, D_MODEL)
                              for m in (1.0 + scale, shift, gate)))
        g_pre = norm_pre[i].reshape(1, D_MODEL)
        g_post = norm_post[i].reshape(1, D_MODEL)
        if kind == 0:
            p = dict(mu=rw_mu[j], w_in=rw_w_in[j].astype(BF16),
                     w0=rw_w0[j].reshape(2, 1, D_INNER), w1=rw_w1[j].astype(BF16), w2=rw_w2[j].astype(BF16),
                     a0=rw_a0[j].reshape(2, 1, D_INNER), a1=rw_a1[j].astype(BF16), a2=rw_a2[j].astype(BF16),
                     k_k=rw_k_k[j].reshape(1, D_INNER), k_a=rw_k_a[j].reshape(1, D_INNER),
                     r_k=rw_r_k[j].reshape(1, D_INNER),
                     ln_w=rw_ln_w[j].reshape(N_GROUPS, 1, GROUP_W), ln_b=rw_ln_b[j].reshape(N_GROUPS, 1, GROUP_W),
                     w_out=rw_w_out[j].astype(BF16).reshape(N_GROUPS, GROUP_W, D_MODEL), bd=bd)
            xs, cs = _rwkv_layer(xs, cs, nb, mods, g_pre, g_post, p, need_ctx)
        elif kind == 1:
            p = dict(w_in=cf_w_in[j].astype(BF16), b_in=cf_b_in[j].reshape(1, 3 * D_INNER),
                     dw=cf_dw[j], dw_b=cf_dw_b[j].reshape(1, D_INNER),
                     ln_w=cf_ln_w[j].reshape(1, D_INNER), ln_b=cf_ln_b[j].reshape(1, D_INNER),
                     w_out=cf_w_out[j].astype(BF16), b_out=cf_b_out[j].reshape(1, D_MODEL))
            xs, cs = _conv_layer(xs, cs, nb, mods, g_pre, g_post, p, need_ctx)
        else:
            p = dict(w_in=na_w_in[j].astype(BF16), rpb=na_rpb[j], w_out=na_w_out[j].astype(BF16))
            xs, cs = _na_layer(xs, cs, nb, mods, g_pre, g_post, p, need_ctx)
    return xs.reshape(nb, t_len, D_MODEL)
```
